```python
import math
import jax, jax.numpy as jnp
from jax import lax
import numpy as np


D_MODEL = 2048
BATCH = 4
SEQ = 2048
DEPTH = 2
DEC_BATCH = 8
DEC_SEQ = 4096
PAST_LEN = 128

F32 = jnp.float32
N_MIXERS = 2
N_RET_LAYERS = (DEPTH + 1) // 2
N_DIFF_LAYERS = DEPTH // 2
ALPHA = (2 * DEPTH) ** 0.25
BETA = (8 * DEPTH) ** -0.25
LN_EPS = 1e-5

RET_HEADS = 8
RET_DK = D_MODEL // RET_HEADS
RET_DV = 2 * RET_DK
RET_CHUNK = 128
RET_THETA = 10000.0

DIFF_HEADS = 8
DIFF_DH = D_MODEL // (2 * DIFF_HEADS)
DIFF_DV = 2 * DIFF_DH
ROPE_THETA = 500000.0
ROPE_DIMS = DIFF_DH // 4
Q_BLOCK = 128

N_EXPERTS = 64
TOP_K = 8
N_GROUPS = 8
TOPK_GROUPS = 4
EXPERT_HIDDEN = 512
SHARED_HIDDEN = 512
ROUTED_SCALE = 2.5
EXPERT_BLOCK = 128

kernel_name = "hybrid_retention_diffattn_moe_encoder"


def layer_norm(x, g, b):
    xf = x.astype(F32)
    mu = jnp.mean(xf, axis=-1, keepdims=True)
    var = jnp.mean(jnp.square(xf - mu), axis=-1, keepdims=True)
    return ((xf - mu) * lax.rsqrt(var + LN_EPS) * g + b).astype(x.dtype)


def rms_norm(x, g):
    xf = x.astype(F32)
    return (xf * lax.rsqrt(jnp.mean(jnp.square(xf), axis=-1, keepdims=True) + LN_EPS) * g).astype(x.dtype)


def apply_rope(x, theta, rot_dims):
    s = x.shape[1]
    half = rot_dims // 2
    inv = theta ** (-jnp.arange(half, dtype=F32) / half)
    ang = jnp.arange(s, dtype=F32)[:, None] * inv[None, :]
    shape = (s,) + (1,) * (x.ndim - 3) + (half,)
    cos = jnp.cos(ang).reshape(shape)
    sin = jnp.sin(ang).reshape(shape)
    xr = x[..., :rot_dims].astype(F32)
    x1, x2 = xr[..., :half], xr[..., half:]
    rot = jnp.concatenate([x1 * cos - x2 * sin, x2 * cos + x1 * sin], axis=-1).astype(x.dtype)
    return jnp.concatenate([rot, x[..., rot_dims:]], axis=-1)


def chunk_retention(q, k, v, log_gamma, inclusive):
    b, s, h, dk = q.shape
    dv = v.shape[-1]
    c = RET_CHUNK
    n = s // c
    pos = jnp.arange(c, dtype=F32)
    diff = pos[:, None] - pos[None, :]
    mask = (diff >= 0) if inclusive else (diff > 0)
    decay_in = jnp.where(mask, jnp.exp(jnp.where(mask, diff, 0.0)[None] * log_gamma[:, None, None]), 0.0)
    qc = q.reshape(b, n, c, h, dk)
    kc = k.reshape(b, n, c, h, dk)
    vc = v.reshape(b, n, c, h, dv)
    att = jnp.einsum('bnihd,bnjhd->bnhij', qc, kc, preferred_element_type=F32) * decay_in
    inner = jnp.einsum('bnhij,bnjhe->bnihe', att, vc.astype(F32))
    q_decay = jnp.exp((pos + 1.0)[None, :] * log_gamma[:, None])
    k_decay = jnp.exp((c - 1.0 - pos)[None, :] * log_gamma[:, None])
    chunk_decay = jnp.exp(c * log_gamma)

    def step(state, inp):
        qi, ki, vi = inp
        cross = jnp.einsum('bihd,hi,bhde->bihe', qi.astype(F32), q_decay, state)
        state = state * chunk_decay[None, :, None, None] + jnp.einsum(
            'bjhd,hj,bjhe->bhde', ki.astype(F32), k_decay, vi.astype(F32))
        return state, cross

    state0 = jnp.zeros((b, h, dk, dv), F32)
    _, cross = lax.scan(step, state0, (qc.transpose(1, 0, 2, 3, 4), kc.transpose(1, 0, 2, 3, 4), vc.transpose(1, 0, 2, 3, 4)))
    return (inner + cross.transpose(1, 0, 2, 3, 4)).reshape(b, s, h, dv)


def retention_mixer(x, w_in, decay_f, decay_b, w_out):
    b, s, _ = x.shape
    qk = RET_HEADS * RET_DK
    vv = RET_HEADS * RET_DV
    proj = jnp.einsum('bsd,de->bse', x, w_in)
    q = proj[..., :qk].reshape(b, s, RET_HEADS, RET_DK)
    k = proj[..., qk:2 * qk].reshape(b, s, RET_HEADS, RET_DK)
    v = proj[..., 2 * qk:2 * qk + vv].reshape(b, s, RET_HEADS, RET_DV)
    g = proj[..., 2 * qk + vv:]
    q = apply_rope(q, RET_THETA, RET_DK)
    k = apply_rope(k, RET_THETA, RET_DK) * (RET_DK ** -0.5)
    lg_f = -jax.nn.softplus(-decay_f.astype(F32))
    lg_b = -jax.nn.softplus(-decay_b.astype(F32))
    o_f = chunk_retention(q, k, v, lg_f, True)
    o_b = chunk_retention(q[:, ::-1], k[:, ::-1], v[:, ::-1], lg_b, False)[:, ::-1]
    o = o_f + o_b
    mu = jnp.mean(o, axis=-1, keepdims=True)
    var = jnp.mean(jnp.square(o - mu), axis=-1, keepdims=True)
    o = ((o - mu) * lax.rsqrt(var + LN_EPS)).reshape(b, s, vv).astype(x.dtype)
    return jnp.einsum('bse,ed->bsd', jax.nn.silu(g) * o, w_out)


def diff_attention(x, w_in, lam_q1, lam_k1, lam_q2, lam_k2, subln_g, w_out, lambda_init):
    b, s, _ = x.shape
    nq = DIFF_HEADS * 2 * DIFF_DH
    proj = jnp.einsum('bsd,de->bse', x, w_in)
    q = proj[..., :nq].reshape(b, s, DIFF_HEADS, 2, DIFF_DH)
    k = proj[..., nq:2 * nq].reshape(b, s, DIFF_HEADS, 2, DIFF_DH)
    v = proj[..., 2 * nq:].reshape(b, s, DIFF_HEADS, DIFF_DV)
    q = apply_rope(q, ROPE_THETA, ROPE_DIMS) * (DIFF_DH ** -0.5)
    k = apply_rope(k, ROPE_THETA, ROPE_DIMS)
    lam = (jnp.exp(jnp.sum(lam_q1.astype(F32) * lam_k1.astype(F32)))
           - jnp.exp(jnp.sum(lam_q2.astype(F32) * lam_k2.astype(F32))) + lambda_init)
    nb = s // Q_BLOCK
    qb = q.reshape(b, nb, Q_BLOCK, DIFF_HEADS, 2, DIFF_DH).transpose(1, 0, 2, 3, 4, 5)

    def block(qi):
        sc = jnp.einsum('bqhcd,bkhcd->bhcqk', qi, k, preferred_element_type=F32)
        p = jax.nn.softmax(sc, axis=-1)
        a = (p[:, :, 0] - lam * p[:, :, 1]).astype(v.dtype)
        return jnp.einsum('bhqk,bkhe->bqhe', a, v)

    o = lax.map(block, qb)
    o = o.transpose(1, 0, 2, 3, 4).reshape(b, s, DIFF_HEADS, DIFF_DV)
    o = (rms_norm(o, subln_g) * (1.0 - lambda_init)).astype(x.dtype)
    return jnp.einsum('bse,ed->bsd', o.reshape(b, s, DIFF_HEADS * DIFF_DV), w_out)


def shared_expert(h, w_gu, w_down):
    gu = h @ w_gu
    return (jax.nn.silu(gu[:, :SHARED_HIDDEN]) * gu[:, SHARED_HIDDEN:]) @ w_down


def routed_experts(h, w_router, b_router, w_gu, w_down):
    t, d = h.shape
    scores = jax.nn.sigmoid(jnp.einsum('td,de->te', h, w_router, preferred_element_type=F32))
    biased = scores + b_router.astype(F32)
    grp = biased.reshape(t, N_GROUPS, N_EXPERTS // N_GROUPS)
    grp_score = jnp.sum(lax.top_k(grp, 2)[0], axis=-1)
    top_groups = lax.top_k(grp_score, TOPK_GROUPS)[1]
    gmask = jnp.any(top_groups[:, :, None] == jnp.arange(N_GROUPS)[None, None, :], axis=1)
    emask = jnp.repeat(gmask, N_EXPERTS // N_GROUPS, axis=1)
    idx = lax.top_k(jnp.where(emask, biased, -jnp.inf), TOP_K)[1]
    gate = jnp.take_along_axis(scores, idx, axis=1)
    gate = gate / jnp.sum(gate, axis=-1, keepdims=True) * ROUTED_SCALE

    n_assign = t * TOP_K
    flat_e = idx.reshape(n_assign)
    flat_t = jnp.arange(n_assign, dtype=jnp.int32) // TOP_K
    flat_w = gate.reshape(n_assign).astype(h.dtype)
    order = jnp.argsort(flat_e)
    se = flat_e[order]
    counts = jnp.bincount(flat_e, length=N_EXPERTS)
    padded = (counts + EXPERT_BLOCK - 1) // EXPERT_BLOCK * EXPERT_BLOCK
    pad_end = jnp.cumsum(padded)
    pad_start = pad_end - padded
    grp_start = jnp.cumsum(counts) - counts
    dest = pad_start[se] + jnp.arange(n_assign) - grp_start[se]
    n_blocks = (n_assign + N_EXPERTS * (EXPERT_BLOCK - 1) + EXPERT_BLOCK - 1) // EXPERT_BLOCK
    n_slots = n_blocks * EXPERT_BLOCK
    slot_tok = jnp.full((n_slots,), t, jnp.int32).at[dest].set(flat_t[order])
    slot_w = jnp.zeros((n_slots,), h.dtype).at[dest].set(flat_w[order])
    block_e = jnp.minimum(jnp.searchsorted(pad_end, jnp.arange(n_blocks) * EXPERT_BLOCK, side='right'), N_EXPERTS - 1)
    h_pad = jnp.concatenate([h, jnp.zeros((1, d), h.dtype)], axis=0)

    def step(acc, blk):
        tok, wt, e = blk
        xb = h_pad[tok]
        gu = xb @ w_gu[e]
        yb = (jax.nn.silu(gu[:, :EXPERT_HIDDEN]) * gu[:, EXPERT_HIDDEN:]) @ w_down[e]
        return acc.at[tok].add(yb * wt[:, None]), None

    acc0 = jnp.zeros((t + 1, d), h.dtype)
    acc, _ = lax.scan(step, acc0, (slot_tok.reshape(n_blocks, EXPERT_BLOCK), slot_w.reshape(n_blocks, EXPERT_BLOCK), block_e))
    return acc[:t]


def encoder_trunk(x, ret_w_in, ret_decay_f, ret_decay_b, ret_w_out, diff_w_in, diff_lam_q1, diff_lam_k1,
                  diff_lam_q2, diff_lam_k2, diff_subln_g, diff_w_out, ln_mix_g, ln_mix_b, router_w, router_b,
                  exp_w_gu, exp_w_down, shared_w_gu, shared_w_down, ln_ffn_g, ln_ffn_b):
    for i in range(DEPTH):
        j = i // N_MIXERS
        if i % N_MIXERS == 0:
            h = retention_mixer(x, ret_w_in[j], ret_decay_f[j], ret_decay_b[j], ret_w_out[j])
        else:
            lambda_init = 0.8 - 0.6 * math.exp(-0.3 * i)
            h = diff_attention(x, diff_w_in[j], diff_lam_q1[j], diff_lam_k1[j], diff_lam_q2[j], diff_lam_k2[j],
                               diff_subln_g[j], diff_w_out[j], lambda_init)
        x = layer_norm(ALPHA * x + h, ln_mix_g[i], ln_mix_b[i])
        b, s, d = x.shape
        xt = x.reshape(b * s, d)
        h = shared_expert(xt, shared_w_gu[i], shared_w_down[i]) + routed_experts(
            xt, router_w[i], router_b[i], exp_w_gu[i], exp_w_down[i])
        x = layer_norm(ALPHA * x + h.reshape(b, s, d), ln_ffn_g[i], ln_ffn_b[i])
    return x


def _normal(key, shape, scale):
    return jax.random.normal(key, shape, F32) * scale


def setup_inputs(seed: int = 0) -> dict:
    key = jax.random.key(seed)
    ks = jax.random.split(key, 32)
    D = D_MODEL
    nA, nB = N_RET_LAYERS, N_DIFF_LAYERS
    qk = RET_HEADS * RET_DK
    vv = RET_HEADS * RET_DV
    nq = DIFF_HEADS * 2 * DIFF_DH
    nv = DIFF_HEADS * DIFF_DV
    x_prompt = _normal(ks[0], (BATCH, SEQ, D), 1.0)
    x_sample = _normal(ks[1], (DEC_BATCH, DEC_SEQ, D), 1.0)
    ret_w_in = jnp.concatenate([_normal(ks[2], (nA, D, 2 * qk), D ** -0.5),
                                _normal(ks[3], (nA, D, vv), BETA * D ** -0.5),
                                _normal(ks[4], (nA, D, vv), D ** -0.5)], axis=-1)
    base = jnp.log(2.0 ** (5.0 + jnp.arange(RET_HEADS, dtype=F32)) - 1.0)
    ret_decay_f = base[None] + _normal(ks[5], (nA, RET_HEADS), 0.1)
    ret_decay_b = base[None] + _normal(ks[6], (nA, RET_HEADS), 0.1)
    ret_w_out = _normal(ks[7], (nA, vv, D), BETA * vv ** -0.5)
    diff_w_in = jnp.concatenate([_normal(ks[8], (nB, D, 2 * nq), D ** -0.5),
                                 _normal(ks[9], (nB, D, nv), BETA * D ** -0.5)], axis=-1)
    diff_lam_q1 = _normal(ks[10], (nB, DIFF_DH), 0.1)
    diff_lam_k1 = _normal(ks[11], (nB, DIFF_DH), 0.1)
    diff_lam_q2 = _normal(ks[12], (nB, DIFF_DH), 0.1)
    diff_lam_k2 = _normal(ks[13], (nB, DIFF_DH), 0.1)
    diff_subln_g = 1.0 + _normal(ks[14], (nB, DIFF_DV), 0.02)
    diff_w_out = _normal(ks[15], (nB, nv, D), BETA * nv ** -0.5)
    ln_mix_g = 1.0 + _normal(ks[16], (DEPTH, D), 0.02)
    ln_mix_b = _normal(ks[17], (DEPTH, D), 0.02)
    router_w = _normal(ks[18], (DEPTH, D, N_EXPERTS), D ** -0.5)
    router_b = _normal(ks[19], (DEPTH, N_EXPERTS), 0.01)
    exp_w_gu = _normal(ks[20], (DEPTH, N_EXPERTS, D, 2 * EXPERT_HIDDEN), D ** -0.5)
    exp_w_down = _normal(ks[21], (DEPTH, N_EXPERTS, EXPERT_HIDDEN, D), BETA * EXPERT_HIDDEN ** -0.5)
    shared_w_gu = _normal(ks[22], (DEPTH, D, 2 * SHARED_HIDDEN), D ** -0.5)
    shared_w_down = _normal(ks[23], (DEPTH, SHARED_HIDDEN, D), BETA * SHARED_HIDDEN ** -0.5)
    ln_ffn_g = 1.0 + _normal(ks[24], (DEPTH, D), 0.02)
    ln_ffn_b = _normal(ks[25], (DEPTH, D), 0.02)
    return {"x_prompt": x_prompt, "x_sample": x_sample, "ret_w_in": ret_w_in, "ret_decay_f": ret_decay_f,
            "ret_decay_b": ret_decay_b, "ret_w_out": ret_w_out, "diff_w_in": diff_w_in,
            "diff_lam_q1": diff_lam_q1, "diff_lam_k1": diff_lam_k1, "diff_lam_q2": diff_lam_q2,
            "diff_lam_k2": diff_lam_k2, "diff_subln_g": diff_subln_g, "diff_w_out": diff_w_out,
            "ln_mix_g": ln_mix_g, "ln_mix_b": ln_mix_b, "router_w": router_w, "router_b": router_b,
            "exp_w_gu": exp_w_gu, "exp_w_down": exp_w_down, "shared_w_gu": shared_w_gu,
            "shared_w_down": shared_w_down, "ln_ffn_g": ln_ffn_g, "ln_ffn_b": ln_ffn_b}


def reference(x_prompt, x_sample, ret_w_in, ret_decay_f, ret_decay_b, ret_w_out, diff_w_in, diff_lam_q1,
              diff_lam_k1, diff_lam_q2, diff_lam_k2, diff_subln_g, diff_w_out, ln_mix_g, ln_mix_b, router_w,
              router_b, exp_w_gu, exp_w_down, shared_w_gu, shared_w_down, ln_ffn_g, ln_ffn_b):
    y_prompt = encoder_trunk(x_prompt, ret_w_in, ret_decay_f, ret_decay_b, ret_w_out, diff_w_in, diff_lam_q1,
                             diff_lam_k1, diff_lam_q2, diff_lam_k2, diff_subln_g, diff_w_out, ln_mix_g, ln_mix_b,
                             router_w, router_b, exp_w_gu, exp_w_down, shared_w_gu, shared_w_down, ln_ffn_g, ln_ffn_b)
    y_sample = encoder_trunk(x_sample, ret_w_in, ret_decay_f, ret_decay_b, ret_w_out, diff_w_in, diff_lam_q1,
                             diff_lam_k1, diff_lam_q2, diff_lam_k2, diff_subln_g, diff_w_out, ln_mix_g, ln_mix_b,
                             router_w, router_b, exp_w_gu, exp_w_down, shared_w_gu, shared_w_down, ln_ffn_g, ln_ffn_b)
    return (y_prompt, y_sample)
```

```python
import functools
import math

import jax
import jax.numpy as jnp
from jax import lax
from jax.experimental import pallas as pl
from jax.experimental.pallas import tpu as pltpu

F32 = jnp.float32
BF16 = jnp.bfloat16
I32 = jnp.int32

D_MODEL = 2048
DEPTH = 2
ALPHA = (2 * DEPTH) ** 0.25
LN_EPS = 1e-5
RET_HEADS = 8
RET_DK = D_MODEL // RET_HEADS
RET_DV = 2 * RET_DK
RET_THETA = 10000.0
DIFF_HEADS = 8
DIFF_DH = D_MODEL // (2 * DIFF_HEADS)
DIFF_DV = 2 * DIFF_DH
ROPE_THETA = 500000.0
ROPE_DIMS = DIFF_DH // 4
N_EXPERTS = 64
TOP_K = 8
N_GROUPS = 8
TOPK_GROUPS = 4
GROUP_SIZE = N_EXPERTS // N_GROUPS
EXPERT_HIDDEN = 512
SHARED_HIDDEN = 512
ROUTED_SCALE = 2.5

LANES = 128
SUBLANES = 8
VMEM_LIMIT = 48 * 1024 * 1024

RET_CHUNK = 256
EXPERT_ROWS = 256
ROW_TILE = 128
NEG_INF = float("-inf")


def _cparams(sem):
    return pltpu.CompilerParams(dimension_semantics=sem, vmem_limit_bytes=VMEM_LIMIT)


def _pick_tile(n, prefs):
    for p in prefs:
        if n % p == 0:
            return p
    raise ValueError(f"no tile in {prefs} divides {n}")


def _seg_local_block(i, segs_blocks):
    val = None
    for first, per in segs_blocks:
        loc = lax.rem(i - first, per)
        val = loc if val is None else jnp.where(i >= first, loc, val)
    return val


def _proj_kernel(x_ref, w_ref, t0_ref, t1_ref, t2_ref, o_ref, *, mode, n_q_tiles, n_qk_tiles, q_scale, k_scale):
    j = pl.program_id(1)
    acc = jnp.dot(x_ref[...], w_ref[...], preferred_element_type=F32)
    tn = acc.shape[1]

    @pl.when(j >= n_qk_tiles)
    def _plain():
        o_ref[...] = acc.astype(o_ref.dtype)

    @pl.when(j < n_qk_tiles)
    def _rope():
        scale = jnp.where(j < n_q_tiles, q_scale, k_scale)
        if mode == "ret":
            cos = t0_ref[...]
            sin = t1_ref[...]
            half = RET_DK // 2
            for hs in range(tn // RET_DK):
                lo = hs * RET_DK
                x1 = acc[:, lo:lo + half]
                x2 = acc[:, lo + half:lo + RET_DK]
                o_ref[:, lo:lo + half] = ((x1 * cos - x2 * sin) * scale).astype(o_ref.dtype)
                o_ref[:, lo + half:lo + RET_DK] = ((x2 * cos + x1 * sin) * scale).astype(o_ref.dtype)
        else:
            c = t0_ref[...]
            s_up = t1_ref[...]
            s_dn = t2_ref[...]
            half = ROPE_DIMS // 2
            for g in range(tn // DIFF_DH):
                lo = g * DIFF_DH
                seg = acc[:, lo:lo + DIFF_DH]
                rot = seg * c + pltpu.roll(seg, half, 1) * s_up + pltpu.roll(seg, DIFF_DH - half, 1) * s_dn
                o_ref[:, lo:lo + DIFF_DH] = (rot * scale).astype(o_ref.dtype)


def _proj(x, w, tables, segs, *, mode, n_q_cols, n_qk_cols, q_scale, k_scale):
    m, k = x.shape
    n = w.shape[1]
    seg_len = [s for _, s in segs]
    tm = _pick_tile(math.gcd(*seg_len), (1024, 512, 256))
    tn = 1024
    segs_blocks = [(start // tm, s // tm) for start, s in segs]
    tw = tables[0].shape[1]
    tab_spec = pl.BlockSpec((tm, tw), lambda i, j: (_seg_local_block(i, segs_blocks), 0))
    kern = functools.partial(_proj_kernel, mode=mode, n_q_tiles=n_q_cols // tn, n_qk_tiles=n_qk_cols // tn,
                             q_scale=q_scale, k_scale=k_scale)
    return pl.pallas_call(
        kern,
        grid=(m // tm, n // tn),
        in_specs=[pl.BlockSpec((tm, k), lambda i, j: (i, 0)),
                  pl.BlockSpec((k, tn), lambda i, j: (0, j)),
                  tab_spec, tab_spec, tab_spec],
        out_specs=pl.BlockSpec((tm, tn), lambda i, j: (i, j)),
        out_shape=jax.ShapeDtypeStruct((m, n), BF16),
        compiler_params=_cparams(("parallel", "arbitrary")),
        name=f"proj_{mode}",
    )(x, w, *tables)


def _matmul_kernel(x_ref, w_ref, o_ref):
    o_ref[...] = jnp.dot(x_ref[...], w_ref[...], preferred_element_type=F32).astype(o_ref.dtype)


def _matmul(x, w, out_dtype, name):
    m, k = x.shape
    n = w.shape[1]
    tm = _pick_tile(m, (512, 256))
    tn = _pick_tile(n, (1024, 512))
    return pl.pallas_call(
        _matmul_kernel,
        grid=(m // tm, n // tn),
        in_specs=[pl.BlockSpec((tm, k), lambda i, j: (i, 0)),
                  pl.BlockSpec((k, tn), lambda i, j: (0, j))],
        out_specs=pl.BlockSpec((tm, tn), lambda i, j: (i, j)),
        out_shape=jax.ShapeDtypeStruct((m, n), out_dtype),
        compiler_params=_cparams(("parallel", "arbitrary")),
        name=name,
    )(x, w)


def _layer_norm_rows(z, g, b):
    mu = jnp.mean(z, axis=-1, keepdims=True)
    zc = z - mu
    var = jnp.mean(zc * zc, axis=-1, keepdims=True)
    return zc * lax.rsqrt(var + LN_EPS) * g + b


def _add_ln_kernel(x_ref, h_ref, g_ref, b_ref, o_ref, ob_ref):
    y = _layer_norm_rows(ALPHA * x_ref[...] + h_ref[...], g_ref[...], b_ref[...])
    o_ref[...] = y
    ob_ref[...] = y.astype(BF16)


def _add_ln(x, h, g, b):
    m, d = x.shape
    tm = _pick_tile(m, (512, 256))
    row = pl.BlockSpec((tm, d), lambda i: (i, 0))
    vec = pl.BlockSpec((1, d), lambda i: (0, 0))
    return pl.pallas_call(
        _add_ln_kernel,
        grid=(m // tm,),
        in_specs=[row, row, vec, vec],
        out_specs=[row, row],
        out_shape=[jax.ShapeDtypeStruct((m, d), F32), jax.ShapeDtypeStruct((m, d), BF16)],
        compiler_params=_cparams(("parallel",)),
        name="add_ln",
    )(x, h, g.reshape(1, d), b.reshape(1, d))


def _ret_kernel(lg_ref, cd_ref, q_ref, k_ref, v_ref, *rest, backward, segs_chunks):
    if backward:
        of_ref, g_ref, o_ref, state_ref, dmat_ref, qd_ref, kd_ref = rest
    else:
        o_ref, state_ref, dmat_ref, qd_ref, kd_ref = rest
    h = pl.program_id(0)
    c = pl.program_id(1)
    nc = pl.num_programs(1)
    cc = (nc - 1 - c) if backward else c
    lg = lg_ref[h]
    cs = RET_CHUNK

    @pl.when(c == 0)
    def _tables():
        ii = lax.broadcasted_iota(I32, (cs, cs), 0)
        jj = lax.broadcasted_iota(I32, (cs, cs), 1)
        if backward:
            mask = jj > ii
            dist = (jj - ii).astype(F32)
        else:
            mask = ii >= jj
            dist = (ii - jj).astype(F32)
        dmat_ref[...] = jnp.where(mask, jnp.exp(jnp.where(mask, dist, 0.0) * lg), 0.0)
        pos = lax.broadcasted_iota(I32, (cs, 1), 0).astype(F32)
        if backward:
            qd_ref[...] = jnp.exp((cs - pos) * lg)
            kd_ref[...] = jnp.exp(pos * lg)
        else:
            qd_ref[...] = jnp.exp((pos + 1.0) * lg)
            kd_ref[...] = jnp.exp((cs - 1.0 - pos) * lg)

    loc = _seg_local_block(cc, segs_chunks)
    if backward:
        per = None
        for first, p in segs_chunks:
            per = p if per is None else jnp.where(cc >= first, p, per)
        is_start = loc == per - 1
    else:
        is_start = loc == 0

    @pl.when(is_start)
    def _reset():
        state_ref[...] = jnp.zeros_like(state_ref)

    q = q_ref[...]
    k = k_ref[...]
    v = v_ref[...]
    att = lax.dot_general(q, k, (((1,), (1,)), ((), ())), preferred_element_type=F32) * dmat_ref[...]
    inner = jnp.dot(att.astype(BF16), v, preferred_element_type=F32)
    state = state_ref[...]
    cross = jnp.dot(q, state.astype(BF16), preferred_element_type=F32) * qd_ref[...]
    o = inner + cross
    kt = (k.astype(F32) * kd_ref[...]).T.astype(BF16)
    state_ref[...] = state * cd_ref[h] + jnp.dot(kt, v, preferred_element_type=F32)

    if backward:
        tot = of_ref[...] + o
        mu = jnp.mean(tot, axis=-1, keepdims=True)
        tc = tot - mu
        var = jnp.mean(tc * tc, axis=-1, keepdims=True)
        on = tc * lax.rsqrt(var + LN_EPS)
        g = g_ref[...].astype(F32)
        o_ref[...] = ((g * (1.0 / (1.0 + jnp.exp(-g)))) * on).astype(o_ref.dtype)
    else:
        o_ref[...] = o


def _retention(proj, lg_f, lg_b, segs):
    t = proj.shape[0]
    cs = RET_CHUNK
    nc = t // cs
    segs_chunks = [(start // cs, s // cs) for start, s in segs]
    qk_blocks = (RET_HEADS * RET_DK) // RET_DK
    v_blocks0 = (2 * RET_HEADS * RET_DK) // RET_DV
    g_blocks0 = v_blocks0 + RET_HEADS
    smem = pl.BlockSpec(memory_space=pltpu.SMEM)
    scratch = [pltpu.VMEM((RET_DK, RET_DV), F32), pltpu.VMEM((cs, cs), F32),
               pltpu.VMEM((cs, 1), F32), pltpu.VMEM((cs, 1), F32)]

    def run(backward, lg, extra):
        def cmap(c):
            return (nc - 1 - c) if backward else c
        in_specs = [smem, smem,
                    pl.BlockSpec((cs, RET_DK), lambda h, c: (cmap(c), h)),
                    pl.BlockSpec((cs, RET_DK), lambda h, c: (cmap(c), qk_blocks + h)),
                    pl.BlockSpec((cs, RET_DV), lambda h, c: (cmap(c), v_blocks0 + h))]
        args = [lg, jnp.exp(cs * lg), proj, proj, proj]
        if backward:
            in_specs += [pl.BlockSpec((cs, RET_DV), lambda h, c: (cmap(c), h)),
                         pl.BlockSpec((cs, RET_DV), lambda h, c: (cmap(c), g_blocks0 + h))]
            args += [extra, proj]
        return pl.pallas_call(
            functools.partial(_ret_kernel, backward=backward, segs_chunks=segs_chunks),
            grid=(RET_HEADS, nc),
            in_specs=in_specs,
            out_specs=pl.BlockSpec((cs, RET_DV), lambda h, c: (cmap(c), h)),
            out_shape=jax.ShapeDtypeStruct((t, RET_HEADS * RET_DV), BF16 if backward else F32),
            scratch_shapes=scratch,
            compiler_params=_cparams(("parallel", "arbitrary")),
            name="retention_bwd" if backward else "retention_fwd",
        )(*args)

    o_f = run(False, lg_f, None)
    return run(True, lg_b, o_f)


def _attn_kernel(lam_ref, q_ref, k_ref, v_ref, sg_ref, o_ref, m_ref, l_ref, acc_ref, *, out_scale):
    kv = pl.program_id(3)

    @pl.when(kv == 0)
    def _init():
        m_ref[...] = jnp.full_like(m_ref, NEG_INF)
        l_ref[...] = jnp.zeros_like(l_ref)
        acc_ref[...] = jnp.zeros_like(acc_ref)

    v = v_ref[...]
    for c in range(2):
        q = q_ref[:, c * DIFF_DH:(c + 1) * DIFF_DH]
        k = k_ref[:, c * DIFF_DH:(c + 1) * DIFF_DH]
        s = lax.dot_general(q, k, (((1,), (1,)), ((), ())), preferred_element_type=F32)
        m_old = m_ref[c]
        m_new = jnp.maximum(m_old, jnp.max(s, axis=-1, keepdims=True))
        p = jnp.exp(s - m_new)
        a = jnp.exp(m_old - m_new)
        l_ref[c] = a * l_ref[c] + jnp.sum(p, axis=-1, keepdims=True)
        acc_ref[c] = a * acc_ref[c] + jnp.dot(p.astype(BF16), v, preferred_element_type=F32)
        m_ref[c] = m_new

    @pl.when(kv == pl.num_programs(3) - 1)
    def _fin():
        o = acc_ref[0] / l_ref[0] - lam_ref[0] * (acc_ref[1] / l_ref[1])
        ms = jnp.mean(o * o, axis=-1, keepdims=True)
        o_ref[...] = (o * lax.rsqrt(ms + LN_EPS) * sg_ref[...] * out_scale).astype(o_ref.dtype)


def _diff_attention(proj, lam, subln_g, segs, lambda_init):
    nq_blocks = DIFF_HEADS
    outs = []
    for start, s, n_seq in segs:
        tq = _pick_tile(s, (512, 256))
        tk = _pick_tile(s, (512, 256))
        qb0 = start // tq
        kb0 = start // tk
        nqt = s // tq
        nkt = s // tk
        out = pl.pallas_call(
            functools.partial(_attn_kernel, out_scale=1.0 - lambda_init),
            grid=(n_seq, DIFF_HEADS, nqt, nkt),
            in_specs=[pl.BlockSpec(memory_space=pltpu.SMEM),
                      pl.BlockSpec((tq, 2 * DIFF_DH), lambda b, h, i, j: (qb0 + b * nqt + i, h)),
                      pl.BlockSpec((tk, 2 * DIFF_DH), lambda b, h, i, j: (kb0 + b * nkt + j, nq_blocks + h)),
                      pl.BlockSpec((tk, DIFF_DV), lambda b, h, i, j: (kb0 + b * nkt + j, 2 * nq_blocks + h)),
                      pl.BlockSpec((1, DIFF_DV), lambda b, h, i, j: (0, 0))],
            out_specs=pl.BlockSpec((tq, DIFF_DV), lambda b, h, i, j: (b * nqt + i, h)),
            out_shape=jax.ShapeDtypeStruct((n_seq * s, DIFF_HEADS * DIFF_DV), BF16),
            scratch_shapes=[pltpu.VMEM((2, tq, 1), F32), pltpu.VMEM((2, tq, 1), F32),
                            pltpu.VMEM((2, tq, DIFF_DV), F32)],
            compiler_params=_cparams(("parallel", "parallel", "parallel", "arbitrary")),
            name=f"diff_attn_s{s}",
        )(lam, proj, proj, proj, subln_g.reshape(1, DIFF_DV))
        outs.append(out)
    return jnp.concatenate(outs, axis=0)


def _first_max(cur, rows, n_rows):
    m = jnp.max(cur, axis=0, keepdims=True)
    idx = jnp.min(jnp.where(cur == m, rows, float(n_rows)), axis=0, keepdims=True)
    return m, idx


def _router_kernel(x_ref, wt_ref, b_ref, idx_ref, gate_ref, rank_ref, cnt_ref, carry_ref):
    i = pl.program_id(0)

    @pl.when(i == 0)
    def _init():
        carry_ref[...] = jnp.zeros_like(carry_ref)

    tn = x_ref.shape[0]
    x = x_ref[...]
    xh = x.astype(BF16)
    xl = (x - xh.astype(F32)).astype(BF16)
    w = wt_ref[...]
    wh = w.astype(BF16)
    wl = (w - wh.astype(F32)).astype(BF16)

    def nt(a, b):
        return lax.dot_general(a, b, (((1,), (1,)), ((), ())), preferred_element_type=F32)

    logits = nt(wh, xh) + nt(wh, xl) + nt(wl, xh)
    scores = 1.0 / (1.0 + jnp.exp(-logits))
    biased = scores + b_ref[...]

    rows_g = lax.broadcasted_iota(I32, (GROUP_SIZE, tn), 0).astype(F32)
    rows_e = lax.broadcasted_iota(I32, (N_EXPERTS, tn), 0).astype(F32)

    gscore = []
    for g in range(N_GROUPS):
        slab = biased[g * GROUP_SIZE:(g + 1) * GROUP_SIZE]
        m1, i1 = _first_max(slab, rows_g, GROUP_SIZE)
        m2 = jnp.max(jnp.where(rows_g == i1, NEG_INF, slab), axis=0, keepdims=True)
        gscore.append(m1 + m2)
    cur = jnp.concatenate(gscore, axis=0)
    rows_grp = lax.broadcasted_iota(I32, (N_GROUPS, tn), 0).astype(F32)
    gsel = jnp.zeros((N_GROUPS, tn), F32)
    for _ in range(TOPK_GROUPS):
        _, gi = _first_max(cur, rows_grp, N_GROUPS)
        pick = rows_grp == gi
        gsel = jnp.where(pick, 1.0, gsel)
        cur = jnp.where(pick, NEG_INF, cur)

    masked = jnp.concatenate(
        [jnp.where(gsel[g:g + 1] > 0.0, biased[g * GROUP_SIZE:(g + 1) * GROUP_SIZE], NEG_INF)
         for g in range(N_GROUPS)], axis=0)

    sel = jnp.zeros((N_EXPERTS, tn), F32)
    picks = []
    cur = masked
    for _ in range(TOP_K):
        _, ei = _first_max(cur, rows_e, N_EXPERTS)
        pick = rows_e == ei
        sel = jnp.where(pick, 1.0, sel)
        cur = jnp.where(pick, NEG_INF, cur)
        picks.append(ei)

    gsum = jnp.sum(scores * sel, axis=0, keepdims=True)

    ta = lax.broadcasted_iota(I32, (tn, tn), 0)
    tb = lax.broadcasted_iota(I32, (tn, tn), 1)
    before = jnp.where(ta < tb, 1.0, 0.0).astype(BF16)
    rank = jnp.dot(sel.astype(BF16), before, preferred_element_type=F32) + carry_ref[...]
    carry_ref[...] = carry_ref[...] + jnp.sum(sel, axis=1, keepdims=True)
    cnt_ref[...] = carry_ref[...]

    gates = []
    ranks = []
    for ei in picks:
        hit = rows_e == ei
        gates.append(jnp.sum(jnp.where(hit, scores, 0.0), axis=0, keepdims=True) / gsum * ROUTED_SCALE)
        ranks.append(jnp.sum(jnp.where(hit, rank, 0.0), axis=0, keepdims=True))
    idx_ref[...] = jnp.concatenate(picks, axis=0).astype(I32)
    gate_ref[...] = jnp.concatenate(gates, axis=0)
    rank_ref[...] = jnp.concatenate(ranks, axis=0).astype(I32)


def _router(x, w_router, b_router):
    t, d = x.shape
    tn = _pick_tile(t, (512, 256))
    kt = pl.BlockSpec((TOP_K, tn), lambda i: (0, i))
    return pl.pallas_call(
        _router_kernel,
        grid=(t // tn,),
        in_specs=[pl.BlockSpec((tn, d), lambda i: (i, 0)),
                  pl.BlockSpec((N_EXPERTS, d), lambda i: (0, 0)),
                  pl.BlockSpec((N_EXPERTS, 1), lambda i: (0, 0))],
        out_specs=[kt, kt, kt, pl.BlockSpec((N_EXPERTS, 1), lambda i: (0, 0))],
        out_shape=[jax.ShapeDtypeStruct((TOP_K, t), I32), jax.ShapeDtypeStruct((TOP_K, t), F32),
                   jax.ShapeDtypeStruct((TOP_K, t), I32), jax.ShapeDtypeStruct((N_EXPERTS, 1), F32)],
        scratch_shapes=[pltpu.VMEM((N_EXPERTS, 1), F32)],
        compiler_params=_cparams(("arbitrary",)),
        name="router",
    )(x, w_router.T, b_router.reshape(N_EXPERTS, 1))


def _slot_kernel(idx_ref, rank_ref, start_ref, slot_ref):
    tn = idx_ref.shape[1]
    rows_e = lax.broadcasted_iota(I32, (N_EXPERTS, tn), 0)
    out = []
    for k in range(TOP_K):
        hit = rows_e == idx_ref[k:k + 1, :]
        base = jnp.sum(jnp.where(hit, start_ref[...], 0.0), axis=0, keepdims=True)
        out.append(base.astype(I32) + rank_ref[k:k + 1, :])
    slot_ref[...] = jnp.concatenate(out, axis=0)


def _slots(idx_t, rank_t, pad_start):
    t = idx_t.shape[1]
    tn = _pick_tile(t, (512, 256))
    kt = pl.BlockSpec((TOP_K, tn), lambda i: (0, i))
    return pl.pallas_call(
        _slot_kernel,
        grid=(t // tn,),
        in_specs=[kt, kt, pl.BlockSpec((N_EXPERTS, 1), lambda i: (0, 0))],
        out_specs=kt,
        out_shape=jax.ShapeDtypeStruct((TOP_K, t), I32),
        compiler_params=_cparams(("parallel",)),
        name="slots",
    )(idx_t, rank_t, pad_start.astype(F32).reshape(N_EXPERTS, 1))


def _row_copy(src, src_row, dst, dst_row, sem):
    return pltpu.make_async_copy(src.at[pl.ds(src_row, 1)], dst.at[pl.ds(dst_row, 1)], sem)


def _dispatch_kernel(pad_end_ref, padded_ref, slot_hbm, x_ref, xs_hbm, slot_smem, zero_ref, sem, slot_sem):
    i = pl.program_id(0)
    tt = x_ref.shape[0]
    bm = zero_ref.shape[0]
    slot_cp = pltpu.make_async_copy(slot_hbm.at[i], slot_smem, slot_sem)
    slot_cp.start()

    @pl.when(i == 0)
    def _clear():
        zero_ref[...] = jnp.zeros_like(zero_ref)

        def pad_copy(e):
            off = pl.multiple_of(pad_end_ref[e] - bm, bm)
            return pltpu.make_async_copy(zero_ref, xs_hbm.at[pl.ds(off, bm)], sem)

        def start(e, carry):
            @pl.when(padded_ref[e] > 0)
            def _():
                pad_copy(e).start()
            return carry

        def wait(e, carry):
            @pl.when(padded_ref[e] > 0)
            def _():
                pad_copy(e).wait()
            return carry

        lax.fori_loop(0, N_EXPERTS, start, 0)
        lax.fori_loop(0, N_EXPERTS, wait, 0)

    slot_cp.wait()
    per_row = LANES // TOP_K

    def issue(t, carry):
        r = t // per_row
        c0 = (t % per_row) * TOP_K
        for k in range(TOP_K):
            _row_copy(x_ref, t, xs_hbm, slot_smem[r, c0 + k], sem).start()
        return carry

    lax.fori_loop(0, tt, issue, 0)

    def drain(t, carry):
        for k in range(TOP_K):
            _row_copy(x_ref, t, xs_hbm, 0, sem).wait()
        return carry

    lax.fori_loop(0, tt, drain, 0)


def _dispatch(x, slot_tiles, pad_end, padded, n_slots):
    t, d = x.shape
    tt = ROW_TILE
    grid_spec = pltpu.PrefetchScalarGridSpec(
        num_scalar_prefetch=2,
        grid=(t // tt,),
        in_specs=[pl.BlockSpec(memory_space=pl.ANY),
                  pl.BlockSpec((tt, d), lambda i, pe, pd: (i, 0))],
        out_specs=pl.BlockSpec(memory_space=pl.ANY),
        scratch_shapes=[pltpu.SMEM((tt * TOP_K // LANES, LANES), I32),
                        pltpu.VMEM((EXPERT_ROWS, d), F32),
                        pltpu.SemaphoreType.DMA, pltpu.SemaphoreType.DMA],
    )
    return pl.pallas_call(
        _dispatch_kernel,
        grid_spec=grid_spec,
        out_shape=jax.ShapeDtypeStruct((n_slots, d), F32),
        compiler_params=_cparams(("arbitrary",)),
        name="dispatch",
    )(pad_end, padded, slot_tiles, x)


def _ffn_kernel(block_e_ref, n_used_ref, x_ref, wgu_ref, wd_ref, o_ref):
    b = pl.program_id(0)

    @pl.when(b < n_used_ref[0])
    def _():
        hidden = wd_ref.shape[1]
        gu = jnp.dot(x_ref[...].astype(BF16), wgu_ref[0], preferred_element_type=F32)
        gate = gu[:, :hidden]
        act = gate * (1.0 / (1.0 + jnp.exp(-gate))) * gu[:, hidden:]
        o_ref[...] = jnp.dot(act.astype(BF16), wd_ref[0], preferred_element_type=F32)


def _ffn(x, w_gu, w_down, block_e, n_used, rows, name):
    n, d = x.shape
    hidden = w_down.shape[1]

    def row_map(b, be, nu):
        return (jnp.minimum(b, nu[0] - 1), 0)

    grid_spec = pltpu.PrefetchScalarGridSpec(
        num_scalar_prefetch=2,
        grid=(n // rows,),
        in_specs=[pl.BlockSpec((rows, d), row_map),
                  pl.BlockSpec((1, d, 2 * hidden), lambda b, be, nu: (be[b], 0, 0)),
                  pl.BlockSpec((1, hidden, d), lambda b, be, nu: (be[b], 0, 0))],
        out_specs=pl.BlockSpec((rows, d), row_map),
    )
    return pl.pallas_call(
        _ffn_kernel,
        grid_spec=grid_spec,
        out_shape=jax.ShapeDtypeStruct((n, d), F32),
        compiler_params=_cparams(("arbitrary",)),
        name=name,
    )(block_e, n_used, x, w_gu, w_down)


def _combine_kernel(slot_hbm, x_ref, sh_ref, gate_ref, g_ref, b_ref, ys_hbm, o_ref, ob_ref,
                    slot_smem, buf_ref, sem, slot_sem):
    i = pl.program_id(0)
    tt = x_ref.shape[0]
    slot_cp = pltpu.make_async_copy(slot_hbm.at[i], slot_smem, slot_sem)
    slot_cp.start()
    slot_cp.wait()
    per_row = LANES // TOP_K

    def issue(t, carry):
        r = t // per_row
        c0 = (t % per_row) * TOP_K
        for k in range(TOP_K):
            _row_copy(ys_hbm, slot_smem[r, c0 + k], buf_ref.at[k], t, sem).start()
        return carry

    lax.fori_loop(0, tt, issue, 0)

    def drain(t, carry):
        for k in range(TOP_K):
            _row_copy(ys_hbm, 0, buf_ref.at[k], t, sem).wait()
        return carry

    lax.fori_loop(0, tt, drain, 0)

    h = sh_ref[...]
    gate = gate_ref[...]
    for k in range(TOP_K):
        h = h + buf_ref[k] * gate[:, k:k + 1]
    y = _layer_norm_rows(ALPHA * x_ref[...] + h, g_ref[...], b_ref[...])
    o_ref[...] = y
    ob_ref[...] = y.astype(BF16)


def _combine(x, shared, ys, slot_tiles, gate, g, b):
    t, d = x.shape
    tt = ROW_TILE
    row = pl.BlockSpec((tt, d), lambda i: (i, 0))
    vec = pl.BlockSpec((1, d), lambda i: (0, 0))
    return pl.pallas_call(
        _combine_kernel,
        grid=(t // tt,),
        in_specs=[pl.BlockSpec(memory_space=pl.ANY), row, row,
                  pl.BlockSpec((tt, TOP_K), lambda i: (i, 0)), vec, vec,
                  pl.BlockSpec(memory_space=pl.ANY)],
        out_specs=[row, row],
        out_shape=[jax.ShapeDtypeStruct((t, d), F32), jax.ShapeDtypeStruct((t, d), BF16)],
        scratch_shapes=[pltpu.SMEM((tt * TOP_K // LANES, LANES), I32),
                        pltpu.VMEM((TOP_K, tt, d), F32),
                        pltpu.SemaphoreType.DMA, pltpu.SemaphoreType.DMA],
        compiler_params=_cparams(("arbitrary",)),
        name="combine",
    )(slot_tiles, x, shared, gate, g.reshape(1, d), b.reshape(1, d), ys)


def _moe_layer(x, w_router, b_router, w_gu, w_down, sw_gu, sw_down, ln_g, ln_b):
    t, d = x.shape
    bm = EXPERT_ROWS
    idx_t, gate_t, rank_t, counts = _router(x, w_router, b_router)

    counts = counts.reshape(N_EXPERTS).astype(I32)
    padded = (counts + bm - 1) // bm * bm
    pad_end = jnp.cumsum(padded)
    pad_start = pad_end - padded
    n_blocks = (t * TOP_K + N_EXPERTS * (bm - 1) + bm - 1) // bm
    n_slots = n_blocks * bm
    block_e = jnp.minimum(jnp.searchsorted(pad_end, jnp.arange(n_blocks, dtype=I32) * bm, side="right"),
                          N_EXPERTS - 1).astype(I32)
    n_used = (pad_end[-1:] // bm).astype(I32)

    slot_t = _slots(idx_t, rank_t, pad_start)
    slot_tiles = slot_t.T.reshape(t // ROW_TILE, ROW_TILE * TOP_K // LANES, LANES)
    xs = _dispatch(x, slot_tiles, pad_end.astype(I32), padded.astype(I32), n_slots)
    ys = _ffn(xs, w_gu, w_down, block_e, n_used, bm, "expert_ffn")
    shared_rows = _pick_tile(t, (512, 256))
    shared = _ffn(x, sw_gu[None], sw_down[None], jnp.zeros((t // shared_rows,), I32),
                  jnp.full((1,), t // shared_rows, I32), shared_rows, "shared_ffn")
    return _combine(x, shared, ys, slot_tiles, gate_t.T, ln_g, ln_b)


def _rope_angles(s, theta, half):
    inv = theta ** (-jnp.arange(half, dtype=F32) / half)
    ang = jnp.arange(s, dtype=F32)[:, None] * inv[None, :]
    return jnp.cos(ang), jnp.sin(ang)


def _ret_tables(s):
    cos, sin = _rope_angles(s, RET_THETA, RET_DK // 2)
    return cos, sin, sin


def _diff_tables(s):
    half = ROPE_DIMS // 2
    cos, sin = _rope_angles(s, ROPE_THETA, half)
    rest = DIFF_DH - ROPE_DIMS
    zeros_h = jnp.zeros((s, half), F32)
    c = jnp.concatenate([cos, cos, jnp.ones((s, rest), F32)], axis=1)
    s_up = jnp.concatenate([zeros_h, sin, jnp.zeros((s, rest), F32)], axis=1)
    s_dn = jnp.concatenate([-sin, zeros_h, jnp.zeros((s, rest), F32)], axis=1)
    return c, s_up, s_dn


def kernel(x_prompt, x_sample, ret_w_in, ret_decay_f, ret_decay_b, ret_w_out, diff_w_in, diff_lam_q1, diff_lam_k1,
           diff_lam_q2, diff_lam_k2, diff_subln_g, diff_w_out, ln_mix_g, ln_mix_b, router_w, router_b, exp_w_gu,
           exp_w_down, shared_w_gu, shared_w_down, ln_ffn_g, ln_ffn_b):
    bp, sp, d = x_prompt.shape
    bs, ss, _ = x_sample.shape
    tp = bp * sp
    x = jnp.concatenate([x_prompt.reshape(tp, d), x_sample.reshape(bs * ss, d)], axis=0)
    xb = x.astype(BF16)
    segs = [(0, sp), (tp, ss)]
    s_max = max(sp, ss)

    for i in range(DEPTH):
        j = i // 2
        if i % 2 == 0:
            qk = RET_HEADS * RET_DK
            proj = _proj(xb, ret_w_in[j].astype(BF16), _ret_tables(s_max), segs, mode="ret",
                         n_q_cols=qk, n_qk_cols=2 * qk, q_scale=1.0, k_scale=RET_DK ** -0.5)
            lg_f = -jax.nn.softplus(-ret_decay_f[j].astype(F32))
            lg_b = -jax.nn.softplus(-ret_decay_b[j].astype(F32))
            mixed = _retention(proj, lg_f, lg_b, segs)
            h = _matmul(mixed, ret_w_out[j].astype(BF16), F32, "ret_out")
        else:
            lambda_init = 0.8 - 0.6 * math.exp(-0.3 * i)
            nq = DIFF_HEADS * 2 * DIFF_DH
            proj = _proj(xb, diff_w_in[j].astype(BF16), _diff_tables(s_max), segs, mode="diff",
                         n_q_cols=nq, n_qk_cols=2 * nq, q_scale=DIFF_DH ** -0.5, k_scale=1.0)
            lam = (jnp.exp(jnp.sum(diff_lam_q1[j].astype(F32) * diff_lam_k1[j].astype(F32)))
                   - jnp.exp(jnp.sum(diff_lam_q2[j].astype(F32) * diff_lam_k2[j].astype(F32))) + lambda_init)
            mixed = _diff_attention(proj, lam.reshape(1), diff_subln_g[j],
                                    [(0, sp, bp), (tp, ss, bs)], lambda_init)
            h = _matmul(mixed, diff_w_out[j].astype(BF16), F32, "diff_out")
        x, xb = _add_ln(x, h, ln_mix_g[i], ln_mix_b[i])
        x, xb = _moe_layer(x, router_w[i], router_b[i], exp_w_gu[i].astype(BF16), exp_w_down[i].astype(BF16),
                           shared_w_gu[i].astype(BF16), shared_w_down[i].astype(BF16), ln_ffn_g[i], ln_ffn_b[i])
    return x[:tp].reshape(bp, sp, d), x[tp:].reshape(bs, ss, d)
```

```python
import functools
import math

import jax
import jax.numpy as jnp
from jax import lax
from jax.experimental import pallas as pl
from jax.experimental.pallas import tpu as pltpu

F32 = jnp.float32
BF16 = jnp.bfloat16
I32 = jnp.int32

D_MODEL = 2048
DEPTH = 2
ALPHA = (2 * DEPTH) ** 0.25
LN_EPS = 1e-5
RET_HEADS = 8
RET_DK = D_MODEL // RET_HEADS
RET_DV = 2 * RET_DK
RET_THETA = 10000.0
DIFF_HEADS = 8
DIFF_DH = D_MODEL // (2 * DIFF_HEADS)
DIFF_DV = 2 * DIFF_DH
ROPE_THETA = 500000.0
ROPE_DIMS = DIFF_DH // 4
N_EXPERTS = 64
TOP_K = 8
N_GROUPS = 8
TOPK_GROUPS = 4
GROUP_SIZE = N_EXPERTS // N_GROUPS
EXPERT_HIDDEN = 512
SHARED_HIDDEN = 512
ROUTED_SCALE = 2.5

LANES = 128
SUBLANES = 8
VMEM_LIMIT = 48 * 1024 * 1024

RET_CHUNK = 256
EXPERT_ROWS = 256
DISPATCH_TILE = 256
COMBINE_TILE = 128
ATTN_Q_TILE = 256
NEG_INF = float("-inf")


def _cparams(sem):
    return pltpu.CompilerParams(dimension_semantics=sem, vmem_limit_bytes=VMEM_LIMIT)


def _pick_tile(n, prefs):
    for p in prefs:
        if n % p == 0:
            return p
    raise ValueError(f"no tile in {prefs} divides {n}")


def _seg_local_block(i, segs_blocks):
    val = None
    for first, per in segs_blocks:
        loc = lax.rem(i - first, per)
        val = loc if val is None else jnp.where(i >= first, loc, val)
    return val


def _proj_kernel(x_ref, w_ref, t0_ref, t1_ref, t2_ref, o_ref, *, mode, n_q_tiles, n_qk_tiles, q_scale, k_scale):
    j = pl.program_id(1)
    acc = jnp.dot(x_ref[...], w_ref[...], preferred_element_type=F32)
    tn = acc.shape[1]

    @pl.when(j >= n_qk_tiles)
    def _plain():
        o_ref[...] = acc.astype(o_ref.dtype)

    @pl.when(j < n_qk_tiles)
    def _rope():
        scale = jnp.where(j < n_q_tiles, q_scale, k_scale)
        if mode == "ret":
            cos = t0_ref[...]
            sin = t1_ref[...]
            half = RET_DK // 2
            for hs in range(tn // RET_DK):
                lo = hs * RET_DK
                x1 = acc[:, lo:lo + half]
                x2 = acc[:, lo + half:lo + RET_DK]
                o_ref[:, lo:lo + half] = ((x1 * cos - x2 * sin) * scale).astype(o_ref.dtype)
                o_ref[:, lo + half:lo + RET_DK] = ((x2 * cos + x1 * sin) * scale).astype(o_ref.dtype)
        else:
            c = t0_ref[...]
            s_up = t1_ref[...]
            s_dn = t2_ref[...]
            half = ROPE_DIMS // 2
            for g in range(tn // DIFF_DH):
                lo = g * DIFF_DH
                seg = acc[:, lo:lo + DIFF_DH]
                rot = seg * c + pltpu.roll(seg, half, 1) * s_up + pltpu.roll(seg, DIFF_DH - half, 1) * s_dn
                o_ref[:, lo:lo + DIFF_DH] = (rot * scale).astype(o_ref.dtype)


def _proj(x, w, tables, segs, *, mode, n_q_cols, n_qk_cols, q_scale, k_scale):
    m, k = x.shape
    n = w.shape[1]
    seg_len = [s for _, s in segs]
    tm = _pick_tile(math.gcd(*seg_len), (1024, 512, 256))
    tn = 1024
    segs_blocks = [(start // tm, s // tm) for start, s in segs]
    tw = tables[0].shape[1]
    tab_spec = pl.BlockSpec((tm, tw), lambda i, j: (_seg_local_block(i, segs_blocks), 0))
    kern = functools.partial(_proj_kernel, mode=mode, n_q_tiles=n_q_cols // tn, n_qk_tiles=n_qk_cols // tn,
                             q_scale=q_scale, k_scale=k_scale)
    return pl.pallas_call(
        kern,
        grid=(m // tm, n // tn),
        in_specs=[pl.BlockSpec((tm, k), lambda i, j: (i, 0)),
                  pl.BlockSpec((k, tn), lambda i, j: (0, j)),
                  tab_spec, tab_spec, tab_spec],
        out_specs=pl.BlockSpec((tm, tn), lambda i, j: (i, j)),
        out_shape=jax.ShapeDtypeStruct((m, n), BF16),
        compiler_params=_cparams(("arbitrary", "arbitrary")),
        name=f"proj_{mode}",
    )(x, w, *tables)


def _matmul_kernel(x_ref, w_ref, o_ref):
    o_ref[...] = jnp.dot(x_ref[...], w_ref[...], preferred_element_type=F32).astype(o_ref.dtype)


def _matmul(x, w, out_dtype, name):
    m, k = x.shape
    n = w.shape[1]
    tm = _pick_tile(m, (512, 256))
    tn = _pick_tile(n, (1024, 512))
    return pl.pallas_call(
        _matmul_kernel,
        grid=(m // tm, n // tn),
        in_specs=[pl.BlockSpec((tm, k), lambda i, j: (i, 0)),
                  pl.BlockSpec((k, tn), lambda i, j: (0, j))],
        out_specs=pl.BlockSpec((tm, tn), lambda i, j: (i, j)),
        out_shape=jax.ShapeDtypeStruct((m, n), out_dtype),
        compiler_params=_cparams(("arbitrary", "arbitrary")),
        name=name,
    )(x, w)


def _layer_norm_rows(z, g, b):
    mu = jnp.mean(z, axis=-1, keepdims=True)
    zc = z - mu
    var = jnp.mean(zc * zc, axis=-1, keepdims=True)
    return zc * lax.rsqrt(var + LN_EPS) * g + b


def _add_ln_kernel(x_ref, h_ref, g_ref, b_ref, o_ref, ob_ref):
    y = _layer_norm_rows(ALPHA * x_ref[...] + h_ref[...], g_ref[...], b_ref[...])
    o_ref[...] = y
    ob_ref[...] = y.astype(BF16)


def _add_ln(x, h, g, b):
    m, d = x.shape
    tm = _pick_tile(m, (512, 256))
    row = pl.BlockSpec((tm, d), lambda i: (i, 0))
    vec = pl.BlockSpec((1, d), lambda i: (0, 0))
    return pl.pallas_call(
        _add_ln_kernel,
        grid=(m // tm,),
        in_specs=[row, row, vec, vec],
        out_specs=[row, row],
        out_shape=[jax.ShapeDtypeStruct((m, d), F32), jax.ShapeDtypeStruct((m, d), BF16)],
        compiler_params=_cparams(("arbitrary",)),
        name="add_ln",
    )(x, h, g.reshape(1, d), b.reshape(1, d))


def _ret_kernel(lg_ref, cd_ref, q_ref, k_ref, v_ref, *rest, backward, segs_chunks):
    if backward:
        of_ref, g_ref, o_ref, state_ref, dmat_ref, qd_ref, kd_ref = rest
    else:
        o_ref, state_ref, dmat_ref, qd_ref, kd_ref = rest
    h = pl.program_id(0)
    c = pl.program_id(1)
    nc = pl.num_programs(1)
    cc = (nc - 1 - c) if backward else c
    lg = lg_ref[h]
    cs = RET_CHUNK

    @pl.when(c == 0)
    def _tables():
        ii = lax.broadcasted_iota(I32, (cs, cs), 0)
        jj = lax.broadcasted_iota(I32, (cs, cs), 1)
        if backward:
            mask = jj > ii
            dist = (jj - ii).astype(F32)
        else:
            mask = ii >= jj
            dist = (ii - jj).astype(F32)
        dmat_ref[...] = jnp.where(mask, jnp.exp(jnp.where(mask, dist, 0.0) * lg), 0.0)
        pos = lax.broadcasted_iota(I32, (cs, 1), 0).astype(F32)
        if backward:
            qd_ref[...] = jnp.exp((cs - pos) * lg)
            kd_ref[...] = jnp.exp(pos * lg)
        else:
            qd_ref[...] = jnp.exp((pos + 1.0) * lg)
            kd_ref[...] = jnp.exp((cs - 1.0 - pos) * lg)

    loc = _seg_local_block(cc, segs_chunks)
    if backward:
        per = None
        for first, p in segs_chunks:
            per = p if per is None else jnp.where(cc >= first, p, per)
        is_start = loc == per - 1
    else:
        is_start = loc == 0

    @pl.when(is_start)
    def _reset():
        state_ref[...] = jnp.zeros_like(state_ref)

    q = q_ref[...]
    k = k_ref[...]
    v = v_ref[...]
    att = lax.dot_general(q, k, (((1,), (1,)), ((), ())), preferred_element_type=F32) * dmat_ref[...]
    inner = jnp.dot(att.astype(BF16), v, preferred_element_type=F32)
    state = state_ref[...]
    cross = jnp.dot(q, state.astype(BF16), preferred_element_type=F32) * qd_ref[...]
    o = inner + cross
    kt = (k.astype(F32) * kd_ref[...]).T.astype(BF16)
    state_ref[...] = state * cd_ref[h] + jnp.dot(kt, v, preferred_element_type=F32)

    if backward:
        tot = of_ref[...] + o
        mu = jnp.mean(tot, axis=-1, keepdims=True)
        tc = tot - mu
        var = jnp.mean(tc * tc, axis=-1, keepdims=True)
        on = tc * lax.rsqrt(var + LN_EPS)
        g = g_ref[...].astype(F32)
        o_ref[...] = ((g * (1.0 / (1.0 + jnp.exp(-g)))) * on).astype(o_ref.dtype)
    else:
        o_ref[...] = o


def _retention(proj, lg_f, lg_b, segs):
    t = proj.shape[0]
    cs = RET_CHUNK
    nc = t // cs
    segs_chunks = [(start // cs, s // cs) for start, s in segs]
    qk_blocks = (RET_HEADS * RET_DK) // RET_DK
    v_blocks0 = (2 * RET_HEADS * RET_DK) // RET_DV
    g_blocks0 = v_blocks0 + RET_HEADS
    smem = pl.BlockSpec(memory_space=pltpu.SMEM)
    scratch = [pltpu.VMEM((RET_DK, RET_DV), F32), pltpu.VMEM((cs, cs), F32),
               pltpu.VMEM((cs, 1), F32), pltpu.VMEM((cs, 1), F32)]

    def run(backward, lg, extra):
        def cmap(c):
            return (nc - 1 - c) if backward else c
        in_specs = [smem, smem,
                    pl.BlockSpec((cs, RET_DK), lambda h, c: (cmap(c), h)),
                    pl.BlockSpec((cs, RET_DK), lambda h, c: (cmap(c), qk_blocks + h)),
                    pl.BlockSpec((cs, RET_DV), lambda h, c: (cmap(c), v_blocks0 + h))]
        args = [lg, jnp.exp(cs * lg), proj, proj, proj]
        if backward:
            in_specs += [pl.BlockSpec((cs, RET_DV), lambda h, c: (cmap(c), h)),
                         pl.BlockSpec((cs, RET_DV), lambda h, c: (cmap(c), g_blocks0 + h))]
            args += [extra, proj]
        return pl.pallas_call(
            functools.partial(_ret_kernel, backward=backward, segs_chunks=segs_chunks),
            grid=(RET_HEADS, nc),
            in_specs=in_specs,
            out_specs=pl.BlockSpec((cs, RET_DV), lambda h, c: (cmap(c), h)),
            out_shape=jax.ShapeDtypeStruct((t, RET_HEADS * RET_DV), BF16 if backward else F32),
            scratch_shapes=scratch,
            compiler_params=_cparams(("arbitrary", "arbitrary")),
            name="retention_bwd" if backward else "retention_fwd",
        )(*args)

    o_f = run(False, lg_f, None)
    return run(True, lg_b, o_f)


def _attn_kernel(lam_ref, q_ref, k_ref, v_ref, sg_ref, o_ref, *, out_scale):
    v = v_ref[...]
    outs = []
    for c in range(2):
        q = q_ref[:, c * DIFF_DH:(c + 1) * DIFF_DH]
        k = k_ref[:, c * DIFF_DH:(c + 1) * DIFF_DH]
        s = lax.dot_general(q, k, (((1,), (1,)), ((), ())), preferred_element_type=F32)
        p = jnp.exp(s - jnp.max(s, axis=-1, keepdims=True))
        den = jnp.sum(p, axis=-1, keepdims=True)
        outs.append(jnp.dot(p.astype(BF16), v, preferred_element_type=F32) / den)
    o = outs[0] - lam_ref[0] * outs[1]
    ms = jnp.mean(o * o, axis=-1, keepdims=True)
    o_ref[...] = (o * lax.rsqrt(ms + LN_EPS) * sg_ref[...] * out_scale).astype(o_ref.dtype)


def _diff_attention(proj, lam, subln_g, segs, lambda_init):
    nq_blocks = DIFF_HEADS
    outs = []
    for start, s, n_seq in segs:
        tq = ATTN_Q_TILE
        assert s % tq == 0 and start % s == 0
        qb0 = start // tq
        kb0 = start // s
        nqt = s // tq
        out = pl.pallas_call(
            functools.partial(_attn_kernel, out_scale=1.0 - lambda_init),
            grid=(n_seq, DIFF_HEADS, nqt),
            in_specs=[pl.BlockSpec(memory_space=pltpu.SMEM),
                      pl.BlockSpec((tq, 2 * DIFF_DH), lambda b, h, i: (qb0 + b * nqt + i, h)),
                      pl.BlockSpec((s, 2 * DIFF_DH), lambda b, h, i: (kb0 + b, nq_blocks + h)),
                      pl.BlockSpec((s, DIFF_DV), lambda b, h, i: (kb0 + b, 2 * nq_blocks + h)),
                      pl.BlockSpec((1, DIFF_DV), lambda b, h, i: (0, 0))],
            out_specs=pl.BlockSpec((tq, DIFF_DV), lambda b, h, i: (b * nqt + i, h)),
            out_shape=jax.ShapeDtypeStruct((n_seq * s, DIFF_HEADS * DIFF_DV), BF16),
            compiler_params=_cparams(("arbitrary", "arbitrary", "arbitrary")),
            name=f"diff_attn_s{s}",
        )(lam, proj, proj, proj, subln_g.reshape(1, DIFF_DV))
        outs.append(out)
    return jnp.concatenate(outs, axis=0)


def _first_max(cur, rows, n_rows):
    m = jnp.max(cur, axis=0, keepdims=True)
    idx = jnp.min(jnp.where(cur == m, rows, float(n_rows)), axis=0, keepdims=True)
    return m, idx


def _router_kernel(x_ref, wt_ref, b_ref, idx_ref, gate_ref, rank_ref, cnt_ref, carry_ref):
    i = pl.program_id(0)

    @pl.when(i == 0)
    def _init():
        carry_ref[...] = jnp.zeros_like(carry_ref)

    tn = x_ref.shape[0]
    x = x_ref[...]
    xh = x.astype(BF16)
    xl = (x - xh.astype(F32)).astype(BF16)
    w = wt_ref[...]
    wh = w.astype(BF16)
    wl = (w - wh.astype(F32)).astype(BF16)

    def nt(a, b):
        return lax.dot_general(a, b, (((1,), (1,)), ((), ())), preferred_element_type=F32)

    logits = nt(wh, xh) + nt(wh, xl) + nt(wl, xh)
    scores = 1.0 / (1.0 + jnp.exp(-logits))
    biased = scores + b_ref[...]

    rows_g = lax.broadcasted_iota(I32, (GROUP_SIZE, tn), 0).astype(F32)
    rows_e = lax.broadcasted_iota(I32, (N_EXPERTS, tn), 0).astype(F32)

    gscore = []
    for g in range(N_GROUPS):
        slab = biased[g * GROUP_SIZE:(g + 1) * GROUP_SIZE]
        m1, i1 = _first_max(slab, rows_g, GROUP_SIZE)
        m2 = jnp.max(jnp.where(rows_g == i1, NEG_INF, slab), axis=0, keepdims=True)
        gscore.append(m1 + m2)
    cur = jnp.concatenate(gscore, axis=0)
    rows_grp = lax.broadcasted_iota(I32, (N_GROUPS, tn), 0).astype(F32)
    gsel = jnp.zeros((N_GROUPS, tn), F32)
    for _ in range(TOPK_GROUPS):
        _, gi = _first_max(cur, rows_grp, N_GROUPS)
        pick = rows_grp == gi
        gsel = jnp.where(pick, 1.0, gsel)
        cur = jnp.where(pick, NEG_INF, cur)

    masked = jnp.concatenate(
        [jnp.where(gsel[g:g + 1] > 0.0, biased[g * GROUP_SIZE:(g + 1) * GROUP_SIZE], NEG_INF)
         for g in range(N_GROUPS)], axis=0)

    sel = jnp.zeros((N_EXPERTS, tn), F32)
    picks = []
    cur = masked
    for _ in range(TOP_K):
        _, ei = _first_max(cur, rows_e, N_EXPERTS)
        pick = rows_e == ei
        sel = jnp.where(pick, 1.0, sel)
        cur = jnp.where(pick, NEG_INF, cur)
        picks.append(ei)

    gsum = jnp.sum(scores * sel, axis=0, keepdims=True)

    ta = lax.broadcasted_iota(I32, (tn, tn), 0)
    tb = lax.broadcasted_iota(I32, (tn, tn), 1)
    before = jnp.where(ta < tb, 1.0, 0.0).astype(BF16)
    rank = jnp.dot(sel.astype(BF16), before, preferred_element_type=F32) + carry_ref[...]
    carry_ref[...] = carry_ref[...] + jnp.sum(sel, axis=1, keepdims=True)
    cnt_ref[...] = carry_ref[...]

    gates = []
    ranks = []
    for ei in picks:
        hit = rows_e == ei
        gates.append(jnp.sum(jnp.where(hit, scores, 0.0), axis=0, keepdims=True) / gsum * ROUTED_SCALE)
        ranks.append(jnp.sum(jnp.where(hit, rank, 0.0), axis=0, keepdims=True))
    idx_ref[...] = jnp.concatenate(picks, axis=0).astype(I32)
    gate_ref[...] = jnp.concatenate(gates, axis=0)
    rank_ref[...] = jnp.concatenate(ranks, axis=0).astype(I32)


def _router(x, w_router, b_router):
    t, d = x.shape
    tn = _pick_tile(t, (512, 256))
    kt = pl.BlockSpec((TOP_K, tn), lambda i: (0, i))
    return pl.pallas_call(
        _router_kernel,
        grid=(t // tn,),
        in_specs=[pl.BlockSpec((tn, d), lambda i: (i, 0)),
                  pl.BlockSpec((N_EXPERTS, d), lambda i: (0, 0)),
                  pl.BlockSpec((N_EXPERTS, 1), lambda i: (0, 0))],
        out_specs=[kt, kt, kt, pl.BlockSpec((N_EXPERTS, 1), lambda i: (0, 0))],
        out_shape=[jax.ShapeDtypeStruct((TOP_K, t), I32), jax.ShapeDtypeStruct((TOP_K, t), F32),
                   jax.ShapeDtypeStruct((TOP_K, t), I32), jax.ShapeDtypeStruct((N_EXPERTS, 1), F32)],
        scratch_shapes=[pltpu.VMEM((N_EXPERTS, 1), F32)],
        compiler_params=_cparams(("arbitrary",)),
        name="router",
    )(x, w_router.T, b_router.reshape(N_EXPERTS, 1))


def _slot_kernel(idx_ref, rank_ref, start_ref, slot_ref):
    tn = idx_ref.shape[1]
    rows_e = lax.broadcasted_iota(I32, (N_EXPERTS, tn), 0)
    out = []
    for k in range(TOP_K):
        hit = rows_e == idx_ref[k:k + 1, :]
        base = jnp.sum(jnp.where(hit, start_ref[...], 0.0), axis=0, keepdims=True)
        out.append(base.astype(I32) + rank_ref[k:k + 1, :])
    slot_ref[...] = jnp.concatenate(out, axis=0)


def _slots(idx_t, rank_t, pad_start):
    t = idx_t.shape[1]
    tn = _pick_tile(t, (512, 256))
    kt = pl.BlockSpec((TOP_K, tn), lambda i: (0, i))
    return pl.pallas_call(
        _slot_kernel,
        grid=(t // tn,),
        in_specs=[kt, kt, pl.BlockSpec((N_EXPERTS, 1), lambda i: (0, 0))],
        out_specs=kt,
        out_shape=jax.ShapeDtypeStruct((TOP_K, t), I32),
        compiler_params=_cparams(("arbitrary",)),
        name="slots",
    )(idx_t, rank_t, pad_start.astype(F32).reshape(N_EXPERTS, 1))


def _row_copy(src, src_row, dst, dst_row, sem):
    return pltpu.make_async_copy(src.at[pl.ds(src_row, 1)], dst.at[pl.ds(dst_row, 1)], sem)


def _dispatch_kernel(pad_end_ref, padded_ref, slot_hbm, x_ref, xs_hbm, slot_smem, zero_ref, sem, slot_sem):
    i = pl.program_id(0)
    tt = x_ref.shape[0]
    bm = zero_ref.shape[0]
    slot_cp = pltpu.make_async_copy(slot_hbm.at[i], slot_smem, slot_sem)
    slot_cp.start()

    @pl.when(i == 0)
    def _clear():
        zero_ref[...] = jnp.zeros_like(zero_ref)

        def pad_copy(e):
            off = pl.multiple_of(pad_end_ref[e] - bm, bm)
            return pltpu.make_async_copy(zero_ref, xs_hbm.at[pl.ds(off, bm)], sem)

        def start(e, carry):
            @pl.when(padded_ref[e] > 0)
            def _():
                pad_copy(e).start()
            return carry

        def wait(e, carry):
            @pl.when(padded_ref[e] > 0)
            def _():
                pad_copy(e).wait()
            return carry

        lax.fori_loop(0, N_EXPERTS, start, 0)
        lax.fori_loop(0, N_EXPERTS, wait, 0)

    slot_cp.wait()
    per_row = LANES // TOP_K

    def issue(r, carry):
        for j in range(LANES):
            _row_copy(x_ref, r * per_row + j // TOP_K, xs_hbm, slot_smem[r, j], sem).start()
        return carry

    lax.fori_loop(0, tt // per_row, issue, 0)

    def drain(t, carry):
        for k in range(TOP_K):
            _row_copy(x_ref, t, xs_hbm, 0, sem).wait()
        return carry

    lax.fori_loop(0, tt, drain, 0)


def _dispatch(x, slot_tiles, pad_end, padded, n_slots):
    t, d = x.shape
    tt = DISPATCH_TILE
    grid_spec = pltpu.PrefetchScalarGridSpec(
        num_scalar_prefetch=2,
        grid=(t // tt,),
        in_specs=[pl.BlockSpec(memory_space=pl.ANY),
                  pl.BlockSpec((tt, d), lambda i, pe, pd: (i, 0))],
        out_specs=pl.BlockSpec(memory_space=pl.ANY),
        scratch_shapes=[pltpu.SMEM((tt * TOP_K // LANES, LANES), I32),
                        pltpu.VMEM((EXPERT_ROWS, d), F32),
                        pltpu.SemaphoreType.DMA, pltpu.SemaphoreType.DMA],
    )
    return pl.pallas_call(
        _dispatch_kernel,
        grid_spec=grid_spec,
        out_shape=jax.ShapeDtypeStruct((n_slots, d), F32),
        compiler_params=_cparams(("arbitrary",)),
        name="dispatch",
    )(pad_end, padded, slot_tiles, x)


def _ffn_kernel(block_e_ref, n_used_ref, x_ref, wgu_ref, wd_ref, o_ref):
    b = pl.program_id(0)

    @pl.when(b < n_used_ref[0])
    def _():
        hidden = wd_ref.shape[1]
        gu = jnp.dot(x_ref[...].astype(BF16), wgu_ref[0], preferred_element_type=F32)
        gate = gu[:, :hidden]
        act = gate * (1.0 / (1.0 + jnp.exp(-gate))) * gu[:, hidden:]
        o_ref[...] = jnp.dot(act.astype(BF16), wd_ref[0], preferred_element_type=F32)


def _ffn(x, w_gu, w_down, block_e, n_used, rows, name):
    n, d = x.shape
    hidden = w_down.shape[1]

    def row_map(b, be, nu):
        return (jnp.minimum(b, nu[0] - 1), 0)

    grid_spec = pltpu.PrefetchScalarGridSpec(
        num_scalar_prefetch=2,
        grid=(n // rows,),
        in_specs=[pl.BlockSpec((rows, d), row_map),
                  pl.BlockSpec((1, d, 2 * hidden), lambda b, be, nu: (be[b], 0, 0)),
                  pl.BlockSpec((1, hidden, d), lambda b, be, nu: (be[b], 0, 0))],
        out_specs=pl.BlockSpec((rows, d), row_map),
    )
    return pl.pallas_call(
        _ffn_kernel,
        grid_spec=grid_spec,
        out_shape=jax.ShapeDtypeStruct((n, d), F32),
        compiler_params=_cparams(("arbitrary",)),
        name=name,
    )(block_e, n_used, x, w_gu, w_down)


def _combine_kernel(slot_hbm, x_ref, sh_ref, gate_ref, g_ref, b_ref, ys_hbm, o_ref, ob_ref,
                    slot_smem, buf_ref, sems, slot_sem):
    i = pl.program_id(0)
    n = pl.num_programs(0)
    tt = x_ref.shape[0]
    per_row = LANES // TOP_K

    def fetch(tile):
        par = tile % 2
        slot_cp = pltpu.make_async_copy(slot_hbm.at[tile], slot_smem.at[par], slot_sem)
        slot_cp.start()
        slot_cp.wait()

        def issue(r, carry):
            for j in range(LANES):
                _row_copy(ys_hbm, slot_smem[par, r, j], buf_ref.at[par, j % TOP_K], r * per_row + j // TOP_K,
                          sems.at[par]).start()
            return carry

        lax.fori_loop(0, tt // per_row, issue, 0)

    @pl.when(i == 0)
    def _():
        fetch(i)

    @pl.when(i + 1 < n)
    def _():
        fetch(i + 1)

    par = i % 2

    def drain(t, carry):
        for k in range(TOP_K):
            _row_copy(ys_hbm, 0, buf_ref.at[par, k], t, sems.at[par]).wait()
        return carry

    lax.fori_loop(0, tt, drain, 0)

    h = sh_ref[...]
    gate = gate_ref[...]
    for k in range(TOP_K):
        h = h + buf_ref[par, k] * gate[:, k:k + 1]
    y = _layer_norm_rows(ALPHA * x_ref[...] + h, g_ref[...], b_ref[...])
    o_ref[...] = y
    ob_ref[...] = y.astype(BF16)


def _combine(x, shared, ys, slot_tiles, gate, g, b):
    t, d = x.shape
    tt = COMBINE_TILE
    row = pl.BlockSpec((tt, d), lambda i: (i, 0))
    vec = pl.BlockSpec((1, d), lambda i: (0, 0))
    return pl.pallas_call(
        _combine_kernel,
        grid=(t // tt,),
        in_specs=[pl.BlockSpec(memory_space=pl.ANY), row, row,
                  pl.BlockSpec((tt, TOP_K), lambda i: (i, 0)), vec, vec,
                  pl.BlockSpec(memory_space=pl.ANY)],
        out_specs=[row, row],
        out_shape=[jax.ShapeDtypeStruct((t, d), F32), jax.ShapeDtypeStruct((t, d), BF16)],
        scratch_shapes=[pltpu.SMEM((2, tt * TOP_K // LANES, LANES), I32),
                        pltpu.VMEM((2, TOP_K, tt, d), F32),
                        pltpu.SemaphoreType.DMA((2,)), pltpu.SemaphoreType.DMA],
        compiler_params=_cparams(("arbitrary",)),
        name="combine",
    )(slot_tiles, x, shared, gate, g.reshape(1, d), b.reshape(1, d), ys)


def _moe_layer(x, w_router, b_router, w_gu, w_down, sw_gu, sw_down, ln_g, ln_b):
    t, d = x.shape
    bm = EXPERT_ROWS
    idx_t, gate_t, rank_t, counts = _router(x, w_router, b_router)

    counts = counts.reshape(N_EXPERTS).astype(I32)
    padded = (counts + bm - 1) // bm * bm
    pad_end = jnp.cumsum(padded)
    pad_start = pad_end - padded
    n_blocks = (t * TOP_K + N_EXPERTS * (bm - 1) + bm - 1) // bm
    n_slots = n_blocks * bm
    block_first = jnp.arange(n_blocks, dtype=I32) * bm
    block_e = jnp.minimum(jnp.sum((pad_end[None, :] <= block_first[:, None]).astype(I32), axis=1), N_EXPERTS - 1)
    n_used = (pad_end[-1:] // bm).astype(I32)

    slot_t = _slots(idx_t, rank_t, pad_start)
    slot_flat = slot_t.T.reshape(t * TOP_K // LANES, LANES)

    def slot_tiles(tile):
        return slot_flat.reshape(t // tile, tile * TOP_K // LANES, LANES)

    xs = _dispatch(x, slot_tiles(DISPATCH_TILE), pad_end.astype(I32), padded.astype(I32), n_slots)
    ys = _ffn(xs, w_gu, w_down, block_e, n_used, bm, "expert_ffn")
    shared_rows = _pick_tile(t, (512, 256))
    shared = _ffn(x, sw_gu[None], sw_down[None], jnp.zeros((t // shared_rows,), I32),
                  jnp.full((1,), t // shared_rows, I32), shared_rows, "shared_ffn")
    return _combine(x, shared, ys, slot_tiles(COMBINE_TILE), gate_t.T, ln_g, ln_b)


def _rope_angles(s, theta, half):
    inv = theta ** (-jnp.arange(half, dtype=F32) / half)
    ang = jnp.arange(s, dtype=F32)[:, None] * inv[None, :]
    return jnp.cos(ang), jnp.sin(ang)


def _ret_tables(s):
    cos, sin = _rope_angles(s, RET_THETA, RET_DK // 2)
    return cos, sin, sin


def _diff_tables(s):
    half = ROPE_DIMS // 2
    cos, sin = _rope_angles(s, ROPE_THETA, half)
    rest = DIFF_DH - ROPE_DIMS
    zeros_h = jnp.zeros((s, half), F32)
    c = jnp.concatenate([cos, cos, jnp.ones((s, rest), F32)], axis=1)
    s_up = jnp.concatenate([zeros_h, sin, jnp.zeros((s, rest), F32)], axis=1)
    s_dn = jnp.concatenate([-sin, zeros_h, jnp.zeros((s, rest), F32)], axis=1)
    return c, s_up, s_dn


def kernel(x_prompt, x_sample, ret_w_in, ret_decay_f, ret_decay_b, ret_w_out, diff_w_in, diff_lam_q1, diff_lam_k1,
           diff_lam_q2, diff_lam_k2, diff_subln_g, diff_w_out, ln_mix_g, ln_mix_b, router_w, router_b, exp_w_gu,
           exp_w_down, shared_w_gu, shared_w_down, ln_ffn_g, ln_ffn_b):
    bp, sp, d = x_prompt.shape
    bs, ss, _ = x_sample.shape
    tp = bp * sp
    x = jnp.concatenate([x_prompt.reshape(tp, d), x_sample.reshape(bs * ss, d)], axis=0)
    xb = x.astype(BF16)
    segs = [(0, sp), (tp, ss)]
    s_max = max(sp, ss)

    for i in range(DEPTH):
        j = i // 2
        if i % 2 == 0:
            qk = RET_HEADS * RET_DK
            proj = _proj(xb, ret_w_in[j].astype(BF16), _ret_tables(s_max), segs, mode="ret",
                         n_q_cols=qk, n_qk_cols=2 * qk, q_scale=1.0, k_scale=RET_DK ** -0.5)
            lg_f = -jax.nn.softplus(-ret_decay_f[j].astype(F32))
            lg_b = -jax.nn.softplus(-ret_decay_b[j].astype(F32))
            mixed = _retention(proj, lg_f, lg_b, segs)
            h = _matmul(mixed, ret_w_out[j].astype(BF16), F32, "ret_out")
        else:
            lambda_init = 0.8 - 0.6 * math.exp(-0.3 * i)
            nq = DIFF_HEADS * 2 * DIFF_DH
            proj = _proj(xb, diff_w_in[j].astype(BF16), _diff_tables(s_max), segs, mode="diff",
                         n_q_cols=nq, n_qk_cols=2 * nq, q_scale=DIFF_DH ** -0.5, k_scale=1.0)
            lam = (jnp.exp(jnp.sum(diff_lam_q1[j].astype(F32) * diff_lam_k1[j].astype(F32)))
                   - jnp.exp(jnp.sum(diff_lam_q2[j].astype(F32) * diff_lam_k2[j].astype(F32))) + lambda_init)
            mixed = _diff_attention(proj, lam.reshape(1), diff_subln_g[j],
                                    [(0, sp, bp), (tp, ss, bs)], lambda_init)
            h = _matmul(mixed, diff_w_out[j].astype(BF16), F32, "diff_out")
        x, xb = _add_ln(x, h, ln_mix_g[i], ln_mix_b[i])
        x, xb = _moe_layer(x, router_w[i], router_b[i], exp_w_gu[i].astype(BF16), exp_w_down[i].astype(BF16),
                           shared_w_gu[i].astype(BF16), shared_w_down[i].astype(BF16), ln_ffn_g[i], ln_ffn_b[i])
    return x[:tp].reshape(bp, sp, d), x[tp:].reshape(bs, ss, d)
```

```python
import functools
import math

import jax
import jax.numpy as jnp
from jax import lax
from jax.experimental import pallas as pl
from jax.experimental.pallas import tpu as pltpu

F32 = jnp.float32
BF16 = jnp.bfloat16
I32 = jnp.int32
U32 = jnp.uint32

D_MODEL = 2048
DEPTH = 2
ALPHA = (2 * DEPTH) ** 0.25
LN_EPS = 1e-5
RET_HEADS = 8
RET_DK = D_MODEL // RET_HEADS
RET_DV = 2 * RET_DK
RET_THETA = 10000.0
DIFF_HEADS = 8
DIFF_DH = D_MODEL // (2 * DIFF_HEADS)
DIFF_DV = 2 * DIFF_DH
ROPE_THETA = 500000.0
ROPE_DIMS = DIFF_DH // 4
N_EXPERTS = 64
TOP_K = 8
N_GROUPS = 8
TOPK_GROUPS = 4
GROUP_SIZE = N_EXPERTS // N_GROUPS
EXPERT_HIDDEN = 512
SHARED_HIDDEN = 512
ROUTED_SCALE = 2.5

LANES = 128
SUBLANES = 8
VMEM_LIMIT = 48 * 1024 * 1024

RET_CHUNK = 256
EXPERT_ROWS = 256
DISPATCH_TILE = 256
COMBINE_TILE = 128
ATTN_Q_TILE = 256
NEG_INF = float("-inf")


def _cparams(sem):
    return pltpu.CompilerParams(dimension_semantics=sem, vmem_limit_bytes=VMEM_LIMIT)


def _pick_tile(n, prefs):
    for p in prefs:
        if n % p == 0:
            return p
    raise ValueError(f"no tile in {prefs} divides {n}")


def _seg_local_block(i, segs_blocks):
    val = None
    for first, per in segs_blocks:
        loc = lax.rem(i - first, per)
        val = loc if val is None else jnp.where(i >= first, loc, val)
    return val


def _proj_kernel(x_ref, w_ref, t0_ref, t1_ref, t2_ref, o_ref, *, mode, n_q_tiles, n_qk_tiles, q_scale, k_scale):
    j = pl.program_id(1)
    acc = jnp.dot(x_ref[...], w_ref[...], preferred_element_type=F32)
    tn = acc.shape[1]

    @pl.when(j >= n_qk_tiles)
    def _plain():
        o_ref[...] = acc.astype(o_ref.dtype)

    @pl.when(j < n_qk_tiles)
    def _rope():
        scale = jnp.where(j < n_q_tiles, q_scale, k_scale)
        if mode == "ret":
            cos = t0_ref[...]
            sin = t1_ref[...]
            half = RET_DK // 2
            for hs in range(tn // RET_DK):
                lo = hs * RET_DK
                x1 = acc[:, lo:lo + half]
                x2 = acc[:, lo + half:lo + RET_DK]
                o_ref[:, lo:lo + half] = ((x1 * cos - x2 * sin) * scale).astype(o_ref.dtype)
                o_ref[:, lo + half:lo + RET_DK] = ((x2 * cos + x1 * sin) * scale).astype(o_ref.dtype)
        else:
            c = t0_ref[...]
            s_up = t1_ref[...]
            s_dn = t2_ref[...]
            half = ROPE_DIMS // 2
            for g in range(tn // DIFF_DH):
                lo = g * DIFF_DH
                seg = acc[:, lo:lo + DIFF_DH]
                rot = seg * c + pltpu.roll(seg, half, 1) * s_up + pltpu.roll(seg, DIFF_DH - half, 1) * s_dn
                o_ref[:, lo:lo + DIFF_DH] = (rot * scale).astype(o_ref.dtype)


def _proj(x, w, tables, segs, *, mode, n_q_cols, n_qk_cols, q_scale, k_scale):
    m, k = x.shape
    n = w.shape[1]
    seg_len = [s for _, s in segs]
    tm = _pick_tile(math.gcd(*seg_len), (1024, 512, 256))
    tn = 1024
    segs_blocks = [(start // tm, s // tm) for start, s in segs]
    tw = tables[0].shape[1]
    tab_spec = pl.BlockSpec((tm, tw), lambda i, j: (_seg_local_block(i, segs_blocks), 0))
    kern = functools.partial(_proj_kernel, mode=mode, n_q_tiles=n_q_cols // tn, n_qk_tiles=n_qk_cols // tn,
                             q_scale=q_scale, k_scale=k_scale)
    return pl.pallas_call(
        kern,
        grid=(m // tm, n // tn),
        in_specs=[pl.BlockSpec((tm, k), lambda i, j: (i, 0)),
                  pl.BlockSpec((k, tn), lambda i, j: (0, j)),
                  tab_spec, tab_spec, tab_spec],
        out_specs=pl.BlockSpec((tm, tn), lambda i, j: (i, j)),
        out_shape=jax.ShapeDtypeStruct((m, n), BF16),
        compiler_params=_cparams(("arbitrary", "arbitrary")),
        name=f"proj_{mode}",
    )(x, w, *tables)


HIGH_HALF = 0xFFFF0000


def _pack_rows(y):
    n = y.shape[1] // 2
    lo = lax.bitcast_convert_type(y[:, :n].astype(BF16).astype(F32), U32) >> 16
    hi = lax.bitcast_convert_type(y[:, n:].astype(BF16).astype(F32), U32) & jnp.uint32(HIGH_HALF)
    return lo | hi


def _unpack_rows(w):
    lo = lax.bitcast_convert_type(w << 16, F32)
    hi = lax.bitcast_convert_type(w & jnp.uint32(HIGH_HALF), F32)
    return lo, hi


def _layer_norm_rows(z, g, b):
    mu = jnp.mean(z, axis=-1, keepdims=True)
    zc = z - mu
    var = jnp.mean(zc * zc, axis=-1, keepdims=True)
    return zc * lax.rsqrt(var + LN_EPS) * g + b


def _out_ln_kernel(a_ref, w_ref, x_ref, g_ref, b_ref, o_ref, op_ref, acc_ref):
    k = pl.program_id(1)
    part = jnp.dot(a_ref[...], w_ref[...], preferred_element_type=F32)

    @pl.when(k == 0)
    def _():
        acc_ref[...] = part

    @pl.when(k > 0)
    def _():
        acc_ref[...] += part

    @pl.when(k == pl.num_programs(1) - 1)
    def _():
        y = _layer_norm_rows(ALPHA * x_ref[...] + acc_ref[...], g_ref[...], b_ref[...])
        o_ref[...] = y
        op_ref[...] = _pack_rows(y)


def _out_ln(a, w, x, g, b, name):
    m, kk = a.shape
    d = w.shape[1]
    tm = _pick_tile(m, (512, 256))
    tk = 1024
    row = pl.BlockSpec((tm, d), lambda i, k: (i, 0))
    vec = pl.BlockSpec((1, d), lambda i, k: (0, 0))
    return pl.pallas_call(
        _out_ln_kernel,
        grid=(m // tm, kk // tk),
        in_specs=[pl.BlockSpec((tm, tk), lambda i, k: (i, k)),
                  pl.BlockSpec((tk, d), lambda i, k: (k, 0)), row, vec, vec],
        out_specs=[row, pl.BlockSpec((tm, d // 2), lambda i, k: (i, 0))],
        out_shape=[jax.ShapeDtypeStruct((m, d), F32), jax.ShapeDtypeStruct((m, d // 2), U32)],
        scratch_shapes=[pltpu.VMEM((tm, d), F32)],
        compiler_params=_cparams(("arbitrary", "arbitrary")),
        name=name,
    )(a, w, x, g.reshape(1, d), b.reshape(1, d))


def _ret_kernel(lg_ref, cd_ref, q_ref, k_ref, v_ref, *rest, backward, segs_chunks):
    if backward:
        of_ref, g_ref, o_ref, state_ref, dmat_ref, qd_ref, kd_ref = rest
    else:
        o_ref, state_ref, dmat_ref, qd_ref, kd_ref = rest
    h = pl.program_id(0)
    c = pl.program_id(1)
    nc = pl.num_programs(1)
    cc = (nc - 1 - c) if backward else c
    lg = lg_ref[h]
    cs = RET_CHUNK

    @pl.when(c == 0)
    def _tables():
        ii = lax.broadcasted_iota(I32, (cs, cs), 0)
        jj = lax.broadcasted_iota(I32, (cs, cs), 1)
        if backward:
            mask = jj > ii
            dist = (jj - ii).astype(F32)
        else:
            mask = ii >= jj
            dist = (ii - jj).astype(F32)
        dmat_ref[...] = jnp.where(mask, jnp.exp(jnp.where(mask, dist, 0.0) * lg), 0.0)
        pos = lax.broadcasted_iota(I32, (cs, 1), 0).astype(F32)
        if backward:
            qd_ref[...] = jnp.exp((cs - pos) * lg)
            kd_ref[...] = jnp.exp(pos * lg)
        else:
            qd_ref[...] = jnp.exp((pos + 1.0) * lg)
            kd_ref[...] = jnp.exp((cs - 1.0 - pos) * lg)

    loc = _seg_local_block(cc, segs_chunks)
    if backward:
        per = None
        for first, p in segs_chunks:
            per = p if per is None else jnp.where(cc >= first, p, per)
        is_start = loc == per - 1
    else:
        is_start = loc == 0

    @pl.when(is_start)
    def _reset():
        state_ref[...] = jnp.zeros_like(state_ref)

    q = q_ref[...]
    k = k_ref[...]
    v = v_ref[...]
    att = lax.dot_general(q, k, (((1,), (1,)), ((), ())), preferred_element_type=F32) * dmat_ref[...]
    inner = jnp.dot(att.astype(BF16), v, preferred_element_type=F32)
    state = state_ref[...]
    cross = jnp.dot(q, state.astype(BF16), preferred_element_type=F32) * qd_ref[...]
    o = inner + cross
    kt = (k.astype(F32) * kd_ref[...]).T.astype(BF16)
    state_ref[...] = state * cd_ref[h] + jnp.dot(kt, v, preferred_element_type=F32)

    if backward:
        tot = of_ref[...] + o
        mu = jnp.mean(tot, axis=-1, keepdims=True)
        tc = tot - mu
        var = jnp.mean(tc * tc, axis=-1, keepdims=True)
        on = tc * lax.rsqrt(var + LN_EPS)
        g = g_ref[...].astype(F32)
        o_ref[...] = ((g * (1.0 / (1.0 + jnp.exp(-g)))) * on).astype(o_ref.dtype)
    else:
        o_ref[...] = o


def _retention(proj, lg_f, lg_b, segs):
    t = proj.shape[0]
    cs = RET_CHUNK
    nc = t // cs
    segs_chunks = [(start // cs, s // cs) for start, s in segs]
    qk_blocks = (RET_HEADS * RET_DK) // RET_DK
    v_blocks0 = (2 * RET_HEADS * RET_DK) // RET_DV
    g_blocks0 = v_blocks0 + RET_HEADS
    smem = pl.BlockSpec(memory_space=pltpu.SMEM)
    scratch = [pltpu.VMEM((RET_DK, RET_DV), F32), pltpu.VMEM((cs, cs), F32),
               pltpu.VMEM((cs, 1), F32), pltpu.VMEM((cs, 1), F32)]

    def run(backward, lg, extra):
        def cmap(c):
            return (nc - 1 - c) if backward else c
        in_specs = [smem, smem,
                    pl.BlockSpec((cs, RET_DK), lambda h, c: (cmap(c), h)),
                    pl.BlockSpec((cs, RET_DK), lambda h, c: (cmap(c), qk_blocks + h)),
                    pl.BlockSpec((cs, RET_DV), lambda h, c: (cmap(c), v_blocks0 + h))]
        args = [lg, jnp.exp(cs * lg), proj, proj, proj]
        if backward:
            in_specs += [pl.BlockSpec((cs, RET_DV), lambda h, c: (cmap(c), h)),
                         pl.BlockSpec((cs, RET_DV), lambda h, c: (cmap(c), g_blocks0 + h))]
            args += [extra, proj]
        return pl.pallas_call(
            functools.partial(_ret_kernel, backward=backward, segs_chunks=segs_chunks),
            grid=(RET_HEADS, nc),
            in_specs=in_specs,
            out_specs=pl.BlockSpec((cs, RET_DV), lambda h, c: (cmap(c), h)),
            out_shape=jax.ShapeDtypeStruct((t, RET_HEADS * RET_DV), BF16 if backward else F32),
            scratch_shapes=scratch,
            compiler_params=_cparams(("arbitrary", "arbitrary")),
            name="retention_bwd" if backward else "retention_fwd",
        )(*args)

    o_f = run(False, lg_f, None)
    return run(True, lg_b, o_f)


def _attn_kernel(lam_ref, q_ref, k_ref, v_ref, sg_ref, o_ref, *, out_scale):
    v = v_ref[...]
    outs = []
    for c in range(2):
        q = q_ref[:, c * DIFF_DH:(c + 1) * DIFF_DH]
        k = k_ref[:, c * DIFF_DH:(c + 1) * DIFF_DH]
        s = lax.dot_general(q, k, (((1,), (1,)), ((), ())), preferred_element_type=F32)
        p = jnp.exp(s - jnp.max(s, axis=-1, keepdims=True))
        den = jnp.sum(p, axis=-1, keepdims=True)
        outs.append(jnp.dot(p.astype(BF16), v, preferred_element_type=F32) / den)
    o = outs[0] - lam_ref[0] * outs[1]
    ms = jnp.mean(o * o, axis=-1, keepdims=True)
    o_ref[...] = (o * lax.rsqrt(ms + LN_EPS) * sg_ref[...] * out_scale).astype(o_ref.dtype)


def _diff_attention(proj, lam, subln_g, segs, lambda_init):
    nq_blocks = DIFF_HEADS
    outs = []
    for start, s, n_seq in segs:
        tq = ATTN_Q_TILE
        assert s % tq == 0 and start % s == 0
        qb0 = start // tq
        kb0 = start // s
        nqt = s // tq
        out = pl.pallas_call(
            functools.partial(_attn_kernel, out_scale=1.0 - lambda_init),
            grid=(n_seq, DIFF_HEADS, nqt),
            in_specs=[pl.BlockSpec(memory_space=pltpu.SMEM),
                      pl.BlockSpec((tq, 2 * DIFF_DH), lambda b, h, i: (qb0 + b * nqt + i, h)),
                      pl.BlockSpec((s, 2 * DIFF_DH), lambda b, h, i: (kb0 + b, nq_blocks + h)),
                      pl.BlockSpec((s, DIFF_DV), lambda b, h, i: (kb0 + b, 2 * nq_blocks + h)),
                      pl.BlockSpec((1, DIFF_DV), lambda b, h, i: (0, 0))],
            out_specs=pl.BlockSpec((tq, DIFF_DV), lambda b, h, i: (b * nqt + i, h)),
            out_shape=jax.ShapeDtypeStruct((n_seq * s, DIFF_HEADS * DIFF_DV), BF16),
            compiler_params=_cparams(("arbitrary", "arbitrary", "arbitrary")),
            name=f"diff_attn_s{s}",
        )(lam, proj, proj, proj, subln_g.reshape(1, DIFF_DV))
        outs.append(out)
    return jnp.concatenate(outs, axis=0)


def _first_max(cur, rows, n_rows):
    m = jnp.max(cur, axis=0, keepdims=True)
    idx = jnp.min(jnp.where(cur == m, rows, float(n_rows)), axis=0, keepdims=True)
    return m, idx


def _router_kernel(x_ref, wt_ref, b_ref, idx_ref, gate_ref, rank_ref, cnt_ref, carry_ref):
    i = pl.program_id(0)

    @pl.when(i == 0)
    def _init():
        carry_ref[...] = jnp.zeros_like(carry_ref)

    tn = x_ref.shape[0]
    x = x_ref[...]
    xh = x.astype(BF16)
    xl = (x - xh.astype(F32)).astype(BF16)
    w = wt_ref[...]
    wh = w.astype(BF16)
    wl = (w - wh.astype(F32)).astype(BF16)

    def nt(a, b):
        return lax.dot_general(a, b, (((1,), (1,)), ((), ())), preferred_element_type=F32)

    logits = nt(wh, xh) + nt(wh, xl) + nt(wl, xh)
    scores = 1.0 / (1.0 + jnp.exp(-logits))
    biased = scores + b_ref[...]

    rows_g = lax.broadcasted_iota(I32, (GROUP_SIZE, tn), 0).astype(F32)
    rows_e = lax.broadcasted_iota(I32, (N_EXPERTS, tn), 0).astype(F32)

    gscore = []
    for g in range(N_GROUPS):
        slab = biased[g * GROUP_SIZE:(g + 1) * GROUP_SIZE]
        m1, i1 = _first_max(slab, rows_g, GROUP_SIZE)
        m2 = jnp.max(jnp.where(rows_g == i1, NEG_INF, slab), axis=0, keepdims=True)
        gscore.append(m1 + m2)
    cur = jnp.concatenate(gscore, axis=0)
    rows_grp = lax.broadcasted_iota(I32, (N_GROUPS, tn), 0).astype(F32)
    gsel = jnp.zeros((N_GROUPS, tn), F32)
    for _ in range(TOPK_GROUPS):
        _, gi = _first_max(cur, rows_grp, N_GROUPS)
        pick = rows_grp == gi
        gsel = jnp.where(pick, 1.0, gsel)
        cur = jnp.where(pick, NEG_INF, cur)

    masked = jnp.concatenate(
        [jnp.where(gsel[g:g + 1] > 0.0, biased[g * GROUP_SIZE:(g + 1) * GROUP_SIZE], NEG_INF)
         for g in range(N_GROUPS)], axis=0)

    sel = jnp.zeros((N_EXPERTS, tn), F32)
    picks = []
    cur = masked
    for _ in range(TOP_K):
        _, ei = _first_max(cur, rows_e, N_EXPERTS)
        pick = rows_e == ei
        sel = jnp.where(pick, 1.0, sel)
        cur = jnp.where(pick, NEG_INF, cur)
        picks.append(ei)

    gsum = jnp.sum(scores * sel, axis=0, keepdims=True)

    ta = lax.broadcasted_iota(I32, (tn, tn), 0)
    tb = lax.broadcasted_iota(I32, (tn, tn), 1)
    before = jnp.where(ta < tb, 1.0, 0.0).astype(BF16)
    rank = jnp.dot(sel.astype(BF16), before, preferred_element_type=F32) + carry_ref[...]
    carry_ref[...] = carry_ref[...] + jnp.sum(sel, axis=1, keepdims=True)
    cnt_ref[...] = carry_ref[...]

    gates = []
    ranks = []
    for ei in picks:
        hit = rows_e == ei
        gates.append(jnp.sum(jnp.where(hit, scores, 0.0), axis=0, keepdims=True) / gsum * ROUTED_SCALE)
        ranks.append(jnp.sum(jnp.where(hit, rank, 0.0), axis=0, keepdims=True))
    idx_ref[...] = jnp.concatenate(picks, axis=0).astype(I32)
    gate_ref[...] = jnp.concatenate(gates, axis=0)
    rank_ref[...] = jnp.concatenate(ranks, axis=0).astype(I32)


def _router(x, w_router, b_router):
    t, d = x.shape
    tn = _pick_tile(t, (512, 256))
    kt = pl.BlockSpec((TOP_K, tn), lambda i: (0, i))
    return pl.pallas_call(
        _router_kernel,
        grid=(t // tn,),
        in_specs=[pl.BlockSpec((tn, d), lambda i: (i, 0)),
                  pl.BlockSpec((N_EXPERTS, d), lambda i: (0, 0)),
                  pl.BlockSpec((N_EXPERTS, 1), lambda i: (0, 0))],
        out_specs=[kt, kt, kt, pl.BlockSpec((N_EXPERTS, 1), lambda i: (0, 0))],
        out_shape=[jax.ShapeDtypeStruct((TOP_K, t), I32), jax.ShapeDtypeStruct((TOP_K, t), F32),
                   jax.ShapeDtypeStruct((TOP_K, t), I32), jax.ShapeDtypeStruct((N_EXPERTS, 1), F32)],
        scratch_shapes=[pltpu.VMEM((N_EXPERTS, 1), F32)],
        compiler_params=_cparams(("arbitrary",)),
        name="router",
    )(x, w_router.T, b_router.reshape(N_EXPERTS, 1))


def _slot_kernel(idx_ref, rank_ref, start_ref, slot_ref):
    tn = idx_ref.shape[1]
    rows_e = lax.broadcasted_iota(I32, (N_EXPERTS, tn), 0)
    out = []
    for k in range(TOP_K):
        hit = rows_e == idx_ref[k:k + 1, :]
        base = jnp.sum(jnp.where(hit, start_ref[...], 0.0), axis=0, keepdims=True)
        out.append(base.astype(I32) + rank_ref[k:k + 1, :])
    slot_ref[...] = jnp.concatenate(out, axis=0)


def _slots(idx_t, rank_t, pad_start):
    t = idx_t.shape[1]
    tn = _pick_tile(t, (512, 256))
    kt = pl.BlockSpec((TOP_K, tn), lambda i: (0, i))
    return pl.pallas_call(
        _slot_kernel,
        grid=(t // tn,),
        in_specs=[kt, kt, pl.BlockSpec((N_EXPERTS, 1), lambda i: (0, 0))],
        out_specs=kt,
        out_shape=jax.ShapeDtypeStruct((TOP_K, t), I32),
        compiler_params=_cparams(("arbitrary",)),
        name="slots",
    )(idx_t, rank_t, pad_start.astype(F32).reshape(N_EXPERTS, 1))


def _row_copy(src, src_row, dst, dst_row, sem):
    return pltpu.make_async_copy(src.at[pl.ds(src_row, 1)], dst.at[pl.ds(dst_row, 1)], sem)


def _dispatch_kernel(pad_end_ref, padded_ref, slot_hbm, x_ref, xs_hbm, slot_smem, zero_ref, sem, slot_sem):
    i = pl.program_id(0)
    tt = x_ref.shape[0]
    bm = zero_ref.shape[0]
    slot_cp = pltpu.make_async_copy(slot_hbm.at[i], slot_smem, slot_sem)
    slot_cp.start()

    @pl.when(i == 0)
    def _clear():
        zero_ref[...] = jnp.zeros_like(zero_ref)

        def pad_copy(e):
            off = pl.multiple_of(pad_end_ref[e] - bm, bm)
            return pltpu.make_async_copy(zero_ref, xs_hbm.at[pl.ds(off, bm)], sem)

        def start(e, carry):
            @pl.when(padded_ref[e] > 0)
            def _():
                pad_copy(e).start()
            return carry

        def wait(e, carry):
            @pl.when(padded_ref[e] > 0)
            def _():
                pad_copy(e).wait()
            return carry

        lax.fori_loop(0, N_EXPERTS, start, 0)
        lax.fori_loop(0, N_EXPERTS, wait, 0)

    slot_cp.wait()
    per_row = LANES // TOP_K

    def issue(r, carry):
        for j in range(LANES):
            _row_copy(x_ref, r * per_row + j // TOP_K, xs_hbm, slot_smem[r, j], sem).start(priority=j % 2)
        return carry

    lax.fori_loop(0, tt // per_row, issue, 0)

    def drain(t, carry):
        for k in range(TOP_K):
            _row_copy(x_ref, t, xs_hbm, 0, sem).wait()
        return carry

    lax.fori_loop(0, tt, drain, 0)


def _dispatch(x, slot_tiles, pad_end, padded, n_slots):
    t, d = x.shape
    tt = DISPATCH_TILE
    grid_spec = pltpu.PrefetchScalarGridSpec(
        num_scalar_prefetch=2,
        grid=(t // tt,),
        in_specs=[pl.BlockSpec(memory_space=pl.ANY),
                  pl.BlockSpec((tt, d), lambda i, pe, pd: (i, 0))],
        out_specs=pl.BlockSpec(memory_space=pl.ANY),
        scratch_shapes=[pltpu.SMEM((tt * TOP_K // LANES, LANES), I32),
                        pltpu.VMEM((EXPERT_ROWS, d), U32),
                        pltpu.SemaphoreType.DMA, pltpu.SemaphoreType.DMA],
    )
    return pl.pallas_call(
        _dispatch_kernel,
        grid_spec=grid_spec,
        out_shape=jax.ShapeDtypeStruct((n_slots, d), U32),
        compiler_params=_cparams(("arbitrary",)),
        name="dispatch",
    )(pad_end, padded, slot_tiles, x)


def _ffn_kernel(block_e_ref, n_used_ref, x_ref, wgu_ref, wd_ref, o_ref, wgu_bf, wd_bf, *, pack_out):
    b = pl.program_id(0)

    @pl.when(b < n_used_ref[0])
    def _():
        e = block_e_ref[b]
        e_prev = block_e_ref[jnp.maximum(b - 1, 0)]

        @pl.when((b == 0) | (e != e_prev))
        def _():
            wgu_bf[...] = wgu_ref[0, 0].astype(BF16)
            wd_bf[...] = wd_ref[0, 0].astype(BF16)

        hidden = wd_bf.shape[0]
        half = x_ref.shape[1]
        lo, hi = _unpack_rows(x_ref[...])
        gu = (jnp.dot(lo.astype(BF16), wgu_bf[:half], preferred_element_type=F32)
              + jnp.dot(hi.astype(BF16), wgu_bf[half:], preferred_element_type=F32))
        gate = gu[:, :hidden]
        act = gate * (1.0 / (1.0 + jnp.exp(-gate))) * gu[:, hidden:]
        y = jnp.dot(act.astype(BF16), wd_bf[...], preferred_element_type=F32)
        o_ref[...] = _pack_rows(y) if pack_out else y


def _ffn(xp, w_gu, w_down, layer, block_e, n_used, rows, pack_out, name):
    n, half = xp.shape
    d = 2 * half
    hidden = w_down.shape[2]

    def row_map(b, be, nu):
        return (jnp.minimum(b, nu[0] - 1), 0)

    out_w, out_dt = (half, U32) if pack_out else (d, F32)
    grid_spec = pltpu.PrefetchScalarGridSpec(
        num_scalar_prefetch=2,
        grid=(n // rows,),
        in_specs=[pl.BlockSpec((rows, half), row_map),
                  pl.BlockSpec((1, 1, d, 2 * hidden), lambda b, be, nu: (layer, be[b], 0, 0)),
                  pl.BlockSpec((1, 1, hidden, d), lambda b, be, nu: (layer, be[b], 0, 0))],
        out_specs=pl.BlockSpec((rows, out_w), row_map),
        scratch_shapes=[pltpu.VMEM((d, 2 * hidden), BF16), pltpu.VMEM((hidden, d), BF16)],
    )
    return pl.pallas_call(
        functools.partial(_ffn_kernel, pack_out=pack_out),
        grid_spec=grid_spec,
        out_shape=jax.ShapeDtypeStruct((n, out_w), out_dt),
        compiler_params=_cparams(("arbitrary",)),
        name=name,
    )(block_e, n_used, xp, w_gu, w_down)


def _combine_kernel(slot_hbm, x_ref, sh_ref, gate_ref, g_ref, b_ref, ys_hbm, o0_ref, o1_ref,
                    slot_smem, buf_ref, sems, slot_sem, *, split_at):
    i = pl.program_id(0)
    n = pl.num_programs(0)
    tt = x_ref.shape[0]
    per_row = LANES // TOP_K

    def fetch(tile):
        par = tile % 2
        slot_cp = pltpu.make_async_copy(slot_hbm.at[tile], slot_smem.at[par], slot_sem)
        slot_cp.start()
        slot_cp.wait()

        def issue(r, carry):
            for j in range(LANES):
                _row_copy(ys_hbm, slot_smem[par, r, j], buf_ref.at[par, j % TOP_K], r * per_row + j // TOP_K,
                          sems.at[par]).start(priority=j % 2)
            return carry

        lax.fori_loop(0, tt // per_row, issue, 0)

    @pl.when(i == 0)
    def _():
        fetch(i)

    @pl.when(i + 1 < n)
    def _():
        fetch(i + 1)

    par = i % 2

    def drain(t, carry):
        for k in range(TOP_K):
            _row_copy(ys_hbm, 0, buf_ref.at[par, k], t, sems.at[par]).wait()
        return carry

    lax.fori_loop(0, tt, drain, 0)

    half = buf_ref.shape[-1]
    sh = sh_ref[...]
    h_lo = sh[:, :half]
    h_hi = sh[:, half:]
    gate = gate_ref[...]
    for k in range(TOP_K):
        lo, hi = _unpack_rows(buf_ref[par, k])
        h_lo = h_lo + lo * gate[:, k:k + 1]
        h_hi = h_hi + hi * gate[:, k:k + 1]
    h = jnp.concatenate([h_lo, h_hi], axis=1)
    y = _layer_norm_rows(ALPHA * x_ref[...] + h, g_ref[...], b_ref[...])
    if split_at is None:
        o0_ref[...] = y
        o1_ref[...] = y.astype(BF16)
    else:
        @pl.when(i < split_at)
        def _():
            o0_ref[...] = y

        @pl.when(i >= split_at)
        def _():
            o1_ref[...] = y


def _combine(x, shared, ys, slot_tiles, gate, g, b, split_rows):
    t, d = x.shape
    tt = COMBINE_TILE
    row = pl.BlockSpec((tt, d), lambda i: (i, 0))
    vec = pl.BlockSpec((1, d), lambda i: (0, 0))
    if split_rows is None:
        split_at = None
        out_specs = [row, row]
        out_shape = [jax.ShapeDtypeStruct((t, d), F32), jax.ShapeDtypeStruct((t, d), BF16)]
    else:
        split_at = split_rows // tt
        out_specs = [pl.BlockSpec((tt, d), lambda i: (jnp.minimum(i, split_at - 1), 0)),
                     pl.BlockSpec((tt, d), lambda i: (jnp.maximum(i - split_at, 0), 0))]
        out_shape = [jax.ShapeDtypeStruct((split_rows, d), F32), jax.ShapeDtypeStruct((t - split_rows, d), F32)]
    return pl.pallas_call(
        functools.partial(_combine_kernel, split_at=split_at),
        grid=(t // tt,),
        in_specs=[pl.BlockSpec(memory_space=pl.ANY), row, row,
                  pl.BlockSpec((tt, TOP_K), lambda i: (i, 0)), vec, vec,
                  pl.BlockSpec(memory_space=pl.ANY)],
        out_specs=out_specs,
        out_shape=out_shape,
        scratch_shapes=[pltpu.SMEM((2, tt * TOP_K // LANES, LANES), I32),
                        pltpu.VMEM((2, TOP_K, tt, d // 2), U32),
                        pltpu.SemaphoreType.DMA((2,)), pltpu.SemaphoreType.DMA],
        compiler_params=_cparams(("arbitrary",)),
        name="combine",
    )(slot_tiles, x, shared, gate, g.reshape(1, d), b.reshape(1, d), ys)


def _moe_layer(x, xp, layer, w_router, b_router, w_gu, w_down, sw_gu, sw_down, ln_g, ln_b, split_rows):
    t, d = x.shape
    bm = EXPERT_ROWS
    idx_t, gate_t, rank_t, counts = _router(x, w_router, b_router)

    counts = counts.reshape(N_EXPERTS).astype(I32)
    padded = (counts + bm - 1) // bm * bm
    pad_end = jnp.cumsum(padded)
    pad_start = pad_end - padded
    n_blocks = (t * TOP_K + N_EXPERTS * (bm - 1) + bm - 1) // bm
    n_slots = n_blocks * bm
    block_first = jnp.arange(n_blocks, dtype=I32) * bm
    block_e = jnp.minimum(jnp.sum((pad_end[None, :] <= block_first[:, None]).astype(I32), axis=1), N_EXPERTS - 1)
    n_used = (pad_end[-1:] // bm).astype(I32)

    slot_t = _slots(idx_t, rank_t, pad_start)
    slot_flat = slot_t.T.reshape(t * TOP_K // LANES, LANES)

    def slot_tiles(tile):
        return slot_flat.reshape(t // tile, tile * TOP_K // LANES, LANES)

    xs = _dispatch(xp, slot_tiles(DISPATCH_TILE), pad_end.astype(I32), padded.astype(I32), n_slots)
    ys = _ffn(xs, w_gu, w_down, layer, block_e, n_used, bm, True, "expert_ffn")
    shared_rows = _pick_tile(t, (512, 256))
    shared = _ffn(xp, sw_gu[:, None], sw_down[:, None], layer, jnp.zeros((t // shared_rows,), I32),
                  jnp.full((1,), t // shared_rows, I32), shared_rows, False, "shared_ffn")
    return _combine(x, shared, ys, slot_tiles(COMBINE_TILE), gate_t.T, ln_g, ln_b, split_rows)


def _rope_angles(s, theta, half):
    inv = theta ** (-jnp.arange(half, dtype=F32) / half)
    ang = jnp.arange(s, dtype=F32)[:, None] * inv[None, :]
    return jnp.cos(ang), jnp.sin(ang)


def _ret_tables(s):
    cos, sin = _rope_angles(s, RET_THETA, RET_DK // 2)
    return cos, sin, sin


def _diff_tables(s):
    half = ROPE_DIMS // 2
    cos, sin = _rope_angles(s, ROPE_THETA, half)
    rest = DIFF_DH - ROPE_DIMS
    zeros_h = jnp.zeros((s, half), F32)
    c = jnp.concatenate([cos, cos, jnp.ones((s, rest), F32)], axis=1)
    s_up = jnp.concatenate([zeros_h, sin, jnp.zeros((s, rest), F32)], axis=1)
    s_dn = jnp.concatenate([-sin, zeros_h, jnp.zeros((s, rest), F32)], axis=1)
    return c, s_up, s_dn


def kernel(x_prompt, x_sample, ret_w_in, ret_decay_f, ret_decay_b, ret_w_out, diff_w_in, diff_lam_q1, diff_lam_k1,
           diff_lam_q2, diff_lam_k2, diff_subln_g, diff_w_out, ln_mix_g, ln_mix_b, router_w, router_b, exp_w_gu,
           exp_w_down, shared_w_gu, shared_w_down, ln_ffn_g, ln_ffn_b):
    bp, sp, d = x_prompt.shape
    bs, ss, _ = x_sample.shape
    tp = bp * sp
    x = jnp.concatenate([x_prompt.reshape(tp, d), x_sample.reshape(bs * ss, d)], axis=0)
    xb = x.astype(BF16)
    segs = [(0, sp), (tp, ss)]
    s_max = max(sp, ss)

    for i in range(DEPTH):
        j = i // 2
        if i % 2 == 0:
            qk = RET_HEADS * RET_DK
            proj = _proj(xb, ret_w_in[j].astype(BF16), _ret_tables(s_max), segs, mode="ret",
                         n_q_cols=qk, n_qk_cols=2 * qk, q_scale=1.0, k_scale=RET_DK ** -0.5)
            lg_f = -jax.nn.softplus(-ret_decay_f[j].astype(F32))
            lg_b = -jax.nn.softplus(-ret_decay_b[j].astype(F32))
            mixed = _retention(proj, lg_f, lg_b, segs)
            x, xp = _out_ln(mixed, ret_w_out[j].astype(BF16), x, ln_mix_g[i], ln_mix_b[i], "ret_out")
        else:
            lambda_init = 0.8 - 0.6 * math.exp(-0.3 * i)
            nq = DIFF_HEADS * 2 * DIFF_DH
            proj = _proj(xb, diff_w_in[j].astype(BF16), _diff_tables(s_max), segs, mode="diff",
                         n_q_cols=nq, n_qk_cols=2 * nq, q_scale=DIFF_DH ** -0.5, k_scale=1.0)
            lam = (jnp.exp(jnp.sum(diff_lam_q1[j].astype(F32) * diff_lam_k1[j].astype(F32)))
                   - jnp.exp(jnp.sum(diff_lam_q2[j].astype(F32) * diff_lam_k2[j].astype(F32))) + lambda_init)
            mixed = _diff_attention(proj, lam.reshape(1), diff_subln_g[j],
                                    [(0, sp, bp), (tp, ss, bs)], lambda_init)
            x, xp = _out_ln(mixed, diff_w_out[j].astype(BF16), x, ln_mix_g[i], ln_mix_b[i], "diff_out")
        last = i == DEPTH - 1
        x, xb = _moe_layer(x, xp, i, router_w[i], router_b[i], exp_w_gu, exp_w_down, shared_w_gu, shared_w_down,
                           ln_ffn_g[i], ln_ffn_b[i], tp if last else None)
    return x.reshape(bp, sp, d), xb.reshape(bs, ss, d)
```

```python
import functools
import math

import jax
import jax.numpy as jnp
from jax import lax
from jax.experimental import pallas as pl
from jax.experimental.pallas import tpu as pltpu

F32 = jnp.float32
BF16 = jnp.bfloat16
I32 = jnp.int32
U32 = jnp.uint32

D_MODEL = 2048
DEPTH = 2
ALPHA = (2 * DEPTH) ** 0.25
LN_EPS = 1e-5
RET_HEADS = 8
RET_DK = D_MODEL // RET_HEADS
RET_DV = 2 * RET_DK
RET_THETA = 10000.0
DIFF_HEADS = 8
DIFF_DH = D_MODEL // (2 * DIFF_HEADS)
DIFF_DV = 2 * DIFF_DH
ROPE_THETA = 500000.0
ROPE_DIMS = DIFF_DH // 4
N_EXPERTS = 64
TOP_K = 8
N_GROUPS = 8
TOPK_GROUPS = 4
GROUP_SIZE = N_EXPERTS // N_GROUPS
EXPERT_HIDDEN = 512
SHARED_HIDDEN = 512
ROUTED_SCALE = 2.5

LANES = 128
SUBLANES = 8
VMEM_LIMIT = 48 * 1024 * 1024

RET_CHUNK = 256
RET_GROUP = 4
EXPERT_ROWS = 256
DISPATCH_TILE = 256
COMBINE_TILE = 128
ATTN_Q_TILE = 512
NEG_INF = float("-inf")
LOG2E = math.log2(math.e)


def _cparams(sem):
    return pltpu.CompilerParams(dimension_semantics=sem, vmem_limit_bytes=VMEM_LIMIT)


def _pick_tile(n, prefs):
    for p in prefs:
        if n % p == 0:
            return p
    raise ValueError(f"no tile in {prefs} divides {n}")


def _seg_local_block(i, segs_blocks):
    val = None
    for first, per in segs_blocks:
        loc = lax.rem(i - first, per)
        val = loc if val is None else jnp.where(i >= first, loc, val)
    return val


def _proj_kernel(x_ref, w_ref, t0_ref, t1_ref, t2_ref, o_ref, *, mode, n_q_tiles, n_qk_tiles, q_scale, k_scale):
    j = pl.program_id(1)
    acc = jnp.dot(x_ref[...], w_ref[...], preferred_element_type=F32)
    tn = acc.shape[1]

    @pl.when(j >= n_qk_tiles)
    def _plain():
        o_ref[...] = acc.astype(o_ref.dtype)

    @pl.when(j < n_qk_tiles)
    def _rope():
        scale = jnp.where(j < n_q_tiles, q_scale, k_scale)
        if mode == "ret":
            cos = t0_ref[...]
            sin = t1_ref[...]
            half = RET_DK // 2
            for hs in range(tn // RET_DK):
                lo = hs * RET_DK
                x1 = acc[:, lo:lo + half]
                x2 = acc[:, lo + half:lo + RET_DK]
                o_ref[:, lo:lo + half] = ((x1 * cos - x2 * sin) * scale).astype(o_ref.dtype)
                o_ref[:, lo + half:lo + RET_DK] = ((x2 * cos + x1 * sin) * scale).astype(o_ref.dtype)
        else:
            c = t0_ref[...]
            s_up = t1_ref[...]
            s_dn = t2_ref[...]
            half = ROPE_DIMS // 2
            for g in range(tn // DIFF_DH):
                lo = g * DIFF_DH
                seg = acc[:, lo:lo + DIFF_DH]
                rot = seg * c + pltpu.roll(seg, half, 1) * s_up + pltpu.roll(seg, DIFF_DH - half, 1) * s_dn
                o_ref[:, lo:lo + DIFF_DH] = (rot * scale).astype(o_ref.dtype)


def _proj(x, w, tables, segs, *, mode, n_q_cols, n_qk_cols, q_scale, k_scale):
    m, k = x.shape
    n = w.shape[1]
    seg_len = [s for _, s in segs]
    tm = _pick_tile(math.gcd(*seg_len), (1024, 512, 256))
    tn = 1024
    segs_blocks = [(start // tm, s // tm) for start, s in segs]
    tw = tables[0].shape[1]
    tab_spec = pl.BlockSpec((tm, tw), lambda i, j: (_seg_local_block(i, segs_blocks), 0))
    kern = functools.partial(_proj_kernel, mode=mode, n_q_tiles=n_q_cols // tn, n_qk_tiles=n_qk_cols // tn,
                             q_scale=q_scale, k_scale=k_scale)
    return pl.pallas_call(
        kern,
        grid=(m // tm, n // tn),
        in_specs=[pl.BlockSpec((tm, k), lambda i, j: (i, 0)),
                  pl.BlockSpec((k, tn), lambda i, j: (0, j)),
                  tab_spec, tab_spec, tab_spec],
        out_specs=pl.BlockSpec((tm, tn), lambda i, j: (i, j)),
        out_shape=jax.ShapeDtypeStruct((m, n), BF16),
        compiler_params=_cparams(("arbitrary", "arbitrary")),
        name=f"proj_{mode}",
    )(x, w, *tables)


HIGH_HALF = 0xFFFF0000


def _pack_rows(y):
    n = y.shape[1] // 2
    lo = lax.bitcast_convert_type(y[:, :n].astype(BF16).astype(F32), U32) >> 16
    hi = lax.bitcast_convert_type(y[:, n:].astype(BF16).astype(F32), U32) & jnp.uint32(HIGH_HALF)
    return lo | hi


def _unpack_rows(w):
    lo = lax.bitcast_convert_type(w << 16, F32)
    hi = lax.bitcast_convert_type(w & jnp.uint32(HIGH_HALF), F32)
    return lo, hi


PACK_WORDS = D_MODEL // 2
PACK_SUB = PACK_WORDS // LANES
assert PACK_SUB == SUBLANES


def _tile_of(row):
    start = row * PACK_SUB
    return pl.ds(start if isinstance(start, int) else pl.multiple_of(start, PACK_SUB), PACK_SUB)


def _store_packed(ref, words):
    r = words.shape[0]
    for s in range(PACK_SUB):
        ref[pl.ds(s, r, stride=PACK_SUB), :] = words[:, s * LANES:(s + 1) * LANES]


def _load_packed(ref, r):
    return jnp.concatenate([ref[pl.ds(s, r, stride=PACK_SUB), :] for s in range(PACK_SUB)], axis=1)


def _layer_norm_rows(z, g, b):
    mu = jnp.mean(z, axis=-1, keepdims=True)
    zc = z - mu
    var = jnp.mean(zc * zc, axis=-1, keepdims=True)
    return zc * lax.rsqrt(var + LN_EPS) * g + b


def _out_ln_kernel(a_ref, w_ref, x_ref, g_ref, b_ref, o_ref, op_ref):
    h = jnp.dot(a_ref[...], w_ref[...], preferred_element_type=F32)
    y = _layer_norm_rows(ALPHA * x_ref[...] + h, g_ref[...], b_ref[...])
    o_ref[...] = y
    _store_packed(op_ref, _pack_rows(y))


def _out_ln(a, w, x, g, b, name):
    m, kk = a.shape
    d = w.shape[1]
    tm = 256
    row = pl.BlockSpec((tm, d), lambda i: (i, 0))
    vec = pl.BlockSpec((1, d), lambda i: (0, 0))
    return pl.pallas_call(
        _out_ln_kernel,
        grid=(m // tm,),
        in_specs=[pl.BlockSpec((tm, kk), lambda i: (i, 0)),
                  pl.BlockSpec((kk, d), lambda i: (0, 0), pipeline_mode=pl.Buffered(1)), row, vec, vec],
        out_specs=[row, pl.BlockSpec((tm * PACK_SUB, LANES), lambda i: (i, 0))],
        out_shape=[jax.ShapeDtypeStruct((m, d), F32), jax.ShapeDtypeStruct((m * PACK_SUB, LANES), U32)],
        compiler_params=_cparams(("arbitrary",)),
        name=name,
    )(a, w, x, g.reshape(1, d), b.reshape(1, d))


def _ret_kernel(lg_ref, cd_ref, q_ref, k_ref, v_ref, *rest, backward, segs_chunks):
    if backward:
        of_ref, g_ref, o_ref, state_ref, dmat_ref, qd_ref, kd_ref = rest
    else:
        o_ref, state_ref, dmat_ref, qd_ref, kd_ref = rest
    h = pl.program_id(0)
    c = pl.program_id(1)
    nc = pl.num_programs(1)
    cc = (nc - 1 - c) if backward else c
    lg = lg_ref[h]
    cs = RET_CHUNK

    @pl.when(c == 0)
    def _tables():
        ii = lax.broadcasted_iota(I32, (cs, cs), 0)
        jj = lax.broadcasted_iota(I32, (cs, cs), 1)
        if backward:
            mask = jj > ii
            dist = (jj - ii).astype(F32)
        else:
            mask = ii >= jj
            dist = (ii - jj).astype(F32)
        dmat_ref[...] = jnp.where(mask, jnp.exp(jnp.where(mask, dist, 0.0) * lg), 0.0)
        pos = lax.broadcasted_iota(I32, (cs, 1), 0).astype(F32)
        if backward:
            qd_ref[...] = jnp.exp((cs - pos) * lg)
            kd_ref[...] = jnp.exp(pos * lg)
        else:
            qd_ref[...] = jnp.exp((pos + 1.0) * lg)
            kd_ref[...] = jnp.exp((cs - 1.0 - pos) * lg)

    loc = _seg_local_block(cc, segs_chunks)
    if backward:
        per = None
        for first, p in segs_chunks:
            per = p if per is None else jnp.where(cc >= first, p, per)
        is_start = loc == per - 1
    else:
        is_start = loc == 0

    @pl.when(is_start)
    def _reset():
        state_ref[...] = jnp.zeros_like(state_ref)

    state = state_ref[...]
    n_sub = q_ref.shape[0] // cs
    for sub in (range(n_sub - 1, -1, -1) if backward else range(n_sub)):
        rows = pl.ds(sub * cs, cs)
        q = q_ref[rows, :]
        k = k_ref[rows, :]
        v = v_ref[rows, :]
        att = lax.dot_general(q, k, (((1,), (1,)), ((), ())), preferred_element_type=F32) * dmat_ref[...]
        inner = jnp.dot(att.astype(BF16), v, preferred_element_type=F32)
        cross = jnp.dot(q, state.astype(BF16), preferred_element_type=F32) * qd_ref[...]
        o = inner + cross
        kt = (k.astype(F32) * kd_ref[...]).T.astype(BF16)
        state = state * cd_ref[h] + jnp.dot(kt, v, preferred_element_type=F32)

        if backward:
            tot = of_ref[rows, :] + o
            mu = jnp.mean(tot, axis=-1, keepdims=True)
            tc = tot - mu
            var = jnp.mean(tc * tc, axis=-1, keepdims=True)
            on = tc * lax.rsqrt(var + LN_EPS)
            g = g_ref[rows, :].astype(F32)
            o_ref[rows, :] = ((g * (1.0 / (1.0 + jnp.exp(-g)))) * on).astype(o_ref.dtype)
        else:
            o_ref[rows, :] = o
    state_ref[...] = state


def _retention(proj, lg_f, lg_b, segs):
    t = proj.shape[0]
    cs = RET_CHUNK
    rb = RET_CHUNK * RET_GROUP
    nc = t // rb
    segs_chunks = [(start // rb, s // rb) for start, s in segs]
    qk_blocks = (RET_HEADS * RET_DK) // RET_DK
    v_blocks0 = (2 * RET_HEADS * RET_DK) // RET_DV
    g_blocks0 = v_blocks0 + RET_HEADS
    smem = pl.BlockSpec(memory_space=pltpu.SMEM)
    scratch = [pltpu.VMEM((RET_DK, RET_DV), F32), pltpu.VMEM((cs, cs), F32),
               pltpu.VMEM((cs, 1), F32), pltpu.VMEM((cs, 1), F32)]

    def run(backward, lg, extra):
        def cmap(c):
            return (nc - 1 - c) if backward else c
        in_specs = [smem, smem,
                    pl.BlockSpec((rb, RET_DK), lambda h, c: (cmap(c), h)),
                    pl.BlockSpec((rb, RET_DK), lambda h, c: (cmap(c), qk_blocks + h)),
                    pl.BlockSpec((rb, RET_DV), lambda h, c: (cmap(c), v_blocks0 + h))]
        args = [lg, jnp.exp(cs * lg), proj, proj, proj]
        if backward:
            in_specs += [pl.BlockSpec((rb, RET_DV), lambda h, c: (cmap(c), h)),
                         pl.BlockSpec((rb, RET_DV), lambda h, c: (cmap(c), g_blocks0 + h))]
            args += [extra, proj]
        return pl.pallas_call(
            functools.partial(_ret_kernel, backward=backward, segs_chunks=segs_chunks),
            grid=(RET_HEADS, nc),
            in_specs=in_specs,
            out_specs=pl.BlockSpec((rb, RET_DV), lambda h, c: (cmap(c), h)),
            out_shape=jax.ShapeDtypeStruct((t, RET_HEADS * RET_DV), BF16 if backward else F32),
            scratch_shapes=scratch,
            compiler_params=_cparams(("arbitrary", "arbitrary")),
            name="retention_bwd" if backward else "retention_fwd",
        )(*args)

    o_f = run(False, lg_f, None)
    return run(True, lg_b, o_f)


def _attn_kernel(lam_ref, q_ref, k_ref, v_ref, sg_ref, o_ref, *, out_scale):
    v = v_ref[...]
    outs = []
    for c in range(2):
        q = q_ref[:, c * DIFF_DH:(c + 1) * DIFF_DH]
        k = k_ref[:, c * DIFF_DH:(c + 1) * DIFF_DH]
        s = lax.dot_general(q, k, (((1,), (1,)), ((), ())), preferred_element_type=F32)
        p = jnp.exp2(s - jnp.max(s, axis=-1, keepdims=True))
        den = jnp.sum(p, axis=-1, keepdims=True)
        outs.append(jnp.dot(p.astype(BF16), v, preferred_element_type=F32) / den)
    o = outs[0] - lam_ref[0] * outs[1]
    ms = jnp.mean(o * o, axis=-1, keepdims=True)
    o_ref[...] = (o * lax.rsqrt(ms + LN_EPS) * sg_ref[...] * out_scale).astype(o_ref.dtype)


def _diff_attention(proj, lam, subln_g, segs, lambda_init):
    nq_blocks = DIFF_HEADS
    outs = []
    for start, s, n_seq in segs:
        tq = ATTN_Q_TILE
        assert s % tq == 0 and start % s == 0
        qb0 = start // tq
        kb0 = start // s
        nqt = s // tq
        out = pl.pallas_call(
            functools.partial(_attn_kernel, out_scale=1.0 - lambda_init),
            grid=(n_seq, DIFF_HEADS, nqt),
            in_specs=[pl.BlockSpec(memory_space=pltpu.SMEM),
                      pl.BlockSpec((tq, 2 * DIFF_DH), lambda b, h, i: (qb0 + b * nqt + i, h)),
                      pl.BlockSpec((s, 2 * DIFF_DH), lambda b, h, i: (kb0 + b, nq_blocks + h)),
                      pl.BlockSpec((s, DIFF_DV), lambda b, h, i: (kb0 + b, 2 * nq_blocks + h)),
                      pl.BlockSpec((1, DIFF_DV), lambda b, h, i: (0, 0))],
            out_specs=pl.BlockSpec((tq, DIFF_DV), lambda b, h, i: (b * nqt + i, h)),
            out_shape=jax.ShapeDtypeStruct((n_seq * s, DIFF_HEADS * DIFF_DV), BF16),
            compiler_params=_cparams(("arbitrary", "arbitrary", "arbitrary")),
            name=f"diff_attn_s{s}",
        )(lam, proj, proj, proj, subln_g.reshape(1, DIFF_DV))
        outs.append(out)
    return jnp.concatenate(outs, axis=0)


def _first_max(cur, rows, n_rows):
    m = jnp.max(cur, axis=0, keepdims=True)
    idx = jnp.min(jnp.where(cur == m, rows, float(n_rows)), axis=0, keepdims=True)
    return m, idx


def _router_kernel(x_ref, wt_ref, b_ref, idx_ref, gate_ref, rank_ref, cnt_ref, carry_ref):
    i = pl.program_id(0)

    @pl.when(i == 0)
    def _init():
        carry_ref[...] = jnp.zeros_like(carry_ref)

    tn = x_ref.shape[0]
    x = x_ref[...]
    xh = x.astype(BF16)
    xl = (x - xh.astype(F32)).astype(BF16)
    w = wt_ref[...]
    wh = w.astype(BF16)
    wl = (w - wh.astype(F32)).astype(BF16)

    def nt(a, b):
        return lax.dot_general(a, b, (((1,), (1,)), ((), ())), preferred_element_type=F32)

    logits = nt(wh, xh) + nt(wh, xl) + nt(wl, xh)
    scores = 1.0 / (1.0 + jnp.exp(-logits))
    biased = scores + b_ref[...]

    rows_g = lax.broadcasted_iota(I32, (GROUP_SIZE, tn), 0).astype(F32)
    rows_e = lax.broadcasted_iota(I32, (N_EXPERTS, tn), 0).astype(F32)

    gscore = []
    for g in range(N_GROUPS):
        slab = biased[g * GROUP_SIZE:(g + 1) * GROUP_SIZE]
        m1, i1 = _first_max(slab, rows_g, GROUP_SIZE)
        m2 = jnp.max(jnp.where(rows_g == i1, NEG_INF, slab), axis=0, keepdims=True)
        gscore.append(m1 + m2)
    cur = jnp.concatenate(gscore, axis=0)
    rows_grp = lax.broadcasted_iota(I32, (N_GROUPS, tn), 0).astype(F32)
    gsel = jnp.zeros((N_GROUPS, tn), F32)
    for _ in range(TOPK_GROUPS):
        _, gi = _first_max(cur, rows_grp, N_GROUPS)
        pick = rows_grp == gi
        gsel = jnp.where(pick, 1.0, gsel)
        cur = jnp.where(pick, NEG_INF, cur)

    masked = jnp.concatenate(
        [jnp.where(gsel[g:g + 1] > 0.0, biased[g * GROUP_SIZE:(g + 1) * GROUP_SIZE], NEG_INF)
         for g in range(N_GROUPS)], axis=0)

    sel = jnp.zeros((N_EXPERTS, tn), F32)
    picks = []
    cur = masked
    for _ in range(TOP_K):
        _, ei = _first_max(cur, rows_e, N_EXPERTS)
        pick = rows_e == ei
        sel = jnp.where(pick, 1.0, sel)
        cur = jnp.where(pick, NEG_INF, cur)
        picks.append(ei)

    gsum = jnp.sum(scores * sel, axis=0, keepdims=True)

    ta = lax.broadcasted_iota(I32, (tn, tn), 0)
    tb = lax.broadcasted_iota(I32, (tn, tn), 1)
    before = jnp.where(ta < tb, 1.0, 0.0).astype(BF16)
    rank = jnp.dot(sel.astype(BF16), before, preferred_element_type=F32) + carry_ref[...]
    carry_ref[...] = carry_ref[...] + jnp.sum(sel, axis=1, keepdims=True)
    cnt_ref[...] = carry_ref[...]

    gates = []
    ranks = []
    for ei in picks:
        hit = rows_e == ei
        gates.append(jnp.sum(jnp.where(hit, scores, 0.0), axis=0, keepdims=True) / gsum * ROUTED_SCALE)
        ranks.append(jnp.sum(jnp.where(hit, rank, 0.0), axis=0, keepdims=True))
    idx_ref[...] = jnp.concatenate(picks, axis=0).astype(I32)
    gate_ref[...] = jnp.concatenate(gates, axis=0)
    rank_ref[...] = jnp.concatenate(ranks, axis=0).astype(I32)


def _router(x, w_router, b_router):
    t, d = x.shape
    tn = _pick_tile(t, (512, 256))
    kt = pl.BlockSpec((TOP_K, tn), lambda i: (0, i))
    return pl.pallas_call(
        _router_kernel,
        grid=(t // tn,),
        in_specs=[pl.BlockSpec((tn, d), lambda i: (i, 0)),
                  pl.BlockSpec((N_EXPERTS, d), lambda i: (0, 0)),
                  pl.BlockSpec((N_EXPERTS, 1), lambda i: (0, 0))],
        out_specs=[kt, kt, kt, pl.BlockSpec((N_EXPERTS, 1), lambda i: (0, 0))],
        out_shape=[jax.ShapeDtypeStruct((TOP_K, t), I32), jax.ShapeDtypeStruct((TOP_K, t), F32),
                   jax.ShapeDtypeStruct((TOP_K, t), I32), jax.ShapeDtypeStruct((N_EXPERTS, 1), F32)],
        scratch_shapes=[pltpu.VMEM((N_EXPERTS, 1), F32)],
        compiler_params=_cparams(("arbitrary",)),
        name="router",
    )(x, w_router.T, b_router.reshape(N_EXPERTS, 1))


def _slot_kernel(idx_ref, rank_ref, start_ref, slot_ref):
    tn = idx_ref.shape[1]
    rows_e = lax.broadcasted_iota(I32, (N_EXPERTS, tn), 0)
    out = []
    for k in range(TOP_K):
        hit = rows_e == idx_ref[k:k + 1, :]
        base = jnp.sum(jnp.where(hit, start_ref[...], 0.0), axis=0, keepdims=True)
        out.append(base.astype(I32) + rank_ref[k:k + 1, :])
    slot_ref[...] = jnp.concatenate(out, axis=0)


def _slots(idx_t, rank_t, pad_start):
    t = idx_t.shape[1]
    tn = _pick_tile(t, (512, 256))
    kt = pl.BlockSpec((TOP_K, tn), lambda i: (0, i))
    return pl.pallas_call(
        _slot_kernel,
        grid=(t // tn,),
        in_specs=[kt, kt, pl.BlockSpec((N_EXPERTS, 1), lambda i: (0, 0))],
        out_specs=kt,
        out_shape=jax.ShapeDtypeStruct((TOP_K, t), I32),
        compiler_params=_cparams(("arbitrary",)),
        name="slots",
    )(idx_t, rank_t, pad_start.astype(F32).reshape(N_EXPERTS, 1))


def _row_copy(src, src_row, dst, dst_row, sem):
    return pltpu.make_async_copy(src.at[_tile_of(src_row)], dst.at[_tile_of(dst_row)], sem)


def _dispatch_kernel(pad_end_ref, padded_ref, slot_hbm, x_ref, xs_hbm, slot_smem, zero_ref, sem, slot_sem):
    i = pl.program_id(0)
    tt = x_ref.shape[0] // PACK_SUB
    bm = zero_ref.shape[0] // PACK_SUB
    slot_cp = pltpu.make_async_copy(slot_hbm.at[i], slot_smem, slot_sem)
    slot_cp.start()

    @pl.when(i == 0)
    def _clear():
        zero_ref[...] = jnp.zeros_like(zero_ref)

        def pad_copy(e):
            off = pl.multiple_of((pad_end_ref[e] - bm) * PACK_SUB, bm * PACK_SUB)
            return pltpu.make_async_copy(zero_ref, xs_hbm.at[pl.ds(off, bm * PACK_SUB)], sem)

        def start(e, carry):
            @pl.when(padded_ref[e] > 0)
            def _():
                pad_copy(e).start()
            return carry

        def wait(e, carry):
            @pl.when(padded_ref[e] > 0)
            def _():
                pad_copy(e).wait()
            return carry

        lax.fori_loop(0, N_EXPERTS, start, 0)
        lax.fori_loop(0, N_EXPERTS, wait, 0)

    slot_cp.wait()
    per_row = LANES // TOP_K

    def issue(r, carry):
        for j in range(LANES):
            _row_copy(x_ref, r * per_row + j // TOP_K, xs_hbm, slot_smem[r, j], sem).start(priority=j % 2)
        return carry

    lax.fori_loop(0, tt // per_row, issue, 0)

    def drain(t, carry):
        for k in range(TOP_K):
            _row_copy(x_ref, t, xs_hbm, 0, sem).wait()
        return carry

    lax.fori_loop(0, tt, drain, 0)


def _dispatch(xp, slot_tiles, pad_end, padded, n_slots):
    t = xp.shape[0] // PACK_SUB
    tt = DISPATCH_TILE
    grid_spec = pltpu.PrefetchScalarGridSpec(
        num_scalar_prefetch=2,
        grid=(t // tt,),
        in_specs=[pl.BlockSpec(memory_space=pl.ANY),
                  pl.BlockSpec((tt * PACK_SUB, LANES), lambda i, pe, pd: (i, 0))],
        out_specs=pl.BlockSpec(memory_space=pl.ANY),
        scratch_shapes=[pltpu.SMEM((tt * TOP_K // LANES, LANES), I32),
                        pltpu.VMEM((EXPERT_ROWS * PACK_SUB, LANES), U32),
                        pltpu.SemaphoreType.DMA, pltpu.SemaphoreType.DMA],
    )
    return pl.pallas_call(
        _dispatch_kernel,
        grid_spec=grid_spec,
        out_shape=jax.ShapeDtypeStruct((n_slots * PACK_SUB, LANES), U32),
        compiler_params=_cparams(("arbitrary",)),
        name="dispatch",
    )(pad_end, padded, slot_tiles, xp)


def _ffn_kernel(block_e_ref, n_used_ref, x_ref, wgu_ref, wd_ref, o_ref, wgu_bf, wd_bf, *, pack_out):
    b = pl.program_id(0)

    @pl.when(b < n_used_ref[0])
    def _():
        e = block_e_ref[b]
        e_prev = block_e_ref[jnp.maximum(b - 1, 0)]

        @pl.when((b == 0) | (e != e_prev))
        def _():
            wgu_bf[...] = wgu_ref[0, 0].astype(BF16)
            wd_bf[...] = wd_ref[0, 0].astype(BF16)

        hidden = wd_bf.shape[0]
        rows = x_ref.shape[0] // PACK_SUB
        lo, hi = _unpack_rows(_load_packed(x_ref, rows))
        gu = (jnp.dot(lo.astype(BF16), wgu_bf[:PACK_WORDS], preferred_element_type=F32)
              + jnp.dot(hi.astype(BF16), wgu_bf[PACK_WORDS:], preferred_element_type=F32))
        gate = gu[:, :hidden]
        act = gate * (1.0 / (1.0 + jnp.exp(-gate))) * gu[:, hidden:]
        y = jnp.dot(act.astype(BF16), wd_bf[...], preferred_element_type=F32)
        if pack_out:
            _store_packed(o_ref, _pack_rows(y))
        else:
            o_ref[...] = y


def _ffn(xp, w_gu, w_down, layer, block_e, n_used, rows, pack_out, name):
    n = xp.shape[0] // PACK_SUB
    d = 2 * PACK_WORDS
    hidden = w_down.shape[2]

    def row_map(b, be, nu):
        return (jnp.minimum(b, nu[0] - 1), 0)

    packed_rows = pl.BlockSpec((rows * PACK_SUB, LANES), row_map)
    if pack_out:
        out_spec, out_shape = packed_rows, jax.ShapeDtypeStruct((n * PACK_SUB, LANES), U32)
    else:
        out_spec, out_shape = pl.BlockSpec((rows, d), row_map), jax.ShapeDtypeStruct((n, d), F32)
    grid_spec = pltpu.PrefetchScalarGridSpec(
        num_scalar_prefetch=2,
        grid=(n // rows,),
        in_specs=[packed_rows,
                  pl.BlockSpec((1, 1, d, 2 * hidden), lambda b, be, nu: (layer, be[b], 0, 0)),
                  pl.BlockSpec((1, 1, hidden, d), lambda b, be, nu: (layer, be[b], 0, 0))],
        out_specs=out_spec,
        scratch_shapes=[pltpu.VMEM((d, 2 * hidden), BF16), pltpu.VMEM((hidden, d), BF16)],
    )
    return pl.pallas_call(
        functools.partial(_ffn_kernel, pack_out=pack_out),
        grid_spec=grid_spec,
        out_shape=out_shape,
        compiler_params=_cparams(("arbitrary",)),
        name=name,
    )(block_e, n_used, xp, w_gu, w_down)


def _combine_kernel(slot_hbm, x_ref, sh_ref, gate_ref, g_ref, b_ref, ys_hbm, o0_ref, o1_ref,
                    slot_smem, buf_ref, sems, slot_sem, *, split_at):
    i = pl.program_id(0)
    n = pl.num_programs(0)
    tt = x_ref.shape[0]
    per_row = LANES // TOP_K

    def fetch(tile):
        par = tile % 2
        slot_cp = pltpu.make_async_copy(slot_hbm.at[tile], slot_smem.at[par], slot_sem)
        slot_cp.start()
        slot_cp.wait()

        def issue(r, carry):
            for j in range(LANES):
                _row_copy(ys_hbm, slot_smem[par, r, j], buf_ref.at[par, j % TOP_K], r * per_row + j // TOP_K,
                          sems.at[par]).start(priority=j % 2)
            return carry

        lax.fori_loop(0, tt // per_row, issue, 0)

    @pl.when(i == 0)
    def _():
        fetch(i)

    @pl.when(i + 1 < n)
    def _():
        fetch(i + 1)

    par = i % 2

    def drain(t, carry):
        for k in range(TOP_K):
            _row_copy(ys_hbm, 0, buf_ref.at[par, k], t, sems.at[par]).wait()
        return carry

    lax.fori_loop(0, tt, drain, 0)

    sh = sh_ref[...]
    h_lo = sh[:, :PACK_WORDS]
    h_hi = sh[:, PACK_WORDS:]
    gate = gate_ref[...]
    for k in range(TOP_K):
        lo, hi = _unpack_rows(_load_packed(buf_ref.at[par, k], tt))
        h_lo = h_lo + lo * gate[:, k:k + 1]
        h_hi = h_hi + hi * gate[:, k:k + 1]
    h = jnp.concatenate([h_lo, h_hi], axis=1)
    y = _layer_norm_rows(ALPHA * x_ref[...] + h, g_ref[...], b_ref[...])
    if split_at is None:
        o0_ref[...] = y
        o1_ref[...] = y.astype(BF16)
    else:
        @pl.when(i < split_at)
        def _():
            o0_ref[...] = y

        @pl.when(i >= split_at)
        def _():
            o1_ref[...] = y


def _combine(x, shared, ys, slot_tiles, gate, g, b, split_rows):
    t, d = x.shape
    tt = COMBINE_TILE
    row = pl.BlockSpec((tt, d), lambda i: (i, 0))
    vec = pl.BlockSpec((1, d), lambda i: (0, 0))
    if split_rows is None:
        split_at = None
        out_specs = [row, row]
        out_shape = [jax.ShapeDtypeStruct((t, d), F32), jax.ShapeDtypeStruct((t, d), BF16)]
    else:
        split_at = split_rows // tt
        out_specs = [pl.BlockSpec((tt, d), lambda i: (jnp.minimum(i, split_at - 1), 0)),
                     pl.BlockSpec((tt, d), lambda i: (jnp.maximum(i - split_at, 0), 0))]
        out_shape = [jax.ShapeDtypeStruct((split_rows, d), F32), jax.ShapeDtypeStruct((t - split_rows, d), F32)]
    return pl.pallas_call(
        functools.partial(_combine_kernel, split_at=split_at),
        grid=(t // tt,),
        in_specs=[pl.BlockSpec(memory_space=pl.ANY), row, row,
                  pl.BlockSpec((tt, TOP_K), lambda i: (i, 0)), vec, vec,
                  pl.BlockSpec(memory_space=pl.ANY)],
        out_specs=out_specs,
        out_shape=out_shape,
        scratch_shapes=[pltpu.SMEM((2, tt * TOP_K // LANES, LANES), I32),
                        pltpu.VMEM((2, TOP_K, tt * PACK_SUB, LANES), U32),
                        pltpu.SemaphoreType.DMA((2,)), pltpu.SemaphoreType.DMA],
        compiler_params=_cparams(("arbitrary",)),
        name="combine",
    )(slot_tiles, x, shared, gate, g.reshape(1, d), b.reshape(1, d), ys)


def _moe_layer(x, xp, layer, w_router, b_router, w_gu, w_down, sw_gu, sw_down, ln_g, ln_b, split_rows):
    t, d = x.shape
    bm = EXPERT_ROWS
    idx_t, gate_t, rank_t, counts = _router(x, w_router, b_router)

    counts = counts.reshape(N_EXPERTS).astype(I32)
    padded = (counts + bm - 1) // bm * bm
    pad_end = jnp.cumsum(padded)
    pad_start = pad_end - padded
    n_blocks = (t * TOP_K + N_EXPERTS * (bm - 1) + bm - 1) // bm
    n_slots = n_blocks * bm
    block_first = jnp.arange(n_blocks, dtype=I32) * bm
    block_e = jnp.minimum(jnp.sum((pad_end[None, :] <= block_first[:, None]).astype(I32), axis=1), N_EXPERTS - 1)
    n_used = (pad_end[-1:] // bm).astype(I32)

    slot_t = _slots(idx_t, rank_t, pad_start)
    slot_flat = slot_t.T.reshape(t * TOP_K // LANES, LANES)

    def slot_tiles(tile):
        return slot_flat.reshape(t // tile, tile * TOP_K // LANES, LANES)

    xs = _dispatch(xp, slot_tiles(DISPATCH_TILE), pad_end.astype(I32), padded.astype(I32), n_slots)
    ys = _ffn(xs, w_gu, w_down, layer, block_e, n_used, bm, True, "expert_ffn")
    shared_rows = _pick_tile(t, (512, 256))
    shared = _ffn(xp, sw_gu[:, None], sw_down[:, None], layer, jnp.zeros((t // shared_rows,), I32),
                  jnp.full((1,), t // shared_rows, I32), shared_rows, False, "shared_ffn")
    return _combine(x, shared, ys, slot_tiles(COMBINE_TILE), gate_t.T, ln_g, ln_b, split_rows)


def _rope_angles(s, theta, half):
    inv = theta ** (-jnp.arange(half, dtype=F32) / half)
    ang = jnp.arange(s, dtype=F32)[:, None] * inv[None, :]
    return jnp.cos(ang), jnp.sin(ang)


def _ret_tables(s):
    cos, sin = _rope_angles(s, RET_THETA, RET_DK // 2)
    return cos, sin, sin


def _diff_tables(s):
    half = ROPE_DIMS // 2
    cos, sin = _rope_angles(s, ROPE_THETA, half)
    rest = DIFF_DH - ROPE_DIMS
    zeros_h = jnp.zeros((s, half), F32)
    c = jnp.concatenate([cos, cos, jnp.ones((s, rest), F32)], axis=1)
    s_up = jnp.concatenate([zeros_h, sin, jnp.zeros((s, rest), F32)], axis=1)
    s_dn = jnp.concatenate([-sin, zeros_h, jnp.zeros((s, rest), F32)], axis=1)
    return c, s_up, s_dn


def kernel(x_prompt, x_sample, ret_w_in, ret_decay_f, ret_decay_b, ret_w_out, diff_w_in, diff_lam_q1, diff_lam_k1,
           diff_lam_q2, diff_lam_k2, diff_subln_g, diff_w_out, ln_mix_g, ln_mix_b, router_w, router_b, exp_w_gu,
           exp_w_down, shared_w_gu, shared_w_down, ln_ffn_g, ln_ffn_b):
    bp, sp, d = x_prompt.shape
    bs, ss, _ = x_sample.shape
    tp = bp * sp
    x = jnp.concatenate([x_prompt.reshape(tp, d), x_sample.reshape(bs * ss, d)], axis=0)
    xb = x.astype(BF16)
    segs = [(0, sp), (tp, ss)]
    s_max = max(sp, ss)

    for i in range(DEPTH):
        j = i // 2
        if i % 2 == 0:
            qk = RET_HEADS * RET_DK
            proj = _proj(xb, ret_w_in[j].astype(BF16), _ret_tables(s_max), segs, mode="ret",
                         n_q_cols=qk, n_qk_cols=2 * qk, q_scale=1.0, k_scale=RET_DK ** -0.5)
            lg_f = -jax.nn.softplus(-ret_decay_f[j].astype(F32))
            lg_b = -jax.nn.softplus(-ret_decay_b[j].astype(F32))
            mixed = _retention(proj, lg_f, lg_b, segs)
            x, xp = _out_ln(mixed, ret_w_out[j].astype(BF16), x, ln_mix_g[i], ln_mix_b[i], "ret_out")
        else:
            lambda_init = 0.8 - 0.6 * math.exp(-0.3 * i)
            nq = DIFF_HEADS * 2 * DIFF_DH
            proj = _proj(xb, diff_w_in[j].astype(BF16), _diff_tables(s_max), segs, mode="diff",
                         n_q_cols=nq, n_qk_cols=2 * nq, q_scale=DIFF_DH ** -0.5 * LOG2E, k_scale=1.0)
            lam = (jnp.exp(jnp.sum(diff_lam_q1[j].astype(F32) * diff_lam_k1[j].astype(F32)))
                   - jnp.exp(jnp.sum(diff_lam_q2[j].astype(F32) * diff_lam_k2[j].astype(F32))) + lambda_init)
            mixed = _diff_attention(proj, lam.reshape(1), diff_subln_g[j],
                                    [(0, sp, bp), (tp, ss, bs)], lambda_init)
            x, xp = _out_ln(mixed, diff_w_out[j].astype(BF16), x, ln_mix_g[i], ln_mix_b[i], "diff_out")
        last = i == DEPTH - 1
        x, xb = _moe_layer(x, xp, i, router_w[i], router_b[i], exp_w_gu, exp_w_down, shared_w_gu, shared_w_down,
                           ln_ffn_g[i], ln_ffn_b[i], tp if last else None)
    return x.reshape(bp, sp, d), xb.reshape(bs, ss, d)
```

```python
import functools
import math

import jax
import jax.numpy as jnp
from jax import lax
from jax.experimental import pallas as pl
from jax.experimental.pallas import tpu as pltpu

F32 = jnp.float32
BF16 = jnp.bfloat16
I32 = jnp.int32
U32 = jnp.uint32

D_MODEL = 2048
DEPTH = 2
ALPHA = (2 * DEPTH) ** 0.25
LN_EPS = 1e-5
RET_HEADS = 8
RET_DK = D_MODEL // RET_HEADS
RET_DV = 2 * RET_DK
RET_THETA = 10000.0
DIFF_HEADS = 8
DIFF_DH = D_MODEL // (2 * DIFF_HEADS)
DIFF_DV = 2 * DIFF_DH
ROPE_THETA = 500000.0
ROPE_DIMS = DIFF_DH // 4
N_EXPERTS = 64
TOP_K = 8
N_GROUPS = 8
TOPK_GROUPS = 4
GROUP_SIZE = N_EXPERTS // N_GROUPS
EXPERT_HIDDEN = 512
SHARED_HIDDEN = 512
ROUTED_SCALE = 2.5

LANES = 128
SUBLANES = 8
VMEM_LIMIT = 48 * 1024 * 1024

RET_CHUNK = 256
RET_GROUP = 4
EXPERT_ROWS = 512
DISPATCH_TILE = 256
COMBINE_TILE = 128
ATTN_Q_TILE = 512
NEG_INF = float("-inf")
LOG2E = math.log2(math.e)


def _cparams(sem):
    return pltpu.CompilerParams(dimension_semantics=sem, vmem_limit_bytes=VMEM_LIMIT)


def _pick_tile(n, prefs):
    for p in prefs:
        if n % p == 0:
            return p
    raise ValueError(f"no tile in {prefs} divides {n}")


def _seg_local_block(i, segs_blocks):
    val = None
    for first, per in segs_blocks:
        loc = lax.rem(i - first, per)
        val = loc if val is None else jnp.where(i >= first, loc, val)
    return val


def _proj_kernel(*refs, src_first, mode, n_q_tiles, n_qk_tiles, q_scale, k_scale):
    n_src = len(src_first)
    x_refs = refs[:n_src]
    w_ref, t0_ref, t1_ref, t2_ref, o_ref = refs[n_src:n_src + 5]
    i = pl.program_id(0)
    j = pl.program_id(1)
    if n_src == 1:
        x = x_refs[0][...]
    else:
        xb_ref = refs[n_src + 5]
        for s in range(n_src):
            hi = src_first[s + 1] if s + 1 < n_src else None
            in_src = (i >= src_first[s]) if hi is None else ((i >= src_first[s]) & (i < hi))

            @pl.when((j == 0) & in_src)
            def _(s=s):
                xb_ref[...] = x_refs[s][...].astype(BF16)
        x = xb_ref[...]
    acc = jnp.dot(x, w_ref[...], preferred_element_type=F32)
    tn = acc.shape[1]

    @pl.when(j >= n_qk_tiles)
    def _plain():
        o_ref[...] = acc.astype(o_ref.dtype)

    @pl.when(j < n_qk_tiles)
    def _rope():
        scale = jnp.where(j < n_q_tiles, q_scale, k_scale)
        if mode == "ret":
            cos = t0_ref[...]
            sin = t1_ref[...]
            half = RET_DK // 2
            for hs in range(tn // RET_DK):
                lo = hs * RET_DK
                x1 = acc[:, lo:lo + half]
                x2 = acc[:, lo + half:lo + RET_DK]
                o_ref[:, lo:lo + half] = ((x1 * cos - x2 * sin) * scale).astype(o_ref.dtype)
                o_ref[:, lo + half:lo + RET_DK] = ((x2 * cos + x1 * sin) * scale).astype(o_ref.dtype)
        else:
            c = t0_ref[...]
            s_up = t1_ref[...]
            s_dn = t2_ref[...]
            half = ROPE_DIMS // 2
            for g in range(tn // DIFF_DH):
                lo = g * DIFF_DH
                seg = acc[:, lo:lo + DIFF_DH]
                rot = seg * c + pltpu.roll(seg, half, 1) * s_up + pltpu.roll(seg, DIFF_DH - half, 1) * s_dn
                o_ref[:, lo:lo + DIFF_DH] = (rot * scale).astype(o_ref.dtype)


def _row_sources(sources, tm, index_of, **spec_kwargs):
    firsts, specs, first = [], [], 0
    for src in sources:
        nblk = src.shape[0] // tm
        firsts.append(first)
        specs.append(pl.BlockSpec((tm, src.shape[1]),
                                  lambda *g, first=first, nblk=nblk: (jnp.clip(index_of(*g) - first, 0, nblk - 1), 0),
                                  **spec_kwargs))
        first += nblk
    return tuple(firsts), specs, first


def _proj(sources, w, tables, segs, *, mode, n_q_cols, n_qk_cols, q_scale, k_scale):
    k = sources[0].shape[1]
    n = w.shape[1]
    seg_len = [s for _, s in segs]
    tm = _pick_tile(math.gcd(*seg_len), (1024, 512, 256))
    tn = 1024
    segs_blocks = [(start // tm, s // tm) for start, s in segs]
    tw = tables[0].shape[1]
    tab_spec = pl.BlockSpec((tm, tw), lambda i, j: (_seg_local_block(i, segs_blocks), 0))
    kwargs = {"pipeline_mode": pl.Buffered(1)} if len(sources) > 1 else {}
    src_first, src_specs, n_row_blocks = _row_sources(sources, tm, lambda i, j: i, **kwargs)
    kern = functools.partial(_proj_kernel, src_first=src_first, mode=mode, n_q_tiles=n_q_cols // tn,
                             n_qk_tiles=n_qk_cols // tn, q_scale=q_scale, k_scale=k_scale)
    return pl.pallas_call(
        kern,
        grid=(n_row_blocks, n // tn),
        in_specs=src_specs + [pl.BlockSpec((k, tn), lambda i, j: (0, j)), tab_spec, tab_spec, tab_spec],
        out_specs=pl.BlockSpec((tm, tn), lambda i, j: (i, j)),
        out_shape=jax.ShapeDtypeStruct((n_row_blocks * tm, n), BF16),
        scratch_shapes=[pltpu.VMEM((tm, k), BF16)] if len(sources) > 1 else [],
        compiler_params=_cparams(("arbitrary", "arbitrary")),
        name=f"proj_{mode}",
    )(*sources, w, *tables)


HIGH_HALF = 0xFFFF0000


def _pack_rows(y):
    n = y.shape[1] // 2
    lo = lax.bitcast_convert_type(y[:, :n].astype(BF16).astype(F32), U32) >> 16
    hi = lax.bitcast_convert_type(y[:, n:].astype(BF16).astype(F32), U32) & jnp.uint32(HIGH_HALF)
    return lo | hi


def _unpack_rows(w):
    lo = lax.bitcast_convert_type(w << 16, F32)
    hi = lax.bitcast_convert_type(w & jnp.uint32(HIGH_HALF), F32)
    return lo, hi


PACK_WORDS = D_MODEL // 2
PACK_SUB = PACK_WORDS // LANES
assert PACK_SUB == SUBLANES


def _tile_of(row):
    start = row * PACK_SUB
    return pl.ds(start if isinstance(start, int) else pl.multiple_of(start, PACK_SUB), PACK_SUB)


def _store_packed(ref, words):
    r = words.shape[0]
    for s in range(PACK_SUB):
        ref[pl.ds(s, r, stride=PACK_SUB), :] = words[:, s * LANES:(s + 1) * LANES]


def _load_packed(ref, r):
    return jnp.concatenate([ref[pl.ds(s, r, stride=PACK_SUB), :] for s in range(PACK_SUB)], axis=1)


def _layer_norm_rows(z, g, b):
    mu = jnp.mean(z, axis=-1, keepdims=True)
    zc = z - mu
    var = jnp.mean(zc * zc, axis=-1, keepdims=True)
    return zc * lax.rsqrt(var + LN_EPS) * g + b


def _out_ln_kernel(a_ref, w_ref, g_ref, b_ref, *refs, src_first):
    n_src = len(src_first)
    x_refs = refs[:n_src]
    o_ref, op_ref = refs[n_src:]
    i = pl.program_id(0)
    h = jnp.dot(a_ref[...], w_ref[...], preferred_element_type=F32)

    def finish(x_ref):
        y = _layer_norm_rows(ALPHA * x_ref[...] + h, g_ref[...], b_ref[...])
        o_ref[...] = y
        _store_packed(op_ref, _pack_rows(y))

    if n_src == 1:
        finish(x_refs[0])
    else:
        for s in range(n_src):
            hi = src_first[s + 1] if s + 1 < n_src else None
            in_src = (i >= src_first[s]) if hi is None else ((i >= src_first[s]) & (i < hi))

            @pl.when(in_src)
            def _(s=s):
                finish(x_refs[s])


def _out_ln(a, w, x_sources, g, b, name):
    m, kk = a.shape
    d = w.shape[1]
    tm = 256
    row = pl.BlockSpec((tm, d), lambda i: (i, 0))
    vec = pl.BlockSpec((1, d), lambda i: (0, 0))
    src_first, src_specs, n_row_blocks = _row_sources(x_sources, tm, lambda i: i)
    assert n_row_blocks * tm == m
    return pl.pallas_call(
        functools.partial(_out_ln_kernel, src_first=src_first),
        grid=(m // tm,),
        in_specs=[pl.BlockSpec((tm, kk), lambda i: (i, 0)),
                  pl.BlockSpec((kk, d), lambda i: (0, 0), pipeline_mode=pl.Buffered(1)), vec, vec] + src_specs,
        out_specs=[row, pl.BlockSpec((tm * PACK_SUB, LANES), lambda i: (i, 0))],
        out_shape=[jax.ShapeDtypeStruct((m, d), F32), jax.ShapeDtypeStruct((m * PACK_SUB, LANES), U32)],
        compiler_params=_cparams(("arbitrary",)),
        name=name,
    )(a, w, g.reshape(1, d), b.reshape(1, d), *x_sources)


def _ret_kernel(lg_ref, cd_ref, q_ref, k_ref, v_ref, *rest, backward, segs_chunks):
    if backward:
        of_ref, g_ref, o_ref, state_ref, dmat_ref, qd_ref, kd_ref = rest
    else:
        o_ref, state_ref, dmat_ref, qd_ref, kd_ref = rest
    h = pl.program_id(0)
    c = pl.program_id(1)
    nc = pl.num_programs(1)
    cc = (nc - 1 - c) if backward else c
    lg = lg_ref[h]
    cs = RET_CHUNK

    @pl.when(c == 0)
    def _tables():
        ii = lax.broadcasted_iota(I32, (cs, cs), 0)
        jj = lax.broadcasted_iota(I32, (cs, cs), 1)
        if backward:
            mask = jj > ii
            dist = (jj - ii).astype(F32)
        else:
            mask = ii >= jj
            dist = (ii - jj).astype(F32)
        dmat_ref[...] = jnp.where(mask, jnp.exp(jnp.where(mask, dist, 0.0) * lg), 0.0)
        pos = lax.broadcasted_iota(I32, (cs, 1), 0).astype(F32)
        if backward:
            qd_ref[...] = jnp.exp((cs - pos) * lg)
            kd_ref[...] = jnp.exp(pos * lg)
        else:
            qd_ref[...] = jnp.exp((pos + 1.0) * lg)
            kd_ref[...] = jnp.exp((cs - 1.0 - pos) * lg)

    loc = _seg_local_block(cc, segs_chunks)
    if backward:
        per = None
        for first, p in segs_chunks:
            per = p if per is None else jnp.where(cc >= first, p, per)
        is_start = loc == per - 1
    else:
        is_start = loc == 0

    @pl.when(is_start)
    def _reset():
        state_ref[...] = jnp.zeros_like(state_ref)

    state = state_ref[...]
    n_sub = q_ref.shape[0] // cs
    for sub in (range(n_sub - 1, -1, -1) if backward else range(n_sub)):
        rows = pl.ds(sub * cs, cs)
        q = q_ref[rows, :]
        k = k_ref[rows, :]
        v = v_ref[rows, :]
        att = lax.dot_general(q, k, (((1,), (1,)), ((), ())), preferred_element_type=F32) * dmat_ref[...]
        inner = jnp.dot(att.astype(BF16), v, preferred_element_type=F32)
        cross = jnp.dot(q, state.astype(BF16), preferred_element_type=F32) * qd_ref[...]
        o = inner + cross
        kt = (k.astype(F32) * kd_ref[...]).T.astype(BF16)
        state = state * cd_ref[h] + jnp.dot(kt, v, preferred_element_type=F32)

        if backward:
            tot = of_ref[rows, :] + o
            mu = jnp.mean(tot, axis=-1, keepdims=True)
            tc = tot - mu
            var = jnp.mean(tc * tc, axis=-1, keepdims=True)
            on = tc * lax.rsqrt(var + LN_EPS)
            g = g_ref[rows, :].astype(F32)
            o_ref[rows, :] = ((g * (1.0 / (1.0 + jnp.exp(-g)))) * on).astype(o_ref.dtype)
        else:
            o_ref[rows, :] = o
    state_ref[...] = state


def _retention(proj, lg_f, lg_b, segs):
    t = proj.shape[0]
    cs = RET_CHUNK
    rb = RET_CHUNK * RET_GROUP
    nc = t // rb
    segs_chunks = [(start // rb, s // rb) for start, s in segs]
    qk_blocks = (RET_HEADS * RET_DK) // RET_DK
    v_blocks0 = (2 * RET_HEADS * RET_DK) // RET_DV
    g_blocks0 = v_blocks0 + RET_HEADS
    smem = pl.BlockSpec(memory_space=pltpu.SMEM)
    scratch = [pltpu.VMEM((RET_DK, RET_DV), F32), pltpu.VMEM((cs, cs), F32),
               pltpu.VMEM((cs, 1), F32), pltpu.VMEM((cs, 1), F32)]

    def run(backward, lg, extra):
        def cmap(c):
            return (nc - 1 - c) if backward else c
        in_specs = [smem, smem,
                    pl.BlockSpec((rb, RET_DK), lambda h, c: (cmap(c), h)),
                    pl.BlockSpec((rb, RET_DK), lambda h, c: (cmap(c), qk_blocks + h)),
                    pl.BlockSpec((rb, RET_DV), lambda h, c: (cmap(c), v_blocks0 + h))]
        args = [lg, jnp.exp(cs * lg), proj, proj, proj]
        if backward:
            in_specs += [pl.BlockSpec((rb, RET_DV), lambda h, c: (cmap(c), h)),
                         pl.BlockSpec((rb, RET_DV), lambda h, c: (cmap(c), g_blocks0 + h))]
            args += [extra, proj]
        return pl.pallas_call(
            functools.partial(_ret_kernel, backward=backward, segs_chunks=segs_chunks),
            grid=(RET_HEADS, nc),
            in_specs=in_specs,
            out_specs=pl.BlockSpec((rb, RET_DV), lambda h, c: (cmap(c), h)),
            out_shape=jax.ShapeDtypeStruct((t, RET_HEADS * RET_DV), BF16 if backward else F32),
            scratch_shapes=scratch,
            compiler_params=_cparams(("arbitrary", "arbitrary")),
            name="retention_bwd" if backward else "retention_fwd",
        )(*args)

    o_f = run(False, lg_f, None)
    return run(True, lg_b, o_f)


def _attn_kernel(lam_ref, q_ref, k_ref, v_ref, sg_ref, *rest, out_scale):
    o_ref = rest[-1]
    v = v_ref[...]
    outs = []
    for c in range(2):
        q = q_ref[:, c * DIFF_DH:(c + 1) * DIFF_DH]
        k = k_ref[:, c * DIFF_DH:(c + 1) * DIFF_DH]
        s = lax.dot_general(q, k, (((1,), (1,)), ((), ())), preferred_element_type=F32)
        p = jnp.exp2(s - jnp.max(s, axis=-1, keepdims=True))
        den = jnp.sum(p, axis=-1, keepdims=True)
        outs.append(jnp.dot(p.astype(BF16), v, preferred_element_type=F32) / den)
    o = outs[0] - lam_ref[0] * outs[1]
    ms = jnp.mean(o * o, axis=-1, keepdims=True)
    o_ref[...] = (o * lax.rsqrt(ms + LN_EPS) * sg_ref[...] * out_scale).astype(o_ref.dtype)


def _diff_attention(proj, lam, subln_g, segs, lambda_init):
    nq_blocks = DIFF_HEADS
    t = proj.shape[0]
    out = None
    for start, s, n_seq in segs:
        tq = ATTN_Q_TILE
        assert s % tq == 0 and start % s == 0
        qb0 = start // tq
        kb0 = start // s
        nqt = s // tq
        in_specs = [pl.BlockSpec(memory_space=pltpu.SMEM),
                    pl.BlockSpec((tq, 2 * DIFF_DH), lambda b, h, i: (qb0 + b * nqt + i, h)),
                    pl.BlockSpec((s, 2 * DIFF_DH), lambda b, h, i: (kb0 + b, nq_blocks + h)),
                    pl.BlockSpec((s, DIFF_DV), lambda b, h, i: (kb0 + b, 2 * nq_blocks + h)),
                    pl.BlockSpec((1, DIFF_DV), lambda b, h, i: (0, 0))]
        args = [lam, proj, proj, proj, subln_g.reshape(1, DIFF_DV)]
        aliases = {}
        if out is not None:
            in_specs.append(pl.BlockSpec(memory_space=pl.ANY))
            args.append(out)
            aliases = {len(args) - 1: 0}
        out = pl.pallas_call(
            functools.partial(_attn_kernel, out_scale=1.0 - lambda_init),
            grid=(n_seq, DIFF_HEADS, nqt),
            in_specs=in_specs,
            out_specs=pl.BlockSpec((tq, DIFF_DV), lambda b, h, i: (qb0 + b * nqt + i, h)),
            out_shape=jax.ShapeDtypeStruct((t, DIFF_HEADS * DIFF_DV), BF16),
            input_output_aliases=aliases,
            compiler_params=_cparams(("arbitrary", "arbitrary", "arbitrary")),
            name=f"diff_attn_s{s}",
        )(*args)
    return out


def _first_max(cur, rows, n_rows):
    m = jnp.max(cur, axis=0, keepdims=True)
    idx = jnp.min(jnp.where(cur == m, rows, float(n_rows)), axis=0, keepdims=True)
    return m, idx


def _router_kernel(x_ref, wt_ref, b_ref, idx_ref, gate_ref, rank_ref, cnt_ref, carry_ref):
    i = pl.program_id(0)

    @pl.when(i == 0)
    def _init():
        carry_ref[...] = jnp.zeros_like(carry_ref)

    tn = x_ref.shape[0]
    x = x_ref[...]
    xh = x.astype(BF16)
    xl = (x - xh.astype(F32)).astype(BF16)
    w = wt_ref[...]
    wh = w.astype(BF16)
    wl = (w - wh.astype(F32)).astype(BF16)

    def nt(a, b):
        return lax.dot_general(a, b, (((1,), (1,)), ((), ())), preferred_element_type=F32)

    logits = nt(wh, xh) + nt(wh, xl) + nt(wl, xh)
    scores = 1.0 / (1.0 + jnp.exp(-logits))
    biased = scores + b_ref[...]

    rows_g = lax.broadcasted_iota(I32, (GROUP_SIZE, tn), 0).astype(F32)
    rows_e = lax.broadcasted_iota(I32, (N_EXPERTS, tn), 0).astype(F32)

    gscore = []
    for g in range(N_GROUPS):
        slab = biased[g * GROUP_SIZE:(g + 1) * GROUP_SIZE]
        m1, i1 = _first_max(slab, rows_g, GROUP_SIZE)
        m2 = jnp.max(jnp.where(rows_g == i1, NEG_INF, slab), axis=0, keepdims=True)
        gscore.append(m1 + m2)
    cur = jnp.concatenate(gscore, axis=0)
    rows_grp = lax.broadcasted_iota(I32, (N_GROUPS, tn), 0).astype(F32)
    gsel = jnp.zeros((N_GROUPS, tn), F32)
    for _ in range(TOPK_GROUPS):
        _, gi = _first_max(cur, rows_grp, N_GROUPS)
        pick = rows_grp == gi
        gsel = jnp.where(pick, 1.0, gsel)
        cur = jnp.where(pick, NEG_INF, cur)

    masked = jnp.concatenate(
        [jnp.where(gsel[g:g + 1] > 0.0, biased[g * GROUP_SIZE:(g + 1) * GROUP_SIZE], NEG_INF)
         for g in range(N_GROUPS)], axis=0)

    sel = jnp.zeros((N_EXPERTS, tn), F32)
    picks = []
    cur = masked
    for _ in range(TOP_K):
        _, ei = _first_max(cur, rows_e, N_EXPERTS)
        pick = rows_e == ei
        sel = jnp.where(pick, 1.0, sel)
        cur = jnp.where(pick, NEG_INF, cur)
        picks.append(ei)

    gsum = jnp.sum(scores * sel, axis=0, keepdims=True)

    ta = lax.broadcasted_iota(I32, (tn, tn), 0)
    tb = lax.broadcasted_iota(I32, (tn, tn), 1)
    before = jnp.where(ta < tb, 1.0, 0.0).astype(BF16)
    rank = jnp.dot(sel.astype(BF16), before, preferred_element_type=F32) + carry_ref[...]
    carry_ref[...] = carry_ref[...] + jnp.sum(sel, axis=1, keepdims=True)
    cnt_ref[...] = carry_ref[...]

    gates = []
    ranks = []
    for ei in picks:
        hit = rows_e == ei
        gates.append(jnp.sum(jnp.where(hit, scores, 0.0), axis=0, keepdims=True) / gsum * ROUTED_SCALE)
        ranks.append(jnp.sum(jnp.where(hit, rank, 0.0), axis=0, keepdims=True))
    idx_ref[...] = jnp.concatenate(picks, axis=0).astype(I32)
    gate_ref[...] = jnp.concatenate(gates, axis=0)
    rank_ref[...] = jnp.concatenate(ranks, axis=0).astype(I32)


def _router(x, w_router, b_router):
    t, d = x.shape
    tn = _pick_tile(t, (512, 256))
    kt = pl.BlockSpec((TOP_K, tn), lambda i: (0, i))
    return pl.pallas_call(
        _router_kernel,
        grid=(t // tn,),
        in_specs=[pl.BlockSpec((tn, d), lambda i: (i, 0)),
                  pl.BlockSpec((N_EXPERTS, d), lambda i: (0, 0)),
                  pl.BlockSpec((N_EXPERTS, 1), lambda i: (0, 0))],
        out_specs=[kt, kt, kt, pl.BlockSpec((N_EXPERTS, 1), lambda i: (0, 0))],
        out_shape=[jax.ShapeDtypeStruct((TOP_K, t), I32), jax.ShapeDtypeStruct((TOP_K, t), F32),
                   jax.ShapeDtypeStruct((TOP_K, t), I32), jax.ShapeDtypeStruct((N_EXPERTS, 1), F32)],
        scratch_shapes=[pltpu.VMEM((N_EXPERTS, 1), F32)],
        compiler_params=_cparams(("arbitrary",)),
        name="router",
    )(x, w_router.T, b_router.reshape(N_EXPERTS, 1))


def _slot_kernel(idx_ref, rank_ref, start_ref, slot_ref):
    tn = idx_ref.shape[1]
    rows_e = lax.broadcasted_iota(I32, (N_EXPERTS, tn), 0)
    out = []
    for k in range(TOP_K):
        hit = rows_e == idx_ref[k:k + 1, :]
        base = jnp.sum(jnp.where(hit, start_ref[...], 0.0), axis=0, keepdims=True)
        out.append(base.astype(I32) + rank_ref[k:k + 1, :])
    slot_ref[...] = jnp.concatenate(out, axis=0)


def _slots(idx_t, rank_t, pad_start):
    t = idx_t.shape[1]
    tn = _pick_tile(t, (512, 256))
    kt = pl.BlockSpec((TOP_K, tn), lambda i: (0, i))
    return pl.pallas_call(
        _slot_kernel,
        grid=(t // tn,),
        in_specs=[kt, kt, pl.BlockSpec((N_EXPERTS, 1), lambda i: (0, 0))],
        out_specs=kt,
        out_shape=jax.ShapeDtypeStruct((TOP_K, t), I32),
        compiler_params=_cparams(("arbitrary",)),
        name="slots",
    )(idx_t, rank_t, pad_start.astype(F32).reshape(N_EXPERTS, 1))


def _row_copy(src, src_row, dst, dst_row, sem):
    return pltpu.make_async_copy(src.at[_tile_of(src_row)], dst.at[_tile_of(dst_row)], sem)


def _dispatch_kernel(pad_end_ref, padded_ref, slot_hbm, x_ref, xs_hbm, slot_smem, zero_ref, sem, slot_sem):
    i = pl.program_id(0)
    tt = x_ref.shape[0] // PACK_SUB
    bm = zero_ref.shape[0] // PACK_SUB
    slot_cp = pltpu.make_async_copy(slot_hbm.at[i], slot_smem, slot_sem)
    slot_cp.start()

    @pl.when(i == 0)
    def _clear():
        zero_ref[...] = jnp.zeros_like(zero_ref)

        def pad_copy(e):
            off = pl.multiple_of((pad_end_ref[e] - bm) * PACK_SUB, bm * PACK_SUB)
            return pltpu.make_async_copy(zero_ref, xs_hbm.at[pl.ds(off, bm * PACK_SUB)], sem)

        def start(e, carry):
            @pl.when(padded_ref[e] > 0)
            def _():
                pad_copy(e).start()
            return carry

        def wait(e, carry):
            @pl.when(padded_ref[e] > 0)
            def _():
                pad_copy(e).wait()
            return carry

        lax.fori_loop(0, N_EXPERTS, start, 0)
        lax.fori_loop(0, N_EXPERTS, wait, 0)

    slot_cp.wait()
    per_row = LANES // TOP_K

    def issue(r, carry):
        for j in range(LANES):
            _row_copy(x_ref, r * per_row + j // TOP_K, xs_hbm, slot_smem[r, j], sem).start(priority=j % 2)
        return carry

    lax.fori_loop(0, tt // per_row, issue, 0)

    def drain(t, carry):
        for k in range(TOP_K):
            _row_copy(x_ref, t, xs_hbm, 0, sem).wait()
        return carry

    lax.fori_loop(0, tt, drain, 0)


def _dispatch(xp, slot_tiles, pad_end, padded, n_slots):
    t = xp.shape[0] // PACK_SUB
    tt = DISPATCH_TILE
    grid_spec = pltpu.PrefetchScalarGridSpec(
        num_scalar_prefetch=2,
        grid=(t // tt,),
        in_specs=[pl.BlockSpec(memory_space=pl.ANY),
                  pl.BlockSpec((tt * PACK_SUB, LANES), lambda i, pe, pd: (i, 0))],
        out_specs=pl.BlockSpec(memory_space=pl.ANY),
        scratch_shapes=[pltpu.SMEM((tt * TOP_K // LANES, LANES), I32),
                        pltpu.VMEM((EXPERT_ROWS * PACK_SUB, LANES), U32),
                        pltpu.SemaphoreType.DMA, pltpu.SemaphoreType.DMA],
    )
    return pl.pallas_call(
        _dispatch_kernel,
        grid_spec=grid_spec,
        out_shape=jax.ShapeDtypeStruct((n_slots * PACK_SUB, LANES), U32),
        compiler_params=_cparams(("arbitrary",)),
        name="dispatch",
    )(pad_end, padded, slot_tiles, xp)


def _ffn_kernel(block_e_ref, n_used_ref, x_ref, wgu_ref, wd_ref, o_ref, wgu_bf, wd_bf, *, pack_out):
    b = pl.program_id(0)

    @pl.when(b < n_used_ref[0])
    def _():
        e = block_e_ref[b]
        e_prev = block_e_ref[jnp.maximum(b - 1, 0)]

        @pl.when((b == 0) | (e != e_prev))
        def _():
            wgu_bf[...] = wgu_ref[0, 0].astype(BF16)
            wd_bf[...] = wd_ref[0, 0].astype(BF16)

        hidden = wd_bf.shape[0]
        rows = x_ref.shape[0] // PACK_SUB
        lo, hi = _unpack_rows(_load_packed(x_ref, rows))
        gu = (jnp.dot(lo.astype(BF16), wgu_bf[:PACK_WORDS], preferred_element_type=F32)
              + jnp.dot(hi.astype(BF16), wgu_bf[PACK_WORDS:], preferred_element_type=F32))
        gate = gu[:, :hidden]
        act = gate * (1.0 / (1.0 + jnp.exp(-gate))) * gu[:, hidden:]
        y = jnp.dot(act.astype(BF16), wd_bf[...], preferred_element_type=F32)
        if pack_out:
            _store_packed(o_ref, _pack_rows(y))
        else:
            o_ref[...] = y


def _ffn(xp, w_gu, w_down, layer, block_e, n_used, rows, pack_out, name):
    n = xp.shape[0] // PACK_SUB
    d = 2 * PACK_WORDS
    hidden = w_down.shape[2]

    def row_map(b, be, nu):
        return (jnp.minimum(b, nu[0] - 1), 0)

    packed_rows = pl.BlockSpec((rows * PACK_SUB, LANES), row_map)
    if pack_out:
        out_spec, out_shape = packed_rows, jax.ShapeDtypeStruct((n * PACK_SUB, LANES), U32)
    else:
        out_spec, out_shape = pl.BlockSpec((rows, d), row_map), jax.ShapeDtypeStruct((n, d), F32)
    grid_spec = pltpu.PrefetchScalarGridSpec(
        num_scalar_prefetch=2,
        grid=(n // rows,),
        in_specs=[packed_rows,
                  pl.BlockSpec((1, 1, d, 2 * hidden), lambda b, be, nu: (layer, be[b], 0, 0)),
                  pl.BlockSpec((1, 1, hidden, d), lambda b, be, nu: (layer, be[b], 0, 0))],
        out_specs=out_spec,
        scratch_shapes=[pltpu.VMEM((d, 2 * hidden), BF16), pltpu.VMEM((hidden, d), BF16)],
    )
    return pl.pallas_call(
        functools.partial(_ffn_kernel, pack_out=pack_out),
        grid_spec=grid_spec,
        out_shape=out_shape,
        compiler_params=_cparams(("arbitrary",)),
        name=name,
    )(block_e, n_used, xp, w_gu, w_down)


def _combine_kernel(slot_hbm, x_ref, sh_ref, gate_ref, g_ref, b_ref, ys_hbm, o_ref, *rest, tile0, with_bf16):
    if with_bf16:
        ob_ref, *rest = rest
    slots0, slots1, buf0, buf1, sems, slot_sem = rest
    slot_smem = (slots0, slots1)
    bufs = (buf0, buf1)
    i = pl.program_id(0)
    n = pl.num_programs(0)
    tt = x_ref.shape[0]
    per_row = LANES // TOP_K
    groups = tt // per_row

    def load_slots(tile, par):
        slot_cp = pltpu.make_async_copy(slot_hbm.at[tile0 + tile], slot_smem[par], slot_sem)
        slot_cp.start()
        slot_cp.wait()

    def issue_group(r, par):
        for j in range(LANES):
            _row_copy(ys_hbm, slot_smem[par][r, j], bufs[par].at[j % TOP_K], r * per_row + j // TOP_K,
                      sems.at[par]).start(priority=j % 2)

    def drain(par):
        def body(t, carry):
            for k in range(TOP_K):
                _row_copy(ys_hbm, 0, bufs[par].at[k], t, sems.at[par]).wait()
            return carry
        lax.fori_loop(0, tt, body, 0)

    def reduce_group(r, par):
        rows = slice(r * per_row, (r + 1) * per_row)
        sh = sh_ref[rows, :]
        h_lo = sh[:, :PACK_WORDS]
        h_hi = sh[:, PACK_WORDS:]
        gate = gate_ref[rows, :]
        for k in range(TOP_K):
            words = jnp.concatenate(
                [bufs[par][k, pl.ds(r * per_row * PACK_SUB + s, per_row, stride=PACK_SUB), :]
                 for s in range(PACK_SUB)], axis=1)
            lo, hi = _unpack_rows(words)
            h_lo = h_lo + lo * gate[:, k:k + 1]
            h_hi = h_hi + hi * gate[:, k:k + 1]
        h = jnp.concatenate([h_lo, h_hi], axis=1)
        y = _layer_norm_rows(ALPHA * x_ref[rows, :] + h, g_ref[...], b_ref[...])
        o_ref[rows, :] = y
        if with_bf16:
            ob_ref[rows, :] = y.astype(BF16)

    @pl.when(i == 0)
    def _():
        load_slots(0, 0)
        for r in range(groups):
            issue_group(r, 0)

    nxt = jnp.minimum(i + 1, n - 1)
    for par in range(2):
        @pl.when(i % 2 == par)
        def _(par=par):
            drain(par)
            load_slots(nxt, 1 - par)
            for r in range(groups):
                issue_group(r, 1 - par)
                reduce_group(r, par)

            @pl.when(i == n - 1)
            def _():
                drain(1 - par)


def _combine(x, shared, ys, slot_tiles, gate, g, b, row_range, with_bf16):
    d = x.shape[1]
    tt = COMBINE_TILE
    r0, r1 = row_range
    tile0 = r0 // tt
    row_in = pl.BlockSpec((tt, d), lambda i: (tile0 + i, 0))
    row_out = pl.BlockSpec((tt, d), lambda i: (i, 0))
    vec = pl.BlockSpec((1, d), lambda i: (0, 0))
    out_specs = [row_out]
    out_shape = [jax.ShapeDtypeStruct((r1 - r0, d), F32)]
    if with_bf16:
        out_specs.append(row_out)
        out_shape.append(jax.ShapeDtypeStruct((r1 - r0, d), BF16))
    return pl.pallas_call(
        functools.partial(_combine_kernel, tile0=tile0, with_bf16=with_bf16),
        grid=((r1 - r0) // tt,),
        in_specs=[pl.BlockSpec(memory_space=pl.ANY), row_in, row_in,
                  pl.BlockSpec((tt, TOP_K), lambda i: (tile0 + i, 0)), vec, vec,
                  pl.BlockSpec(memory_space=pl.ANY)],
        out_specs=out_specs,
        out_shape=out_shape,
        scratch_shapes=[pltpu.SMEM((tt * TOP_K // LANES, LANES), I32), pltpu.SMEM((tt * TOP_K // LANES, LANES), I32),
                        pltpu.VMEM((TOP_K, tt * PACK_SUB, LANES), U32), pltpu.VMEM((TOP_K, tt * PACK_SUB, LANES), U32),
                        pltpu.SemaphoreType.DMA((2,)), pltpu.SemaphoreType.DMA],
        compiler_params=_cparams(("arbitrary",)),
        name="combine",
    )(slot_tiles, x, shared, gate, g.reshape(1, d), b.reshape(1, d), ys)


def _moe_layer(x, xp, layer, w_router, b_router, w_gu, w_down, sw_gu, sw_down, ln_g, ln_b, split_rows):
    t, d = x.shape
    bm = EXPERT_ROWS
    idx_t, gate_t, rank_t, counts = _router(x, w_router, b_router)

    counts = counts.reshape(N_EXPERTS).astype(I32)
    padded = (counts + bm - 1) // bm * bm
    pad_end = jnp.cumsum(padded)
    pad_start = pad_end - padded
    n_blocks = (t * TOP_K + N_EXPERTS * (bm - 1) + bm - 1) // bm
    n_slots = n_blocks * bm
    block_first = jnp.arange(n_blocks, dtype=I32) * bm
    block_e = jnp.minimum(jnp.sum((pad_end[None, :] <= block_first[:, None]).astype(I32), axis=1), N_EXPERTS - 1)
    n_used = (pad_end[-1:] // bm).astype(I32)

    slot_t = _slots(idx_t, rank_t, pad_start)
    slot_flat = slot_t.T.reshape(t * TOP_K // LANES, LANES)

    def slot_tiles(tile):
        return slot_flat.reshape(t // tile, tile * TOP_K // LANES, LANES)

    xs = _dispatch(xp, slot_tiles(DISPATCH_TILE), pad_end.astype(I32), padded.astype(I32), n_slots)
    ys = _ffn(xs, w_gu, w_down, layer, block_e, n_used, bm, True, "expert_ffn")
    shared_rows = _pick_tile(t, (512, 256))
    shared = _ffn(xp, sw_gu[:, None], sw_down[:, None], layer, jnp.zeros((t // shared_rows,), I32),
                  jnp.full((1,), t // shared_rows, I32), shared_rows, False, "shared_ffn")
    args = (x, shared, ys, slot_tiles(COMBINE_TILE), gate_t.T, ln_g, ln_b)
    if split_rows is None:
        return _combine(*args, (0, t), True)
    return _combine(*args, (0, split_rows), False)[0], _combine(*args, (split_rows, t), False)[0]


def _rope_angles(s, theta, half):
    inv = theta ** (-jnp.arange(half, dtype=F32) / half)
    ang = jnp.arange(s, dtype=F32)[:, None] * inv[None, :]
    return jnp.cos(ang), jnp.sin(ang)


def _ret_tables(s):
    cos, sin = _rope_angles(s, RET_THETA, RET_DK // 2)
    return cos, sin, sin


def _diff_tables(s):
    half = ROPE_DIMS // 2
    cos, sin = _rope_angles(s, ROPE_THETA, half)
    rest = DIFF_DH - ROPE_DIMS
    zeros_h = jnp.zeros((s, half), F32)
    c = jnp.concatenate([cos, cos, jnp.ones((s, rest), F32)], axis=1)
    s_up = jnp.concatenate([zeros_h, sin, jnp.zeros((s, rest), F32)], axis=1)
    s_dn = jnp.concatenate([-sin, zeros_h, jnp.zeros((s, rest), F32)], axis=1)
    return c, s_up, s_dn


def kernel(x_prompt, x_sample, ret_w_in, ret_decay_f, ret_decay_b, ret_w_out, diff_w_in, diff_lam_q1, diff_lam_k1,
           diff_lam_q2, diff_lam_k2, diff_subln_g, diff_w_out, ln_mix_g, ln_mix_b, router_w, router_b, exp_w_gu,
           exp_w_down, shared_w_gu, shared_w_down, ln_ffn_g, ln_ffn_b):
    bp, sp, d = x_prompt.shape
    bs, ss, _ = x_sample.shape
    tp = bp * sp
    x_rows = [x_prompt.reshape(tp, d), x_sample.reshape(bs * ss, d)]
    proj_rows = x_rows
    segs = [(0, sp), (tp, ss)]
    s_max = max(sp, ss)

    for i in range(DEPTH):
        j = i // 2
        if i % 2 == 0:
            qk = RET_HEADS * RET_DK
            proj = _proj(proj_rows, ret_w_in[j].astype(BF16), _ret_tables(s_max), segs, mode="ret",
                         n_q_cols=qk, n_qk_cols=2 * qk, q_scale=1.0, k_scale=RET_DK ** -0.5)
            lg_f = -jax.nn.softplus(-ret_decay_f[j].astype(F32))
            lg_b = -jax.nn.softplus(-ret_decay_b[j].astype(F32))
            mixed = _retention(proj, lg_f, lg_b, segs)
            x, xp = _out_ln(mixed, ret_w_out[j].astype(BF16), x_rows, ln_mix_g[i], ln_mix_b[i], "ret_out")
        else:
            lambda_init = 0.8 - 0.6 * math.exp(-0.3 * i)
            nq = DIFF_HEADS * 2 * DIFF_DH
            proj = _proj(proj_rows, diff_w_in[j].astype(BF16), _diff_tables(s_max), segs, mode="diff",
                         n_q_cols=nq, n_qk_cols=2 * nq, q_scale=DIFF_DH ** -0.5 * LOG2E, k_scale=1.0)
            lam = (jnp.exp(jnp.sum(diff_lam_q1[j].astype(F32) * diff_lam_k1[j].astype(F32)))
                   - jnp.exp(jnp.sum(diff_lam_q2[j].astype(F32) * diff_lam_k2[j].astype(F32))) + lambda_init)
            mixed = _diff_attention(proj, lam.reshape(1), diff_subln_g[j],
                                    [(0, sp, bp), (tp, ss, bs)], lambda_init)
            x, xp = _out_ln(mixed, diff_w_out[j].astype(BF16), x_rows, ln_mix_g[i], ln_mix_b[i], "diff_out")
        last = i == DEPTH - 1
        x, xb = _moe_layer(x, xp, i, router_w[i], router_b[i], exp_w_gu, exp_w_down, shared_w_gu, shared_w_down,
                           ln_ffn_g[i], ln_ffn_b[i], tp if last else None)
        x_rows, proj_rows = [x], [xb]
    return x.reshape(bp, sp, d), xb.reshape(bs, ss, d)
```

```python
import functools
import math

import jax
import jax.numpy as jnp
from jax import lax
from jax.experimental import pallas as pl
from jax.experimental.pallas import tpu as pltpu

F32 = jnp.float32
BF16 = jnp.bfloat16
I32 = jnp.int32
U32 = jnp.uint32

D_MODEL = 2048
DEPTH = 2
ALPHA = (2 * DEPTH) ** 0.25
LN_EPS = 1e-5
RET_HEADS = 8
RET_DK = D_MODEL // RET_HEADS
RET_DV = 2 * RET_DK
RET_THETA = 10000.0
DIFF_HEADS = 8
DIFF_DH = D_MODEL // (2 * DIFF_HEADS)
DIFF_DV = 2 * DIFF_DH
ROPE_THETA = 500000.0
ROPE_DIMS = DIFF_DH // 4
N_EXPERTS = 64
TOP_K = 8
N_GROUPS = 8
TOPK_GROUPS = 4
GROUP_SIZE = N_EXPERTS // N_GROUPS
EXPERT_HIDDEN = 512
SHARED_HIDDEN = 512
ROUTED_SCALE = 2.5

LANES = 128
SUBLANES = 8
VMEM_LIMIT = 48 * 1024 * 1024

RET_CHUNK = 256
RET_GROUP = 8
EXPERT_ROWS = 512
DISPATCH_TILE = 512
COMBINE_TILE = 128
ATTN_Q_TILE = 512
NEG_INF = float("-inf")
LOG2E = math.log2(math.e)


def _cparams(sem):
    return pltpu.CompilerParams(dimension_semantics=sem, vmem_limit_bytes=VMEM_LIMIT)


def _pick_tile(n, prefs):
    for p in prefs:
        if n % p == 0:
            return p
    raise ValueError(f"no tile in {prefs} divides {n}")


def _seg_local_block(i, segs_blocks):
    val = None
    for first, per in segs_blocks:
        loc = lax.rem(i - first, per)
        val = loc if val is None else jnp.where(i >= first, loc, val)
    return val


def _proj_kernel(*refs, src_first, mode, n_q_tiles, n_qk_tiles, q_scale, k_scale):
    n_src = len(src_first)
    x_refs = refs[:n_src]
    w_ref, t0_ref, t1_ref, t2_ref, o_ref = refs[n_src:n_src + 5]
    i = pl.program_id(0)
    j = pl.program_id(1)
    if n_src == 1:
        x = x_refs[0][...]
    else:
        xb_ref = refs[n_src + 5]
        for s in range(n_src):
            hi = src_first[s + 1] if s + 1 < n_src else None
            in_src = (i >= src_first[s]) if hi is None else ((i >= src_first[s]) & (i < hi))

            @pl.when((j == 0) & in_src)
            def _(s=s):
                xb_ref[...] = x_refs[s][...].astype(BF16)
        x = xb_ref[...]
    tn = o_ref.shape[1]

    def cols(lo, width):
        return jnp.dot(x, w_ref[:, lo:lo + width], preferred_element_type=F32)

    @pl.when(j >= n_qk_tiles)
    def _plain():
        o_ref[...] = cols(0, tn).astype(o_ref.dtype)

    @pl.when(j < n_qk_tiles)
    def _rope():
        scale = jnp.where(j < n_q_tiles, q_scale, k_scale)
        if mode == "ret":
            cos = t0_ref[...]
            sin = t1_ref[...]
            half = RET_DK // 2
            for hs in range(tn // RET_DK):
                lo = hs * RET_DK
                acc = cols(lo, RET_DK)
                x1 = acc[:, :half]
                x2 = acc[:, half:]
                o_ref[:, lo:lo + half] = ((x1 * cos - x2 * sin) * scale).astype(o_ref.dtype)
                o_ref[:, lo + half:lo + RET_DK] = ((x2 * cos + x1 * sin) * scale).astype(o_ref.dtype)
        else:
            c = t0_ref[...]
            s_up = t1_ref[...]
            s_dn = t2_ref[...]
            half = ROPE_DIMS // 2
            group = 2 * DIFF_DH
            for hs in range(tn // group):
                acc = cols(hs * group, group)
                for g in range(group // DIFF_DH):
                    lo = hs * group + g * DIFF_DH
                    seg = acc[:, g * DIFF_DH:(g + 1) * DIFF_DH]
                    rot = seg * c + pltpu.roll(seg, half, 1) * s_up + pltpu.roll(seg, DIFF_DH - half, 1) * s_dn
                    o_ref[:, lo:lo + DIFF_DH] = (rot * scale).astype(o_ref.dtype)


def _row_sources(sources, tm, index_of, **spec_kwargs):
    firsts, specs, first = [], [], 0
    for src in sources:
        nblk = src.shape[0] // tm
        firsts.append(first)
        specs.append(pl.BlockSpec((tm, src.shape[1]),
                                  lambda *g, first=first, nblk=nblk: (jnp.clip(index_of(*g) - first, 0, nblk - 1), 0),
                                  **spec_kwargs))
        first += nblk
    return tuple(firsts), specs, first


def _proj(sources, w, tables, segs, *, mode, n_q_cols, n_qk_cols, q_scale, k_scale):
    k = sources[0].shape[1]
    n = w.shape[1]
    seg_len = [s for _, s in segs]
    tm = _pick_tile(math.gcd(*seg_len), (1024, 512, 256))
    tn = 1024
    segs_blocks = [(start // tm, s // tm) for start, s in segs]
    tw = tables[0].shape[1]
    tab_spec = pl.BlockSpec((tm, tw), lambda i, j: (_seg_local_block(i, segs_blocks), 0))
    kwargs = {"pipeline_mode": pl.Buffered(1)} if len(sources) > 1 else {}
    src_first, src_specs, n_row_blocks = _row_sources(sources, tm, lambda i, j: i, **kwargs)
    kern = functools.partial(_proj_kernel, src_first=src_first, mode=mode, n_q_tiles=n_q_cols // tn,
                             n_qk_tiles=n_qk_cols // tn, q_scale=q_scale, k_scale=k_scale)
    return pl.pallas_call(
        kern,
        grid=(n_row_blocks, n // tn),
        in_specs=src_specs + [pl.BlockSpec((k, tn), lambda i, j: (0, j)), tab_spec, tab_spec, tab_spec],
        out_specs=pl.BlockSpec((tm, tn), lambda i, j: (i, j)),
        out_shape=jax.ShapeDtypeStruct((n_row_blocks * tm, n), BF16),
        scratch_shapes=[pltpu.VMEM((tm, k), BF16)] if len(sources) > 1 else [],
        compiler_params=_cparams(("arbitrary", "arbitrary")),
        name=f"proj_{mode}",
    )(*sources, w, *tables)


HIGH_HALF = 0xFFFF0000


def _pack_rows(y):
    n = y.shape[1] // 2
    lo = lax.bitcast_convert_type(y[:, :n].astype(BF16).astype(F32), U32) >> 16
    hi = lax.bitcast_convert_type(y[:, n:].astype(BF16).astype(F32), U32) & jnp.uint32(HIGH_HALF)
    return lo | hi


def _unpack_rows(w):
    lo = lax.bitcast_convert_type(w << 16, F32)
    hi = lax.bitcast_convert_type(w & jnp.uint32(HIGH_HALF), F32)
    return lo, hi


PACK_WORDS = D_MODEL // 2
PACK_SUB = PACK_WORDS // LANES
assert PACK_SUB == SUBLANES


def _tile_of(row):
    start = row * PACK_SUB
    return pl.ds(start if isinstance(start, int) else pl.multiple_of(start, PACK_SUB), PACK_SUB)


def _store_packed(ref, words):
    r = words.shape[0]
    for s in range(PACK_SUB):
        ref[pl.ds(s, r, stride=PACK_SUB), :] = words[:, s * LANES:(s + 1) * LANES]


def _load_packed(ref, r):
    return jnp.concatenate([ref[pl.ds(s, r, stride=PACK_SUB), :] for s in range(PACK_SUB)], axis=1)


def _layer_norm_rows(z, g, b):
    mu = jnp.mean(z, axis=-1, keepdims=True)
    zc = z - mu
    var = jnp.mean(zc * zc, axis=-1, keepdims=True)
    return zc * lax.rsqrt(var + LN_EPS) * g + b


def _out_ln_kernel(a_ref, w_ref, g_ref, b_ref, *refs, src_first):
    n_src = len(src_first)
    x_refs = refs[:n_src]
    o_ref, op_ref = refs[n_src:]
    i = pl.program_id(0)
    h = jnp.dot(a_ref[...], w_ref[...], preferred_element_type=F32)

    def finish(x_ref):
        y = _layer_norm_rows(ALPHA * x_ref[...] + h, g_ref[...], b_ref[...])
        o_ref[...] = y
        _store_packed(op_ref, _pack_rows(y))

    if n_src == 1:
        finish(x_refs[0])
    else:
        for s in range(n_src):
            hi = src_first[s + 1] if s + 1 < n_src else None
            in_src = (i >= src_first[s]) if hi is None else ((i >= src_first[s]) & (i < hi))

            @pl.when(in_src)
            def _(s=s):
                finish(x_refs[s])


def _out_ln(a, w, x_sources, g, b, name):
    m, kk = a.shape
    d = w.shape[1]
    tm = 256
    row = pl.BlockSpec((tm, d), lambda i: (i, 0))
    vec = pl.BlockSpec((1, d), lambda i: (0, 0))
    src_first, src_specs, n_row_blocks = _row_sources(x_sources, tm, lambda i: i)
    assert n_row_blocks * tm == m
    return pl.pallas_call(
        functools.partial(_out_ln_kernel, src_first=src_first),
        grid=(m // tm,),
        in_specs=[pl.BlockSpec((tm, kk), lambda i: (i, 0)),
                  pl.BlockSpec((kk, d), lambda i: (0, 0), pipeline_mode=pl.Buffered(1)), vec, vec] + src_specs,
        out_specs=[row, pl.BlockSpec((tm * PACK_SUB, LANES), lambda i: (i, 0))],
        out_shape=[jax.ShapeDtypeStruct((m, d), F32), jax.ShapeDtypeStruct((m * PACK_SUB, LANES), U32)],
        compiler_params=_cparams(("arbitrary",)),
        name=name,
    )(a, w, g.reshape(1, d), b.reshape(1, d), *x_sources)


def _ret_kernel(lg_ref, cd_ref, q_ref, k_ref, v_ref, *rest, backward, segs_chunks):
    if backward:
        of_ref, g_ref, o_ref, state_ref, dmat_ref, qd_ref, kd_ref = rest
    else:
        o_ref, state_ref, dmat_ref, qd_ref, kd_ref = rest
    h = pl.program_id(0)
    c = pl.program_id(1)
    nc = pl.num_programs(1)
    cc = (nc - 1 - c) if backward else c
    lg = lg_ref[h]
    cs = RET_CHUNK

    @pl.when(c == 0)
    def _tables():
        ii = lax.broadcasted_iota(I32, (cs, cs), 0)
        jj = lax.broadcasted_iota(I32, (cs, cs), 1)
        if backward:
            mask = jj > ii
            dist = (jj - ii).astype(F32)
        else:
            mask = ii >= jj
            dist = (ii - jj).astype(F32)
        dmat_ref[...] = jnp.where(mask, jnp.exp(jnp.where(mask, dist, 0.0) * lg), 0.0)
        pos = lax.broadcasted_iota(I32, (cs, 1), 0).astype(F32)
        if backward:
            qd_ref[...] = jnp.exp((cs - pos) * lg)
            kd_ref[...] = jnp.exp(pos * lg)
        else:
            qd_ref[...] = jnp.exp((pos + 1.0) * lg)
            kd_ref[...] = jnp.exp((cs - 1.0 - pos) * lg)

    loc = _seg_local_block(cc, segs_chunks)
    if backward:
        per = None
        for first, p in segs_chunks:
            per = p if per is None else jnp.where(cc >= first, p, per)
        is_start = loc == per - 1
    else:
        is_start = loc == 0

    @pl.when(is_start)
    def _reset():
        state_ref[...] = jnp.zeros_like(state_ref)

    state = state_ref[...]
    n_sub = q_ref.shape[0] // cs
    for sub in (range(n_sub - 1, -1, -1) if backward else range(n_sub)):
        rows = pl.ds(sub * cs, cs)
        q = q_ref[rows, :]
        k = k_ref[rows, :]
        v = v_ref[rows, :]
        att = lax.dot_general(q, k, (((1,), (1,)), ((), ())), preferred_element_type=F32) * dmat_ref[...]
        inner = jnp.dot(att.astype(BF16), v, preferred_element_type=F32)
        cross = jnp.dot(q, state.astype(BF16), preferred_element_type=F32) * qd_ref[...]
        o = inner + cross
        kt = (k.astype(F32) * kd_ref[...]).T.astype(BF16)
        state = state * cd_ref[h] + jnp.dot(kt, v, preferred_element_type=F32)

        if backward:
            tot = of_ref[rows, :] + o
            mu = jnp.mean(tot, axis=-1, keepdims=True)
            tc = tot - mu
            var = jnp.mean(tc * tc, axis=-1, keepdims=True)
            on = tc * lax.rsqrt(var + LN_EPS)
            g = g_ref[rows, :].astype(F32)
            o_ref[rows, :] = ((g * (1.0 / (1.0 + jnp.exp(-g)))) * on).astype(o_ref.dtype)
        else:
            o_ref[rows, :] = o
    state_ref[...] = state


def _retention(proj, lg_f, lg_b, segs):
    t = proj.shape[0]
    cs = RET_CHUNK
    rb = RET_CHUNK * RET_GROUP
    nc = t // rb
    segs_chunks = [(start // rb, s // rb) for start, s in segs]
    qk_blocks = (RET_HEADS * RET_DK) // RET_DK
    v_blocks0 = (2 * RET_HEADS * RET_DK) // RET_DV
    g_blocks0 = v_blocks0 + RET_HEADS
    smem = pl.BlockSpec(memory_space=pltpu.SMEM)
    scratch = [pltpu.VMEM((RET_DK, RET_DV), F32), pltpu.VMEM((cs, cs), F32),
               pltpu.VMEM((cs, 1), F32), pltpu.VMEM((cs, 1), F32)]

    def run(backward, lg, extra):
        def cmap(c):
            return (nc - 1 - c) if backward else c
        in_specs = [smem, smem,
                    pl.BlockSpec((rb, RET_DK), lambda h, c: (cmap(c), h)),
                    pl.BlockSpec((rb, RET_DK), lambda h, c: (cmap(c), qk_blocks + h)),
                    pl.BlockSpec((rb, RET_DV), lambda h, c: (cmap(c), v_blocks0 + h))]
        args = [lg, jnp.exp(cs * lg), proj, proj, proj]
        if backward:
            in_specs += [pl.BlockSpec((rb, RET_DV), lambda h, c: (cmap(c), h)),
                         pl.BlockSpec((rb, RET_DV), lambda h, c: (cmap(c), g_blocks0 + h))]
            args += [extra, proj]
        return pl.pallas_call(
            functools.partial(_ret_kernel, backward=backward, segs_chunks=segs_chunks),
            grid=(RET_HEADS, nc),
            in_specs=in_specs,
            out_specs=pl.BlockSpec((rb, RET_DV), lambda h, c: (cmap(c), h)),
            out_shape=jax.ShapeDtypeStruct((t, RET_HEADS * RET_DV), BF16 if backward else F32),
            scratch_shapes=scratch,
            compiler_params=_cparams(("arbitrary", "arbitrary")),
            name="retention_bwd" if backward else "retention_fwd",
        )(*args)

    o_f = run(False, lg_f, None)
    return run(True, lg_b, o_f)


def _attn_kernel(lam_ref, q_ref, k_ref, v_ref, sg_ref, *rest, out_scale):
    o_ref = rest[-1]
    probs = []
    for c in range(2):
        q = q_ref[:, c * DIFF_DH:(c + 1) * DIFF_DH]
        k = k_ref[:, c * DIFF_DH:(c + 1) * DIFF_DH]
        s = lax.dot_general(q, k, (((1,), (1,)), ((), ())), preferred_element_type=F32)
        p = jnp.exp2(s - jnp.max(s, axis=-1, keepdims=True))
        probs.append((p, jnp.sum(p, axis=-1, keepdims=True)))
    (p0, den0), (p1, den1) = probs
    a = p0 * (1.0 / den0) - p1 * (lam_ref[0] / den1)
    o = jnp.dot(a.astype(BF16), v_ref[...], preferred_element_type=F32)
    ms = jnp.mean(o * o, axis=-1, keepdims=True)
    o_ref[...] = (o * lax.rsqrt(ms + LN_EPS) * sg_ref[...] * out_scale).astype(o_ref.dtype)


def _diff_attention(proj, lam, subln_g, segs, lambda_init):
    nq_blocks = DIFF_HEADS
    t = proj.shape[0]
    out = None
    for start, s, n_seq in segs:
        tq = ATTN_Q_TILE
        assert s % tq == 0 and start % s == 0
        qb0 = start // tq
        kb0 = start // s
        nqt = s // tq
        in_specs = [pl.BlockSpec(memory_space=pltpu.SMEM),
                    pl.BlockSpec((tq, 2 * DIFF_DH), lambda b, h, i: (qb0 + b * nqt + i, h)),
                    pl.BlockSpec((s, 2 * DIFF_DH), lambda b, h, i: (kb0 + b, nq_blocks + h)),
                    pl.BlockSpec((s, DIFF_DV), lambda b, h, i: (kb0 + b, 2 * nq_blocks + h)),
                    pl.BlockSpec((1, DIFF_DV), lambda b, h, i: (0, 0))]
        args = [lam, proj, proj, proj, subln_g.reshape(1, DIFF_DV)]
        aliases = {}
        if out is not None:
            in_specs.append(pl.BlockSpec(memory_space=pl.ANY))
            args.append(out)
            aliases = {len(args) - 1: 0}
        out = pl.pallas_call(
            functools.partial(_attn_kernel, out_scale=1.0 - lambda_init),
            grid=(n_seq, DIFF_HEADS, nqt),
            in_specs=in_specs,
            out_specs=pl.BlockSpec((tq, DIFF_DV), lambda b, h, i: (qb0 + b * nqt + i, h)),
            out_shape=jax.ShapeDtypeStruct((t, DIFF_HEADS * DIFF_DV), BF16),
            input_output_aliases=aliases,
            compiler_params=_cparams(("arbitrary", "arbitrary", "arbitrary")),
            name=f"diff_attn_s{s}",
        )(*args)
    return out


def _first_max(cur, rows, n_rows):
    m = jnp.max(cur, axis=0, keepdims=True)
    idx = jnp.min(jnp.where(cur == m, rows, float(n_rows)), axis=0, keepdims=True)
    return m, idx


def _router_kernel(x_ref, wt_ref, b_ref, idx_ref, gate_ref, rank_ref, cnt_ref, carry_ref):
    i = pl.program_id(0)

    @pl.when(i == 0)
    def _init():
        carry_ref[...] = jnp.zeros_like(carry_ref)

    tn = x_ref.shape[0]
    x = x_ref[...]
    xh = x.astype(BF16)
    xl = (x - xh.astype(F32)).astype(BF16)
    w = wt_ref[...]
    wh = w.astype(BF16)
    wl = (w - wh.astype(F32)).astype(BF16)

    def nt(a, b):
        return lax.dot_general(a, b, (((1,), (1,)), ((), ())), preferred_element_type=F32)

    logits = nt(wh, xh) + nt(wh, xl) + nt(wl, xh)
    scores = 1.0 / (1.0 + jnp.exp(-logits))
    biased = scores + b_ref[...]

    rows_g = lax.broadcasted_iota(I32, (GROUP_SIZE, tn), 0).astype(F32)
    rows_e = lax.broadcasted_iota(I32, (N_EXPERTS, tn), 0).astype(F32)

    gscore = []
    for g in range(N_GROUPS):
        slab = biased[g * GROUP_SIZE:(g + 1) * GROUP_SIZE]
        m1, i1 = _first_max(slab, rows_g, GROUP_SIZE)
        m2 = jnp.max(jnp.where(rows_g == i1, NEG_INF, slab), axis=0, keepdims=True)
        gscore.append(m1 + m2)
    cur = jnp.concatenate(gscore, axis=0)
    rows_grp = lax.broadcasted_iota(I32, (N_GROUPS, tn), 0).astype(F32)
    gsel = jnp.zeros((N_GROUPS, tn), F32)
    for _ in range(TOPK_GROUPS):
        _, gi = _first_max(cur, rows_grp, N_GROUPS)
        pick = rows_grp == gi
        gsel = jnp.where(pick, 1.0, gsel)
        cur = jnp.where(pick, NEG_INF, cur)

    masked = jnp.concatenate(
        [jnp.where(gsel[g:g + 1] > 0.0, biased[g * GROUP_SIZE:(g + 1) * GROUP_SIZE], NEG_INF)
         for g in range(N_GROUPS)], axis=0)

    sel = jnp.zeros((N_EXPERTS, tn), F32)
    picks = []
    cur = masked
    for _ in range(TOP_K):
        _, ei = _first_max(cur, rows_e, N_EXPERTS)
        pick = rows_e == ei
        sel = jnp.where(pick, 1.0, sel)
        cur = jnp.where(pick, NEG_INF, cur)
        picks.append(ei)

    gsum = jnp.sum(scores * sel, axis=0, keepdims=True)

    ta = lax.broadcasted_iota(I32, (tn, tn), 0)
    tb = lax.broadcasted_iota(I32, (tn, tn), 1)
    before = jnp.where(ta < tb, 1.0, 0.0).astype(BF16)
    rank = jnp.dot(sel.astype(BF16), before, preferred_element_type=F32) + carry_ref[...]
    carry_ref[...] = carry_ref[...] + jnp.sum(sel, axis=1, keepdims=True)
    cnt_ref[...] = carry_ref[...]

    gates = []
    ranks = []
    for ei in picks:
        hit = rows_e == ei
        gates.append(jnp.sum(jnp.where(hit, scores, 0.0), axis=0, keepdims=True) / gsum * ROUTED_SCALE)
        ranks.append(jnp.sum(jnp.where(hit, rank, 0.0), axis=0, keepdims=True))
    idx_ref[...] = jnp.concatenate(picks, axis=0).astype(I32)
    gate_ref[...] = jnp.concatenate(gates, axis=0)
    rank_ref[...] = jnp.concatenate(ranks, axis=0).astype(I32)


def _router(x, w_router, b_router):
    t, d = x.shape
    tn = _pick_tile(t, (512, 256))
    kt = pl.BlockSpec((TOP_K, tn), lambda i: (0, i))
    return pl.pallas_call(
        _router_kernel,
        grid=(t // tn,),
        in_specs=[pl.BlockSpec((tn, d), lambda i: (i, 0)),
                  pl.BlockSpec((N_EXPERTS, d), lambda i: (0, 0)),
                  pl.BlockSpec((N_EXPERTS, 1), lambda i: (0, 0))],
        out_specs=[kt, kt, kt, pl.BlockSpec((N_EXPERTS, 1), lambda i: (0, 0))],
        out_shape=[jax.ShapeDtypeStruct((TOP_K, t), I32), jax.ShapeDtypeStruct((TOP_K, t), F32),
                   jax.ShapeDtypeStruct((TOP_K, t), I32), jax.ShapeDtypeStruct((N_EXPERTS, 1), F32)],
        scratch_shapes=[pltpu.VMEM((N_EXPERTS, 1), F32)],
        compiler_params=_cparams(("arbitrary",)),
        name="router",
    )(x, w_router.T, b_router.reshape(N_EXPERTS, 1))


def _slot_kernel(idx_ref, rank_ref, start_ref, slot_ref):
    tn = idx_ref.shape[1]
    rows_e = lax.broadcasted_iota(I32, (N_EXPERTS, tn), 0)
    out = []
    for k in range(TOP_K):
        hit = rows_e == idx_ref[k:k + 1, :]
        base = jnp.sum(jnp.where(hit, start_ref[...], 0.0), axis=0, keepdims=True)
        out.append(base.astype(I32) + rank_ref[k:k + 1, :])
    slot_ref[...] = jnp.concatenate(out, axis=0)


def _slots(idx_t, rank_t, pad_start):
    t = idx_t.shape[1]
    tn = _pick_tile(t, (512, 256))
    kt = pl.BlockSpec((TOP_K, tn), lambda i: (0, i))
    return pl.pallas_call(
        _slot_kernel,
        grid=(t // tn,),
        in_specs=[kt, kt, pl.BlockSpec((N_EXPERTS, 1), lambda i: (0, 0))],
        out_specs=kt,
        out_shape=jax.ShapeDtypeStruct((TOP_K, t), I32),
        compiler_params=_cparams(("arbitrary",)),
        name="slots",
    )(idx_t, rank_t, pad_start.astype(F32).reshape(N_EXPERTS, 1))


def _row_copy(src, src_row, dst, dst_row, sem):
    return pltpu.make_async_copy(src.at[_tile_of(src_row)], dst.at[_tile_of(dst_row)], sem)


def _dispatch_kernel(pad_end_ref, padded_ref, slot_hbm, x_ref, xs_hbm, slot_smem, zero_ref, sem, slot_sem):
    i = pl.program_id(0)
    tt = x_ref.shape[0] // PACK_SUB
    bm = zero_ref.shape[0] // PACK_SUB
    slot_cp = pltpu.make_async_copy(slot_hbm.at[i], slot_smem, slot_sem)
    slot_cp.start()

    @pl.when(i == 0)
    def _clear():
        zero_ref[...] = jnp.zeros_like(zero_ref)

        def pad_copy(e):
            off = pl.multiple_of((pad_end_ref[e] - bm) * PACK_SUB, bm * PACK_SUB)
            return pltpu.make_async_copy(zero_ref, xs_hbm.at[pl.ds(off, bm * PACK_SUB)], sem)

        def start(e, carry):
            @pl.when(padded_ref[e] > 0)
            def _():
                pad_copy(e).start()
            return carry

        def wait(e, carry):
            @pl.when(padded_ref[e] > 0)
            def _():
                pad_copy(e).wait()
            return carry

        lax.fori_loop(0, N_EXPERTS, start, 0)
        lax.fori_loop(0, N_EXPERTS, wait, 0)

    slot_cp.wait()
    per_row = LANES // TOP_K

    def issue(r, carry):
        for j in range(LANES):
            _row_copy(x_ref, r * per_row + j // TOP_K, xs_hbm, slot_smem[r, j], sem).start(priority=j % 2)
        return carry

    lax.fori_loop(0, tt // per_row, issue, 0)

    def drain(t, carry):
        for k in range(TOP_K):
            _row_copy(x_ref, t, xs_hbm, 0, sem).wait()
        return carry

    lax.fori_loop(0, tt, drain, 0)


def _dispatch(xp, slot_tiles, pad_end, padded, n_slots):
    t = xp.shape[0] // PACK_SUB
    tt = DISPATCH_TILE
    grid_spec = pltpu.PrefetchScalarGridSpec(
        num_scalar_prefetch=2,
        grid=(t // tt,),
        in_specs=[pl.BlockSpec(memory_space=pl.ANY),
                  pl.BlockSpec((tt * PACK_SUB, LANES), lambda i, pe, pd: (i, 0))],
        out_specs=pl.BlockSpec(memory_space=pl.ANY),
        scratch_shapes=[pltpu.SMEM((tt * TOP_K // LANES, LANES), I32),
                        pltpu.VMEM((EXPERT_ROWS * PACK_SUB, LANES), U32),
                        pltpu.SemaphoreType.DMA, pltpu.SemaphoreType.DMA],
    )
    return pl.pallas_call(
        _dispatch_kernel,
        grid_spec=grid_spec,
        out_shape=jax.ShapeDtypeStruct((n_slots * PACK_SUB, LANES), U32),
        compiler_params=_cparams(("arbitrary",)),
        name="dispatch",
    )(pad_end, padded, slot_tiles, xp)


def _ffn_kernel(block_e_ref, n_used_ref, x_ref, wgu_ref, wd_ref, o_ref, wgu_bf, wd_bf, *, pack_out):
    b = pl.program_id(0)

    @pl.when(b < n_used_ref[0])
    def _():
        e = block_e_ref[b]
        e_prev = block_e_ref[jnp.maximum(b - 1, 0)]

        @pl.when((b == 0) | (e != e_prev))
        def _():
            wgu_bf[...] = wgu_ref[0, 0].astype(BF16)
            wd_bf[...] = wd_ref[0, 0].astype(BF16)

        hidden = wd_bf.shape[0]
        rows = x_ref.shape[0] // PACK_SUB
        lo, hi = _unpack_rows(_load_packed(x_ref, rows))
        gu = (jnp.dot(lo.astype(BF16), wgu_bf[:PACK_WORDS], preferred_element_type=F32)
              + jnp.dot(hi.astype(BF16), wgu_bf[PACK_WORDS:], preferred_element_type=F32))
        gate = gu[:, :hidden]
        act = gate * (1.0 / (1.0 + jnp.exp(-gate))) * gu[:, hidden:]
        y = jnp.dot(act.astype(BF16), wd_bf[...], preferred_element_type=F32)
        if pack_out:
            _store_packed(o_ref, _pack_rows(y))
        else:
            o_ref[...] = y


def _ffn(xp, w_gu, w_down, layer, block_e, n_used, rows, pack_out, name):
    n = xp.shape[0] // PACK_SUB
    d = 2 * PACK_WORDS
    hidden = w_down.shape[2]

    def row_map(b, be, nu):
        return (jnp.minimum(b, nu[0] - 1), 0)

    packed_rows = pl.BlockSpec((rows * PACK_SUB, LANES), row_map)
    if pack_out:
        out_spec, out_shape = packed_rows, jax.ShapeDtypeStruct((n * PACK_SUB, LANES), U32)
    else:
        out_spec, out_shape = pl.BlockSpec((rows, d), row_map), jax.ShapeDtypeStruct((n, d), F32)
    grid_spec = pltpu.PrefetchScalarGridSpec(
        num_scalar_prefetch=2,
        grid=(n // rows,),
        in_specs=[packed_rows,
                  pl.BlockSpec((1, 1, d, 2 * hidden), lambda b, be, nu: (layer, be[b], 0, 0)),
                  pl.BlockSpec((1, 1, hidden, d), lambda b, be, nu: (layer, be[b], 0, 0))],
        out_specs=out_spec,
        scratch_shapes=[pltpu.VMEM((d, 2 * hidden), BF16), pltpu.VMEM((hidden, d), BF16)],
    )
    return pl.pallas_call(
        functools.partial(_ffn_kernel, pack_out=pack_out),
        grid_spec=grid_spec,
        out_shape=out_shape,
        compiler_params=_cparams(("arbitrary",)),
        name=name,
    )(block_e, n_used, xp, w_gu, w_down)


def _combine_kernel(slot_hbm, x_ref, sh_ref, gate_ref, g_ref, b_ref, ys_hbm, o_ref, *rest, tile0, with_bf16):
    if with_bf16:
        ob_ref, *rest = rest
    slots0, slots1, buf0, buf1, sems, slot_sem = rest
    slot_smem = (slots0, slots1)
    bufs = (buf0, buf1)
    i = pl.program_id(0)
    n = pl.num_programs(0)
    tt = x_ref.shape[0]
    per_row = LANES // TOP_K
    groups = tt // per_row

    def load_slots(tile, par):
        slot_cp = pltpu.make_async_copy(slot_hbm.at[tile0 + tile], slot_smem[par], slot_sem)
        slot_cp.start()
        slot_cp.wait()

    def issue_group(r, par):
        for j in range(LANES):
            _row_copy(ys_hbm, slot_smem[par][r, j], bufs[par].at[j % TOP_K], r * per_row + j // TOP_K,
                      sems.at[par]).start(priority=j % 2)

    def drain(par):
        def body(t, carry):
            for k in range(TOP_K):
                _row_copy(ys_hbm, 0, bufs[par].at[k], t, sems.at[par]).wait()
            return carry
        lax.fori_loop(0, tt, body, 0)

    def reduce_group(r, par):
        rows = slice(r * per_row, (r + 1) * per_row)
        sh = sh_ref[rows, :]
        h_lo = sh[:, :PACK_WORDS]
        h_hi = sh[:, PACK_WORDS:]
        gate = gate_ref[rows, :]
        for k in range(TOP_K):
            words = jnp.concatenate(
                [bufs[par][k, pl.ds(r * per_row * PACK_SUB + s, per_row, stride=PACK_SUB), :]
                 for s in range(PACK_SUB)], axis=1)
            lo, hi = _unpack_rows(words)
            h_lo = h_lo + lo * gate[:, k:k + 1]
            h_hi = h_hi + hi * gate[:, k:k + 1]
        h = jnp.concatenate([h_lo, h_hi], axis=1)
        y = _layer_norm_rows(ALPHA * x_ref[rows, :] + h, g_ref[...], b_ref[...])
        o_ref[rows, :] = y
        if with_bf16:
            ob_ref[rows, :] = y.astype(BF16)

    @pl.when(i == 0)
    def _():
        load_slots(0, 0)
        for r in range(groups):
            issue_group(r, 0)

    nxt = jnp.minimum(i + 1, n - 1)
    for par in range(2):
        @pl.when(i % 2 == par)
        def _(par=par):
            drain(par)
            load_slots(nxt, 1 - par)
            for r in range(groups):
                issue_group(r, 1 - par)
                reduce_group(r, par)

            @pl.when(i == n - 1)
            def _():
                drain(1 - par)


def _combine(x, shared, ys, slot_tiles, gate, g, b, row_range, with_bf16):
    d = x.shape[1]
    tt = COMBINE_TILE
    r0, r1 = row_range
    tile0 = r0 // tt
    row_in = pl.BlockSpec((tt, d), lambda i: (tile0 + i, 0))
    row_out = pl.BlockSpec((tt, d), lambda i: (i, 0))
    vec = pl.BlockSpec((1, d), lambda i: (0, 0))
    out_specs = [row_out]
    out_shape = [jax.ShapeDtypeStruct((r1 - r0, d), F32)]
    if with_bf16:
        out_specs.append(row_out)
        out_shape.append(jax.ShapeDtypeStruct((r1 - r0, d), BF16))
    return pl.pallas_call(
        functools.partial(_combine_kernel, tile0=tile0, with_bf16=with_bf16),
        grid=((r1 - r0) // tt,),
        in_specs=[pl.BlockSpec(memory_space=pl.ANY), row_in, row_in,
                  pl.BlockSpec((tt, TOP_K), lambda i: (tile0 + i, 0)), vec, vec,
                  pl.BlockSpec(memory_space=pl.ANY)],
        out_specs=out_specs,
        out_shape=out_shape,
        scratch_shapes=[pltpu.SMEM((tt * TOP_K // LANES, LANES), I32), pltpu.SMEM((tt * TOP_K // LANES, LANES), I32),
                        pltpu.VMEM((TOP_K, tt * PACK_SUB, LANES), U32), pltpu.VMEM((TOP_K, tt * PACK_SUB, LANES), U32),
                        pltpu.SemaphoreType.DMA((2,)), pltpu.SemaphoreType.DMA],
        compiler_params=_cparams(("arbitrary",)),
        name="combine",
    )(slot_tiles, x, shared, gate, g.reshape(1, d), b.reshape(1, d), ys)


def _moe_layer(x, xp, layer, w_router, b_router, w_gu, w_down, sw_gu, sw_down, ln_g, ln_b, split_rows):
    t, d = x.shape
    bm = EXPERT_ROWS
    idx_t, gate_t, rank_t, counts = _router(x, w_router, b_router)

    counts = counts.reshape(N_EXPERTS).astype(I32)
    padded = (counts + bm - 1) // bm * bm
    pad_end = jnp.cumsum(padded)
    pad_start = pad_end - padded
    n_blocks = (t * TOP_K + N_EXPERTS * (bm - 1) + bm - 1) // bm
    n_slots = n_blocks * bm
    block_first = jnp.arange(n_blocks, dtype=I32) * bm
    block_e = jnp.minimum(jnp.sum((pad_end[None, :] <= block_first[:, None]).astype(I32), axis=1), N_EXPERTS - 1)
    n_used = (pad_end[-1:] // bm).astype(I32)

    slot_t = _slots(idx_t, rank_t, pad_start)
    slot_flat = slot_t.T.reshape(t * TOP_K // LANES, LANES)

    def slot_tiles(tile):
        return slot_flat.reshape(t // tile, tile * TOP_K // LANES, LANES)

    xs = _dispatch(xp, slot_tiles(DISPATCH_TILE), pad_end.astype(I32), padded.astype(I32), n_slots)
    ys = _ffn(xs, w_gu, w_down, layer, block_e, n_used, bm, True, "expert_ffn")
    shared_rows = _pick_tile(t, (512, 256))
    shared = _ffn(xp, sw_gu[:, None], sw_down[:, None], layer, jnp.zeros((t // shared_rows,), I32),
                  jnp.full((1,), t // shared_rows, I32), shared_rows, False, "shared_ffn")
    args = (x, shared, ys, slot_tiles(COMBINE_TILE), gate_t.T, ln_g, ln_b)
    if split_rows is None:
        return _combine(*args, (0, t), True)
    return _combine(*args, (0, split_rows), False)[0], _combine(*args, (split_rows, t), False)[0]


def _rope_angles(s, theta, half):
    inv = theta ** (-jnp.arange(half, dtype=F32) / half)
    ang = jnp.arange(s, dtype=F32)[:, None] * inv[None, :]
    return jnp.cos(ang), jnp.sin(ang)


def _ret_tables(s):
    cos, sin = _rope_angles(s, RET_THETA, RET_DK // 2)
    return cos, sin, sin


def _diff_tables(s):
    half = ROPE_DIMS // 2
    cos, sin = _rope_angles(s, ROPE_THETA, half)
    rest = DIFF_DH - ROPE_DIMS
    zeros_h = jnp.zeros((s, half), F32)
    c = jnp.concatenate([cos, cos, jnp.ones((s, rest), F32)], axis=1)
    s_up = jnp.concatenate([zeros_h, sin, jnp.zeros((s, rest), F32)], axis=1)
    s_dn = jnp.concatenate([-sin, zeros_h, jnp.zeros((s, rest), F32)], axis=1)
    return c, s_up, s_dn


def kernel(x_prompt, x_sample, ret_w_in, ret_decay_f, ret_decay_b, ret_w_out, diff_w_in, diff_lam_q1, diff_lam_k1,
           diff_lam_q2, diff_lam_k2, diff_subln_g, diff_w_out, ln_mix_g, ln_mix_b, router_w, router_b, exp_w_gu,
           exp_w_down, shared_w_gu, shared_w_down, ln_ffn_g, ln_ffn_b):
    bp, sp, d = x_prompt.shape
    bs, ss, _ = x_sample.shape
    tp = bp * sp
    x_rows = [x_prompt.reshape(tp, d), x_sample.reshape(bs * ss, d)]
    proj_rows = x_rows
    segs = [(0, sp), (tp, ss)]
    s_max = max(sp, ss)

    for i in range(DEPTH):
        j = i // 2
        if i % 2 == 0:
            qk = RET_HEADS * RET_DK
            proj = _proj(proj_rows, ret_w_in[j].astype(BF16), _ret_tables(s_max), segs, mode="ret",
                         n_q_cols=qk, n_qk_cols=2 * qk, q_scale=1.0, k_scale=RET_DK ** -0.5)
            lg_f = -jax.nn.softplus(-ret_decay_f[j].astype(F32))
            lg_b = -jax.nn.softplus(-ret_decay_b[j].astype(F32))
            mixed = _retention(proj, lg_f, lg_b, segs)
            x, xp = _out_ln(mixed, ret_w_out[j].astype(BF16), x_rows, ln_mix_g[i], ln_mix_b[i], "ret_out")
        else:
            lambda_init = 0.8 - 0.6 * math.exp(-0.3 * i)
            nq = DIFF_HEADS * 2 * DIFF_DH
            proj = _proj(proj_rows, diff_w_in[j].astype(BF16), _diff_tables(s_max), segs, mode="diff",
                         n_q_cols=nq, n_qk_cols=2 * nq, q_scale=DIFF_DH ** -0.5 * LOG2E, k_scale=1.0)
            lam = (jnp.exp(jnp.sum(diff_lam_q1[j].astype(F32) * diff_lam_k1[j].astype(F32)))
                   - jnp.exp(jnp.sum(diff_lam_q2[j].astype(F32) * diff_lam_k2[j].astype(F32))) + lambda_init)
            mixed = _diff_attention(proj, lam.reshape(1), diff_subln_g[j],
                                    [(0, sp, bp), (tp, ss, bs)], lambda_init)
            x, xp = _out_ln(mixed, diff_w_out[j].astype(BF16), x_rows, ln_mix_g[i], ln_mix_b[i], "diff_out")
        last = i == DEPTH - 1
        x, xb = _moe_layer(x, xp, i, router_w[i], router_b[i], exp_w_gu, exp_w_down, shared_w_gu, shared_w_down,
                           ln_ffn_g[i], ln_ffn_b[i], tp if last else None)
        x_rows, proj_rows = [x], [xb]
    return x.reshape(bp, sp, d), xb.reshape(bs, ss, d)
```

```python
import functools
import math

import jax
import jax.numpy as jnp
from jax import lax
from jax.experimental import pallas as pl
from jax.experimental.pallas import tpu as pltpu

F32 = jnp.float32
BF16 = jnp.bfloat16
I32 = jnp.int32
U32 = jnp.uint32

D_MODEL = 2048
DEPTH = 2
ALPHA = (2 * DEPTH) ** 0.25
LN_EPS = 1e-5
RET_HEADS = 8
RET_DK = D_MODEL // RET_HEADS
RET_DV = 2 * RET_DK
RET_THETA = 10000.0
DIFF_HEADS = 8
DIFF_DH = D_MODEL // (2 * DIFF_HEADS)
DIFF_DV = 2 * DIFF_DH
ROPE_THETA = 500000.0
ROPE_DIMS = DIFF_DH // 4
N_EXPERTS = 64
TOP_K = 8
N_GROUPS = 8
TOPK_GROUPS = 4
GROUP_SIZE = N_EXPERTS // N_GROUPS
EXPERT_HIDDEN = 512
SHARED_HIDDEN = 512
ROUTED_SCALE = 2.5

LANES = 128
SUBLANES = 8
VMEM_LIMIT = 48 * 1024 * 1024

RET_CHUNK = 256
RET_GROUP = 8
EXPERT_ROWS = 512
DISPATCH_TILE = 512
COMBINE_TILE = 128
ATTN_Q_TILE = 512
NEG_INF = float("-inf")
LOG2E = math.log2(math.e)


def _cparams(sem):
    return pltpu.CompilerParams(dimension_semantics=sem, vmem_limit_bytes=VMEM_LIMIT)


def _pick_tile(n, prefs):
    for p in prefs:
        if n % p == 0:
            return p
    raise ValueError(f"no tile in {prefs} divides {n}")


def _seg_local_block(i, segs_blocks):
    val = None
    for first, per in segs_blocks:
        loc = lax.rem(i - first, per)
        val = loc if val is None else jnp.where(i >= first, loc, val)
    return val


def _proj_kernel(*refs, src_first, mode, n_q_tiles, n_qk_tiles, q_scale, k_scale):
    n_src = len(src_first)
    x_refs = refs[:n_src]
    w_ref, t0_ref, t1_ref, t2_ref, o_ref = refs[n_src:n_src + 5]
    i = pl.program_id(0)
    j = pl.program_id(1)
    if n_src == 1:
        x = x_refs[0][...]
    else:
        xb_ref = refs[n_src + 5]
        for s in range(n_src):
            hi = src_first[s + 1] if s + 1 < n_src else None
            in_src = (i >= src_first[s]) if hi is None else ((i >= src_first[s]) & (i < hi))

            @pl.when((j == 0) & in_src)
            def _(s=s):
                xb_ref[...] = x_refs[s][...].astype(BF16)
        x = xb_ref[...]
    tn = o_ref.shape[1]

    def cols(lo, width):
        return jnp.dot(x, w_ref[:, lo:lo + width], preferred_element_type=F32)

    @pl.when(j >= n_qk_tiles)
    def _plain():
        o_ref[...] = cols(0, tn).astype(o_ref.dtype)

    @pl.when(j < n_qk_tiles)
    def _rope():
        scale = jnp.where(j < n_q_tiles, q_scale, k_scale)
        if mode == "ret":
            cos = t0_ref[...]
            sin = t1_ref[...]
            half = RET_DK // 2
            for hs in range(tn // RET_DK):
                lo = hs * RET_DK
                acc = cols(lo, RET_DK)
                x1 = acc[:, :half]
                x2 = acc[:, half:]
                o_ref[:, lo:lo + half] = ((x1 * cos - x2 * sin) * scale).astype(o_ref.dtype)
                o_ref[:, lo + half:lo + RET_DK] = ((x2 * cos + x1 * sin) * scale).astype(o_ref.dtype)
        else:
            c = t0_ref[...]
            s_up = t1_ref[...]
            s_dn = t2_ref[...]
            half = ROPE_DIMS // 2
            group = 2 * DIFF_DH
            for hs in range(tn // group):
                acc = cols(hs * group, group)
                for g in range(group // DIFF_DH):
                    lo = hs * group + g * DIFF_DH
                    seg = acc[:, g * DIFF_DH:(g + 1) * DIFF_DH]
                    rot = seg * c + pltpu.roll(seg, half, 1) * s_up + pltpu.roll(seg, DIFF_DH - half, 1) * s_dn
                    o_ref[:, lo:lo + DIFF_DH] = (rot * scale).astype(o_ref.dtype)


def _row_sources(sources, tm, index_of, **spec_kwargs):
    firsts, specs, first = [], [], 0
    for src in sources:
        nblk = src.shape[0] // tm
        firsts.append(first)
        specs.append(pl.BlockSpec((tm, src.shape[1]),
                                  lambda *g, first=first, nblk=nblk: (jnp.clip(index_of(*g) - first, 0, nblk - 1), 0),
                                  **spec_kwargs))
        first += nblk
    return tuple(firsts), specs, first


def _proj(sources, w, tables, segs, *, mode, n_q_cols, n_qk_cols, q_scale, k_scale):
    k = sources[0].shape[1]
    n = w.shape[1]
    seg_len = [s for _, s in segs]
    tm = _pick_tile(math.gcd(*seg_len), (1024, 512, 256))
    tn = 1024
    segs_blocks = [(start // tm, s // tm) for start, s in segs]
    tw = tables[0].shape[1]
    tab_spec = pl.BlockSpec((tm, tw), lambda i, j: (_seg_local_block(i, segs_blocks), 0))
    kwargs = {"pipeline_mode": pl.Buffered(1)} if len(sources) > 1 else {}
    src_first, src_specs, n_row_blocks = _row_sources(sources, tm, lambda i, j: i, **kwargs)
    kern = functools.partial(_proj_kernel, src_first=src_first, mode=mode, n_q_tiles=n_q_cols // tn,
                             n_qk_tiles=n_qk_cols // tn, q_scale=q_scale, k_scale=k_scale)
    return pl.pallas_call(
        kern,
        grid=(n_row_blocks, n // tn),
        in_specs=src_specs + [pl.BlockSpec((k, tn), lambda i, j: (0, j)), tab_spec, tab_spec, tab_spec],
        out_specs=pl.BlockSpec((tm, tn), lambda i, j: (i, j)),
        out_shape=jax.ShapeDtypeStruct((n_row_blocks * tm, n), BF16),
        scratch_shapes=[pltpu.VMEM((tm, k), BF16)] if len(sources) > 1 else [],
        compiler_params=_cparams(("arbitrary", "arbitrary")),
        name=f"proj_{mode}",
    )(*sources, w, *tables)


HIGH_HALF = 0xFFFF0000


def _pack_rows(y):
    n = y.shape[1] // 2
    lo = lax.bitcast_convert_type(y[:, :n].astype(BF16).astype(F32), U32) >> 16
    hi = lax.bitcast_convert_type(y[:, n:].astype(BF16).astype(F32), U32) & jnp.uint32(HIGH_HALF)
    return lo | hi


def _unpack_rows(w):
    lo = lax.bitcast_convert_type(w << 16, F32)
    hi = lax.bitcast_convert_type(w & jnp.uint32(HIGH_HALF), F32)
    return lo, hi


PACK_WORDS = D_MODEL // 2
PACK_SUB = PACK_WORDS // LANES
assert PACK_SUB == SUBLANES


def _tile_of(row):
    start = row * PACK_SUB
    return pl.ds(start if isinstance(start, int) else pl.multiple_of(start, PACK_SUB), PACK_SUB)


def _store_packed(ref, words):
    r = words.shape[0]
    for s in range(PACK_SUB):
        ref[pl.ds(s, r, stride=PACK_SUB), :] = words[:, s * LANES:(s + 1) * LANES]


def _load_packed(ref, r):
    return jnp.concatenate([ref[pl.ds(s, r, stride=PACK_SUB), :] for s in range(PACK_SUB)], axis=1)


def _layer_norm_rows(z, g, b):
    mu = jnp.mean(z, axis=-1, keepdims=True)
    zc = z - mu
    var = jnp.mean(zc * zc, axis=-1, keepdims=True)
    return zc * lax.rsqrt(var + LN_EPS) * g + b


def _out_ln_kernel(a_ref, w_ref, g_ref, b_ref, *refs, src_first):
    n_src = len(src_first)
    x_refs = refs[:n_src]
    o_ref, op_ref = refs[n_src:]
    i = pl.program_id(0)
    h = jnp.dot(a_ref[...], w_ref[...], preferred_element_type=F32)

    def finish(x_ref):
        y = _layer_norm_rows(ALPHA * x_ref[...] + h, g_ref[...], b_ref[...])
        o_ref[...] = y
        _store_packed(op_ref, _pack_rows(y))

    if n_src == 1:
        finish(x_refs[0])
    else:
        for s in range(n_src):
            hi = src_first[s + 1] if s + 1 < n_src else None
            in_src = (i >= src_first[s]) if hi is None else ((i >= src_first[s]) & (i < hi))

            @pl.when(in_src)
            def _(s=s):
                finish(x_refs[s])


def _out_ln(a, w, x_sources, g, b, name):
    m, kk = a.shape
    d = w.shape[1]
    tm = 256
    row = pl.BlockSpec((tm, d), lambda i: (i, 0))
    vec = pl.BlockSpec((1, d), lambda i: (0, 0))
    src_first, src_specs, n_row_blocks = _row_sources(x_sources, tm, lambda i: i)
    assert n_row_blocks * tm == m
    return pl.pallas_call(
        functools.partial(_out_ln_kernel, src_first=src_first),
        grid=(m // tm,),
        in_specs=[pl.BlockSpec((tm, kk), lambda i: (i, 0)),
                  pl.BlockSpec((kk, d), lambda i: (0, 0), pipeline_mode=pl.Buffered(1)), vec, vec] + src_specs,
        out_specs=[row, pl.BlockSpec((tm * PACK_SUB, LANES), lambda i: (i, 0))],
        out_shape=[jax.ShapeDtypeStruct((m, d), F32), jax.ShapeDtypeStruct((m * PACK_SUB, LANES), U32)],
        compiler_params=_cparams(("arbitrary",)),
        name=name,
    )(a, w, g.reshape(1, d), b.reshape(1, d), *x_sources)


def _ret_kernel(lg_ref, cd_ref, q_ref, k_ref, v_ref, *rest, backward, segs_chunks):
    if backward:
        of_ref, g_ref, o_ref, state_ref, dmat_ref, qd_ref, kd_ref = rest
    else:
        o_ref, state_ref, dmat_ref, qd_ref, kd_ref = rest
    h = pl.program_id(0)
    c = pl.program_id(1)
    nc = pl.num_programs(1)
    cc = (nc - 1 - c) if backward else c
    lg = lg_ref[h]
    cs = RET_CHUNK

    @pl.when(c == 0)
    def _tables():
        ii = lax.broadcasted_iota(I32, (cs, cs), 0)
        jj = lax.broadcasted_iota(I32, (cs, cs), 1)
        if backward:
            mask = jj > ii
            dist = (jj - ii).astype(F32)
        else:
            mask = ii >= jj
            dist = (ii - jj).astype(F32)
        dmat_ref[...] = jnp.where(mask, jnp.exp(jnp.where(mask, dist, 0.0) * lg), 0.0)
        pos = lax.broadcasted_iota(I32, (cs, 1), 0).astype(F32)
        if backward:
            qd_ref[...] = jnp.exp((cs - pos) * lg)
            kd_ref[...] = jnp.exp(pos * lg)
        else:
            qd_ref[...] = jnp.exp((pos + 1.0) * lg)
            kd_ref[...] = jnp.exp((cs - 1.0 - pos) * lg)

    loc = _seg_local_block(cc, segs_chunks)
    if backward:
        per = None
        for first, p in segs_chunks:
            per = p if per is None else jnp.where(cc >= first, p, per)
        is_start = loc == per - 1
    else:
        is_start = loc == 0

    @pl.when(is_start)
    def _reset():
        state_ref[...] = jnp.zeros_like(state_ref)

    state = state_ref[...]
    n_sub = q_ref.shape[0] // cs
    for sub in (range(n_sub - 1, -1, -1) if backward else range(n_sub)):
        rows = pl.ds(sub * cs, cs)
        q = q_ref[rows, :]
        k = k_ref[rows, :]
        v = v_ref[rows, :]
        att = lax.dot_general(q, k, (((1,), (1,)), ((), ())), preferred_element_type=F32) * dmat_ref[...]
        inner = jnp.dot(att.astype(BF16), v, preferred_element_type=F32)
        cross = jnp.dot(q, state.astype(BF16), preferred_element_type=F32) * qd_ref[...]
        o = inner + cross
        kt = (k.astype(F32) * kd_ref[...]).T.astype(BF16)
        state = state * cd_ref[h] + jnp.dot(kt, v, preferred_element_type=F32)

        if backward:
            tot = of_ref[rows, :] + o
            mu = jnp.mean(tot, axis=-1, keepdims=True)
            tc = tot - mu
            var = jnp.mean(tc * tc, axis=-1, keepdims=True)
            on = tc * lax.rsqrt(var + LN_EPS)
            g = g_ref[rows, :].astype(F32)
            o_ref[rows, :] = ((g * (1.0 / (1.0 + jnp.exp(-g)))) * on).astype(o_ref.dtype)
        else:
            o_ref[rows, :] = o
    state_ref[...] = state


def _retention(proj, lg_f, lg_b, segs):
    t = proj.shape[0]
    cs = RET_CHUNK
    rb = RET_CHUNK * RET_GROUP
    nc = t // rb
    segs_chunks = [(start // rb, s // rb) for start, s in segs]
    qk_blocks = (RET_HEADS * RET_DK) // RET_DK
    v_blocks0 = (2 * RET_HEADS * RET_DK) // RET_DV
    g_blocks0 = v_blocks0 + RET_HEADS
    smem = pl.BlockSpec(memory_space=pltpu.SMEM)
    scratch = [pltpu.VMEM((RET_DK, RET_DV), F32), pltpu.VMEM((cs, cs), F32),
               pltpu.VMEM((cs, 1), F32), pltpu.VMEM((cs, 1), F32)]

    def run(backward, lg, extra):
        def cmap(c):
            return (nc - 1 - c) if backward else c
        in_specs = [smem, smem,
                    pl.BlockSpec((rb, RET_DK), lambda h, c: (cmap(c), h)),
                    pl.BlockSpec((rb, RET_DK), lambda h, c: (cmap(c), qk_blocks + h)),
                    pl.BlockSpec((rb, RET_DV), lambda h, c: (cmap(c), v_blocks0 + h))]
        args = [lg, jnp.exp(cs * lg), proj, proj, proj]
        if backward:
            in_specs += [pl.BlockSpec((rb, RET_DV), lambda h, c: (cmap(c), h)),
                         pl.BlockSpec((rb, RET_DV), lambda h, c: (cmap(c), g_blocks0 + h))]
            args += [extra, proj]
        return pl.pallas_call(
            functools.partial(_ret_kernel, backward=backward, segs_chunks=segs_chunks),
            grid=(RET_HEADS, nc),
            in_specs=in_specs,
            out_specs=pl.BlockSpec((rb, RET_DV), lambda h, c: (cmap(c), h)),
            out_shape=jax.ShapeDtypeStruct((t, RET_HEADS * RET_DV), BF16 if backward else F32),
            scratch_shapes=scratch,
            compiler_params=_cparams(("arbitrary", "arbitrary")),
            name="retention_bwd" if backward else "retention_fwd",
        )(*args)

    o_f = run(False, lg_f, None)
    return run(True, lg_b, o_f)


def _attn_kernel(lam_ref, q_ref, k_ref, v_ref, sg_ref, *rest, out_scale):
    o_ref = rest[-1]
    probs = []
    for c in range(2):
        q = q_ref[:, c * DIFF_DH:(c + 1) * DIFF_DH]
        k = k_ref[:, c * DIFF_DH:(c + 1) * DIFF_DH]
        s = lax.dot_general(q, k, (((1,), (1,)), ((), ())), preferred_element_type=F32)
        p = jnp.exp2(s - jnp.max(s, axis=-1, keepdims=True))
        probs.append((p, jnp.sum(p, axis=-1, keepdims=True)))
    (p0, den0), (p1, den1) = probs
    a = p0 * (1.0 / den0) - p1 * (lam_ref[0] / den1)
    o = jnp.dot(a.astype(BF16), v_ref[...], preferred_element_type=F32)
    ms = jnp.mean(o * o, axis=-1, keepdims=True)
    o_ref[...] = (o * lax.rsqrt(ms + LN_EPS) * sg_ref[...] * out_scale).astype(o_ref.dtype)


def _diff_attention(proj, lam, subln_g, segs, lambda_init):
    nq_blocks = DIFF_HEADS
    t = proj.shape[0]
    out = None
    for start, s, n_seq in segs:
        tq = ATTN_Q_TILE
        assert s % tq == 0 and start % s == 0
        qb0 = start // tq
        kb0 = start // s
        nqt = s // tq
        in_specs = [pl.BlockSpec(memory_space=pltpu.SMEM),
                    pl.BlockSpec((tq, 2 * DIFF_DH), lambda b, h, i: (qb0 + b * nqt + i, h)),
                    pl.BlockSpec((s, 2 * DIFF_DH), lambda b, h, i: (kb0 + b, nq_blocks + h)),
                    pl.BlockSpec((s, DIFF_DV), lambda b, h, i: (kb0 + b, 2 * nq_blocks + h)),
                    pl.BlockSpec((1, DIFF_DV), lambda b, h, i: (0, 0))]
        args = [lam, proj, proj, proj, subln_g.reshape(1, DIFF_DV)]
        aliases = {}
        if out is not None:
            in_specs.append(pl.BlockSpec(memory_space=pl.ANY))
            args.append(out)
            aliases = {len(args) - 1: 0}
        out = pl.pallas_call(
            functools.partial(_attn_kernel, out_scale=1.0 - lambda_init),
            grid=(n_seq, DIFF_HEADS, nqt),
            in_specs=in_specs,
            out_specs=pl.BlockSpec((tq, DIFF_DV), lambda b, h, i: (qb0 + b * nqt + i, h)),
            out_shape=jax.ShapeDtypeStruct((t, DIFF_HEADS * DIFF_DV), BF16),
            input_output_aliases=aliases,
            compiler_params=_cparams(("arbitrary", "arbitrary", "arbitrary")),
            name=f"diff_attn_s{s}",
        )(*args)
    return out


def _first_max(cur, rows, n_rows):
    m = jnp.max(cur, axis=0, keepdims=True)
    idx = jnp.min(jnp.where(cur == m, rows, float(n_rows)), axis=0, keepdims=True)
    return m, idx


def _router_kernel(x_ref, wt_ref, b_ref, idx_ref, gate_ref, rank_ref, cnt_ref, carry_ref):
    i = pl.program_id(0)

    @pl.when(i == 0)
    def _init():
        carry_ref[...] = jnp.zeros_like(carry_ref)

    tn = x_ref.shape[0]
    x = x_ref[...]
    xh = x.astype(BF16)
    xl = (x - xh.astype(F32)).astype(BF16)
    w = wt_ref[...]
    wh = w.astype(BF16)
    wl = (w - wh.astype(F32)).astype(BF16)

    def nt(a, b):
        return lax.dot_general(a, b, (((1,), (1,)), ((), ())), preferred_element_type=F32)

    logits = nt(wh, xh) + nt(wh, xl) + nt(wl, xh)
    scores = 1.0 / (1.0 + jnp.exp(-logits))
    biased = scores + b_ref[...]

    rows_g = lax.broadcasted_iota(I32, (GROUP_SIZE, tn), 0).astype(F32)
    rows_e = lax.broadcasted_iota(I32, (N_EXPERTS, tn), 0).astype(F32)

    gscore = []
    for g in range(N_GROUPS):
        slab = biased[g * GROUP_SIZE:(g + 1) * GROUP_SIZE]
        m1, i1 = _first_max(slab, rows_g, GROUP_SIZE)
        m2 = jnp.max(jnp.where(rows_g == i1, NEG_INF, slab), axis=0, keepdims=True)
        gscore.append(m1 + m2)
    cur = jnp.concatenate(gscore, axis=0)
    rows_grp = lax.broadcasted_iota(I32, (N_GROUPS, tn), 0).astype(F32)
    gsel = jnp.zeros((N_GROUPS, tn), F32)
    for _ in range(TOPK_GROUPS):
        _, gi = _first_max(cur, rows_grp, N_GROUPS)
        pick = rows_grp == gi
        gsel = jnp.where(pick, 1.0, gsel)
        cur = jnp.where(pick, NEG_INF, cur)

    masked = jnp.concatenate(
        [jnp.where(gsel[g:g + 1] > 0.0, biased[g * GROUP_SIZE:(g + 1) * GROUP_SIZE], NEG_INF)
         for g in range(N_GROUPS)], axis=0)

    sel = jnp.zeros((N_EXPERTS, tn), F32)
    picks = []
    cur = masked
    for _ in range(TOP_K):
        _, ei = _first_max(cur, rows_e, N_EXPERTS)
        pick = rows_e == ei
        sel = jnp.where(pick, 1.0, sel)
        cur = jnp.where(pick, NEG_INF, cur)
        picks.append(ei)

    gsum = jnp.sum(scores * sel, axis=0, keepdims=True)

    ta = lax.broadcasted_iota(I32, (tn, tn), 0)
    tb = lax.broadcasted_iota(I32, (tn, tn), 1)
    before = jnp.where(ta < tb, 1.0, 0.0).astype(BF16)
    rank = jnp.dot(sel.astype(BF16), before, preferred_element_type=F32) + carry_ref[...]
    carry_ref[...] = carry_ref[...] + jnp.sum(sel, axis=1, keepdims=True)
    cnt_ref[...] = carry_ref[...]

    gates = []
    ranks = []
    for ei in picks:
        hit = rows_e == ei
        gates.append(jnp.sum(jnp.where(hit, scores, 0.0), axis=0, keepdims=True) / gsum * ROUTED_SCALE)
        ranks.append(jnp.sum(jnp.where(hit, rank, 0.0), axis=0, keepdims=True))
    idx_ref[...] = jnp.concatenate(picks, axis=0).astype(I32)
    gate_ref[...] = jnp.concatenate(gates, axis=0)
    rank_ref[...] = jnp.concatenate(ranks, axis=0).astype(I32)


def _router(x, w_router, b_router):
    t, d = x.shape
    tn = _pick_tile(t, (512, 256))
    kt = pl.BlockSpec((TOP_K, tn), lambda i: (0, i))
    return pl.pallas_call(
        _router_kernel,
        grid=(t // tn,),
        in_specs=[pl.BlockSpec((tn, d), lambda i: (i, 0)),
                  pl.BlockSpec((N_EXPERTS, d), lambda i: (0, 0)),
                  pl.BlockSpec((N_EXPERTS, 1), lambda i: (0, 0))],
        out_specs=[kt, kt, kt, pl.BlockSpec((N_EXPERTS, 1), lambda i: (0, 0))],
        out_shape=[jax.ShapeDtypeStruct((TOP_K, t), I32), jax.ShapeDtypeStruct((TOP_K, t), F32),
                   jax.ShapeDtypeStruct((TOP_K, t), I32), jax.ShapeDtypeStruct((N_EXPERTS, 1), F32)],
        scratch_shapes=[pltpu.VMEM((N_EXPERTS, 1), F32)],
        compiler_params=_cparams(("arbitrary",)),
        name="router",
    )(x, w_router.T, b_router.reshape(N_EXPERTS, 1))


def _slot_kernel(idx_ref, rank_ref, start_ref, slot_ref):
    tn = idx_ref.shape[1]
    rows_e = lax.broadcasted_iota(I32, (N_EXPERTS, tn), 0)
    out = []
    for k in range(TOP_K):
        hit = rows_e == idx_ref[k:k + 1, :]
        base = jnp.sum(jnp.where(hit, start_ref[...], 0.0), axis=0, keepdims=True)
        out.append(base.astype(I32) + rank_ref[k:k + 1, :])
    slot_ref[...] = jnp.concatenate(out, axis=0)


def _slots(idx_t, rank_t, pad_start):
    t = idx_t.shape[1]
    tn = _pick_tile(t, (512, 256))
    kt = pl.BlockSpec((TOP_K, tn), lambda i: (0, i))
    return pl.pallas_call(
        _slot_kernel,
        grid=(t // tn,),
        in_specs=[kt, kt, pl.BlockSpec((N_EXPERTS, 1), lambda i: (0, 0))],
        out_specs=kt,
        out_shape=jax.ShapeDtypeStruct((TOP_K, t), I32),
        compiler_params=_cparams(("arbitrary",)),
        name="slots",
    )(idx_t, rank_t, pad_start.astype(F32).reshape(N_EXPERTS, 1))


def _row_copy(src, src_row, dst, dst_row, sem):
    return pltpu.make_async_copy(src.at[_tile_of(src_row)], dst.at[_tile_of(dst_row)], sem)


def _dispatch_kernel(pad_end_ref, padded_ref, slot_hbm, x_ref, wgu_ref, wd_ref, xs_hbm, sh_ref,
                     slot_smem, zero_ref, wgu_bf, wd_bf, sem, slot_sem):
    i = pl.program_id(0)
    tt = x_ref.shape[0] // PACK_SUB
    bm = zero_ref.shape[0] // PACK_SUB
    slot_cp = pltpu.make_async_copy(slot_hbm.at[i], slot_smem, slot_sem)
    slot_cp.start()

    @pl.when(i == 0)
    def _clear():
        wgu_bf[...] = wgu_ref[0].astype(BF16)
        wd_bf[...] = wd_ref[0].astype(BF16)
        zero_ref[...] = jnp.zeros_like(zero_ref)

        def pad_copy(e):
            off = pl.multiple_of((pad_end_ref[e] - bm) * PACK_SUB, bm * PACK_SUB)
            return pltpu.make_async_copy(zero_ref, xs_hbm.at[pl.ds(off, bm * PACK_SUB)], sem)

        def start(e, carry):
            @pl.when(padded_ref[e] > 0)
            def _():
                pad_copy(e).start()
            return carry

        def wait(e, carry):
            @pl.when(padded_ref[e] > 0)
            def _():
                pad_copy(e).wait()
            return carry

        lax.fori_loop(0, N_EXPERTS, start, 0)
        lax.fori_loop(0, N_EXPERTS, wait, 0)

    slot_cp.wait()
    per_row = LANES // TOP_K

    def issue(r, carry):
        for j in range(LANES):
            _row_copy(x_ref, r * per_row + j // TOP_K, xs_hbm, slot_smem[r, j], sem).start(priority=j % 2)
        return carry

    lax.fori_loop(0, tt // per_row, issue, 0)

    sh_ref[...] = _gated_ffn(x_ref, wgu_bf, wd_bf)

    def drain(t, carry):
        for k in range(TOP_K):
            _row_copy(x_ref, t, xs_hbm, 0, sem).wait()
        return carry

    lax.fori_loop(0, tt, drain, 0)


def _dispatch(xp, slot_tiles, pad_end, padded, n_slots, sw_gu, sw_down, layer):
    t = xp.shape[0] // PACK_SUB
    tt = DISPATCH_TILE
    d = 2 * PACK_WORDS
    hidden = sw_down.shape[1]
    resident = dict(pipeline_mode=pl.Buffered(1))
    grid_spec = pltpu.PrefetchScalarGridSpec(
        num_scalar_prefetch=2,
        grid=(t // tt,),
        in_specs=[pl.BlockSpec(memory_space=pl.ANY),
                  pl.BlockSpec((tt * PACK_SUB, LANES), lambda i, pe, pd: (i, 0)),
                  pl.BlockSpec((1, d, 2 * hidden), lambda i, pe, pd: (layer, 0, 0), **resident),
                  pl.BlockSpec((1, hidden, d), lambda i, pe, pd: (layer, 0, 0), **resident)],
        out_specs=[pl.BlockSpec(memory_space=pl.ANY),
                   pl.BlockSpec((tt, d), lambda i, pe, pd: (i, 0))],
        scratch_shapes=[pltpu.SMEM((tt * TOP_K // LANES, LANES), I32),
                        pltpu.VMEM((EXPERT_ROWS * PACK_SUB, LANES), U32),
                        pltpu.VMEM((d, 2 * hidden), BF16), pltpu.VMEM((hidden, d), BF16),
                        pltpu.SemaphoreType.DMA, pltpu.SemaphoreType.DMA],
    )
    return pl.pallas_call(
        _dispatch_kernel,
        grid_spec=grid_spec,
        out_shape=[jax.ShapeDtypeStruct((n_slots * PACK_SUB, LANES), U32), jax.ShapeDtypeStruct((t, d), F32)],
        compiler_params=_cparams(("arbitrary",)),
        name="dispatch",
    )(pad_end, padded, slot_tiles, xp, sw_gu, sw_down)


def _gated_ffn(x_ref, wgu_bf, wd_bf):
    hidden = wd_bf.shape[0]
    rows = x_ref.shape[0] // PACK_SUB
    lo, hi = _unpack_rows(_load_packed(x_ref, rows))
    xb = jnp.concatenate([lo.astype(BF16), hi.astype(BF16)], axis=1)
    gu = jnp.dot(xb, wgu_bf[...], preferred_element_type=F32)
    gate = gu[:, :hidden]
    act = gate * (1.0 / (1.0 + jnp.exp(-gate))) * gu[:, hidden:]
    return jnp.dot(act.astype(BF16), wd_bf[...], preferred_element_type=F32)


def _ffn_kernel(block_e_ref, n_used_ref, x_ref, wgu_ref, wd_ref, o_ref, wgu_bf, wd_bf):
    b = pl.program_id(0)

    @pl.when(b < n_used_ref[0])
    def _():
        e = block_e_ref[b]
        e_prev = block_e_ref[jnp.maximum(b - 1, 0)]

        @pl.when((b == 0) | (e != e_prev))
        def _():
            wgu_bf[...] = wgu_ref[0, 0].astype(BF16)
            wd_bf[...] = wd_ref[0, 0].astype(BF16)

        _store_packed(o_ref, _pack_rows(_gated_ffn(x_ref, wgu_bf, wd_bf)))


def _ffn(xp, w_gu, w_down, layer, block_e, n_used, rows):
    n = xp.shape[0] // PACK_SUB
    d = 2 * PACK_WORDS
    hidden = w_down.shape[2]

    def row_map(b, be, nu):
        return (jnp.minimum(b, nu[0] - 1), 0)

    packed_rows = pl.BlockSpec((rows * PACK_SUB, LANES), row_map)
    grid_spec = pltpu.PrefetchScalarGridSpec(
        num_scalar_prefetch=2,
        grid=(n // rows,),
        in_specs=[packed_rows,
                  pl.BlockSpec((1, 1, d, 2 * hidden), lambda b, be, nu: (layer, be[b], 0, 0)),
                  pl.BlockSpec((1, 1, hidden, d), lambda b, be, nu: (layer, be[b], 0, 0))],
        out_specs=packed_rows,
        scratch_shapes=[pltpu.VMEM((d, 2 * hidden), BF16), pltpu.VMEM((hidden, d), BF16)],
    )
    return pl.pallas_call(
        _ffn_kernel,
        grid_spec=grid_spec,
        out_shape=jax.ShapeDtypeStruct((n * PACK_SUB, LANES), U32),
        compiler_params=_cparams(("arbitrary",)),
        name="expert_ffn",
    )(block_e, n_used, xp, w_gu, w_down)


def _combine_kernel(slot_hbm, x_ref, sh_ref, gate_ref, g_ref, b_ref, ys_hbm, o_ref, *rest, tile0, with_bf16):
    if with_bf16:
        ob_ref, *rest = rest
    slots0, slots1, buf0, buf1, sems, slot_sem = rest
    slot_smem = (slots0, slots1)
    bufs = (buf0, buf1)
    i = pl.program_id(0)
    n = pl.num_programs(0)
    tt = x_ref.shape[0]
    per_row = LANES // TOP_K
    groups = tt // per_row

    def load_slots(tile, par):
        slot_cp = pltpu.make_async_copy(slot_hbm.at[tile0 + tile], slot_smem[par], slot_sem)
        slot_cp.start()
        slot_cp.wait()

    def issue_group(r, par):
        for j in range(LANES):
            _row_copy(ys_hbm, slot_smem[par][r, j], bufs[par].at[j % TOP_K], r * per_row + j // TOP_K,
                      sems.at[par]).start(priority=j % 2)

    def drain(par):
        def body(t, carry):
            for k in range(TOP_K):
                _row_copy(ys_hbm, 0, bufs[par].at[k], t, sems.at[par]).wait()
            return carry
        lax.fori_loop(0, tt, body, 0)

    def reduce_group(r, par):
        rows = slice(r * per_row, (r + 1) * per_row)
        sh = sh_ref[rows, :]
        h_lo = sh[:, :PACK_WORDS]
        h_hi = sh[:, PACK_WORDS:]
        gate = gate_ref[rows, :]
        for k in range(TOP_K):
            words = jnp.concatenate(
                [bufs[par][k, pl.ds(r * per_row * PACK_SUB + s, per_row, stride=PACK_SUB), :]
                 for s in range(PACK_SUB)], axis=1)
            lo, hi = _unpack_rows(words)
            h_lo = h_lo + lo * gate[:, k:k + 1]
            h_hi = h_hi + hi * gate[:, k:k + 1]
        h = jnp.concatenate([h_lo, h_hi], axis=1)
        y = _layer_norm_rows(ALPHA * x_ref[rows, :] + h, g_ref[...], b_ref[...])
        o_ref[rows, :] = y
        if with_bf16:
            ob_ref[rows, :] = y.astype(BF16)

    @pl.when(i == 0)
    def _():
        load_slots(0, 0)
        for r in range(groups):
            issue_group(r, 0)

    nxt = jnp.minimum(i + 1, n - 1)
    for par in range(2):
        @pl.when(i % 2 == par)
        def _(par=par):
            drain(par)
            load_slots(nxt, 1 - par)
            for r in range(groups):
                issue_group(r, 1 - par)
                reduce_group(r, par)

            @pl.when(i == n - 1)
            def _():
                drain(1 - par)


def _combine(x, shared, ys, slot_tiles, gate, g, b, row_range, with_bf16):
    d = x.shape[1]
    tt = COMBINE_TILE
    r0, r1 = row_range
    tile0 = r0 // tt
    row_in = pl.BlockSpec((tt, d), lambda i: (tile0 + i, 0))
    row_out = pl.BlockSpec((tt, d), lambda i: (i, 0))
    vec = pl.BlockSpec((1, d), lambda i: (0, 0))
    out_specs = [row_out]
    out_shape = [jax.ShapeDtypeStruct((r1 - r0, d), F32)]
    if with_bf16:
        out_specs.append(row_out)
        out_shape.append(jax.ShapeDtypeStruct((r1 - r0, d), BF16))
    return pl.pallas_call(
        functools.partial(_combine_kernel, tile0=tile0, with_bf16=with_bf16),
        grid=((r1 - r0) // tt,),
        in_specs=[pl.BlockSpec(memory_space=pl.ANY), row_in, row_in,
                  pl.BlockSpec((tt, TOP_K), lambda i: (tile0 + i, 0)), vec, vec,
                  pl.BlockSpec(memory_space=pl.ANY)],
        out_specs=out_specs,
        out_shape=out_shape,
        scratch_shapes=[pltpu.SMEM((tt * TOP_K // LANES, LANES), I32), pltpu.SMEM((tt * TOP_K // LANES, LANES), I32),
                        pltpu.VMEM((TOP_K, tt * PACK_SUB, LANES), U32), pltpu.VMEM((TOP_K, tt * PACK_SUB, LANES), U32),
                        pltpu.SemaphoreType.DMA((2,)), pltpu.SemaphoreType.DMA],
        compiler_params=_cparams(("arbitrary",)),
        name="combine",
    )(slot_tiles, x, shared, gate, g.reshape(1, d), b.reshape(1, d), ys)


def _moe_layer(x, xp, layer, w_router, b_router, w_gu, w_down, sw_gu, sw_down, ln_g, ln_b, split_rows):
    t, d = x.shape
    bm = EXPERT_ROWS
    idx_t, gate_t, rank_t, counts = _router(x, w_router, b_router)

    counts = counts.reshape(N_EXPERTS).astype(I32)
    padded = (counts + bm - 1) // bm * bm
    pad_end = jnp.cumsum(padded)
    pad_start = pad_end - padded
    n_blocks = (t * TOP_K + N_EXPERTS * (bm - 1) + bm - 1) // bm
    n_slots = n_blocks * bm
    block_first = jnp.arange(n_blocks, dtype=I32) * bm
    block_e = jnp.minimum(jnp.sum((pad_end[None, :] <= block_first[:, None]).astype(I32), axis=1), N_EXPERTS - 1)
    n_used = (pad_end[-1:] // bm).astype(I32)

    slot_t = _slots(idx_t, rank_t, pad_start)
    slot_flat = slot_t.T.reshape(t * TOP_K // LANES, LANES)

    def slot_tiles(tile):
        return slot_flat.reshape(t // tile, tile * TOP_K // LANES, LANES)

    xs, shared = _dispatch(xp, slot_tiles(DISPATCH_TILE), pad_end.astype(I32), padded.astype(I32), n_slots,
                           sw_gu, sw_down, layer)
    ys = _ffn(xs, w_gu, w_down, layer, block_e, n_used, bm)
    args = (x, shared, ys, slot_tiles(COMBINE_TILE), gate_t.T, ln_g, ln_b)
    if split_rows is None:
        return _combine(*args, (0, t), True)
    return _combine(*args, (0, split_rows), False)[0], _combine(*args, (split_rows, t), False)[0]


def _rope_angles(s, theta, half):
    inv = theta ** (-jnp.arange(half, dtype=F32) / half)
    ang = jnp.arange(s, dtype=F32)[:, None] * inv[None, :]
    return jnp.cos(ang), jnp.sin(ang)


def _ret_tables(s):
    cos, sin = _rope_angles(s, RET_THETA, RET_DK // 2)
    return cos, sin, sin


def _diff_tables(s):
    half = ROPE_DIMS // 2
    cos, sin = _rope_angles(s, ROPE_THETA, half)
    rest = DIFF_DH - ROPE_DIMS
    zeros_h = jnp.zeros((s, half), F32)
    c = jnp.concatenate([cos, cos, jnp.ones((s, rest), F32)], axis=1)
    s_up = jnp.concatenate([zeros_h, sin, jnp.zeros((s, rest), F32)], axis=1)
    s_dn = jnp.concatenate([-sin, zeros_h, jnp.zeros((s, rest), F32)], axis=1)
    return c, s_up, s_dn


def kernel(x_prompt, x_sample, ret_w_in, ret_decay_f, ret_decay_b, ret_w_out, diff_w_in, diff_lam_q1, diff_lam_k1,
           diff_lam_q2, diff_lam_k2, diff_subln_g, diff_w_out, ln_mix_g, ln_mix_b, router_w, router_b, exp_w_gu,
           exp_w_down, shared_w_gu, shared_w_down, ln_ffn_g, ln_ffn_b):
    bp, sp, d = x_prompt.shape
    bs, ss, _ = x_sample.shape
    tp = bp * sp
    x_rows = [x_prompt.reshape(tp, d), x_sample.reshape(bs * ss, d)]
    proj_rows = x_rows
    segs = [(0, sp), (tp, ss)]
    s_max = max(sp, ss)

    for i in range(DEPTH):
        j = i // 2
        if i % 2 == 0:
            qk = RET_HEADS * RET_DK
            proj = _proj(proj_rows, ret_w_in[j].astype(BF16), _ret_tables(s_max), segs, mode="ret",
                         n_q_cols=qk, n_qk_cols=2 * qk, q_scale=1.0, k_scale=RET_DK ** -0.5)
            lg_f = -jax.nn.softplus(-ret_decay_f[j].astype(F32))
            lg_b = -jax.nn.softplus(-ret_decay_b[j].astype(F32))
            mixed = _retention(proj, lg_f, lg_b, segs)
            x, xp = _out_ln(mixed, ret_w_out[j].astype(BF16), x_rows, ln_mix_g[i], ln_mix_b[i], "ret_out")
        else:
            lambda_init = 0.8 - 0.6 * math.exp(-0.3 * i)
            nq = DIFF_HEADS * 2 * DIFF_DH
            proj = _proj(proj_rows, diff_w_in[j].astype(BF16), _diff_tables(s_max), segs, mode="diff",
                         n_q_cols=nq, n_qk_cols=2 * nq, q_scale=DIFF_DH ** -0.5 * LOG2E, k_scale=1.0)
            lam = (jnp.exp(jnp.sum(diff_lam_q1[j].astype(F32) * diff_lam_k1[j].astype(F32)))
                   - jnp.exp(jnp.sum(diff_lam_q2[j].astype(F32) * diff_lam_k2[j].astype(F32))) + lambda_init)
            mixed = _diff_attention(proj, lam.reshape(1), diff_subln_g[j],
                                    [(0, sp, bp), (tp, ss, bs)], lambda_init)
            x, xp = _out_ln(mixed, diff_w_out[j].astype(BF16), x_rows, ln_mix_g[i], ln_mix_b[i], "diff_out")
        last = i == DEPTH - 1
        x, xb = _moe_layer(x, xp, i, router_w[i], router_b[i], exp_w_gu, exp_w_down, shared_w_gu, shared_w_down,
                           ln_ffn_g[i], ln_ffn_b[i], tp if last else None)
        x_rows, proj_rows = [x], [xb]
    return x.reshape(bp, sp, d), xb.reshape(bs, ss, d)
```

```python
import functools
import math

import jax
import jax.numpy as jnp
from jax import lax
from jax.experimental import pallas as pl
from jax.experimental.pallas import tpu as pltpu

F32 = jnp.float32
BF16 = jnp.bfloat16
I32 = jnp.int32
U32 = jnp.uint32

D_MODEL = 2048
DEPTH = 2
ALPHA = (2 * DEPTH) ** 0.25
LN_EPS = 1e-5
RET_HEADS = 8
RET_DK = D_MODEL // RET_HEADS
RET_DV = 2 * RET_DK
RET_THETA = 10000.0
DIFF_HEADS = 8
DIFF_DH = D_MODEL // (2 * DIFF_HEADS)
DIFF_DV = 2 * DIFF_DH
ROPE_THETA = 500000.0
ROPE_DIMS = DIFF_DH // 4
N_EXPERTS = 64
TOP_K = 8
N_GROUPS = 8
TOPK_GROUPS = 4
GROUP_SIZE = N_EXPERTS // N_GROUPS
EXPERT_HIDDEN = 512
SHARED_HIDDEN = 512
ROUTED_SCALE = 2.5

LANES = 128
SUBLANES = 8
VMEM_LIMIT = 48 * 1024 * 1024

RET_CHUNK = 256
RET_GROUP = 8
EXPERT_ROWS = 512
DISPATCH_TILE = 512
COMBINE_TILE = 128
ATTN_Q_TILE = 512
NEG_INF = float("-inf")
LOG2E = math.log2(math.e)


def _cparams(sem):
    return pltpu.CompilerParams(dimension_semantics=sem, vmem_limit_bytes=VMEM_LIMIT)


def _pick_tile(n, prefs):
    for p in prefs:
        if n % p == 0:
            return p
    raise ValueError(f"no tile in {prefs} divides {n}")


def _seg_local_block(i, segs_blocks):
    val = None
    for first, per in segs_blocks:
        loc = lax.rem(i - first, per)
        val = loc if val is None else jnp.where(i >= first, loc, val)
    return val


def _proj_kernel(*refs, src_first, mode, n_q_tiles, n_qk_tiles, q_scale, k_scale):
    n_src = len(src_first)
    x_refs = refs[:n_src]
    w_ref, t0_ref, t1_ref, t2_ref, o_ref = refs[n_src:n_src + 5]
    i = pl.program_id(0)
    j = pl.program_id(1)
    if n_src == 1:
        x = x_refs[0][...]
    else:
        xb_ref = refs[n_src + 5]
        for s in range(n_src):
            hi = src_first[s + 1] if s + 1 < n_src else None
            in_src = (i >= src_first[s]) if hi is None else ((i >= src_first[s]) & (i < hi))

            @pl.when((j == 0) & in_src)
            def _(s=s):
                xb_ref[...] = x_refs[s][...].astype(BF16)
        x = xb_ref[...]
    tn = o_ref.shape[1]

    def cols(lo, width):
        return jnp.dot(x, w_ref[:, lo:lo + width], preferred_element_type=F32)

    @pl.when(j >= n_qk_tiles)
    def _plain():
        o_ref[...] = cols(0, tn).astype(o_ref.dtype)

    @pl.when(j < n_qk_tiles)
    def _rope():
        scale = jnp.where(j < n_q_tiles, q_scale, k_scale)
        if mode == "ret":
            cos = t0_ref[...]
            sin = t1_ref[...]
            half = RET_DK // 2
            for hs in range(tn // RET_DK):
                lo = hs * RET_DK
                acc = cols(lo, RET_DK)
                x1 = acc[:, :half]
                x2 = acc[:, half:]
                o_ref[:, lo:lo + half] = ((x1 * cos - x2 * sin) * scale).astype(o_ref.dtype)
                o_ref[:, lo + half:lo + RET_DK] = ((x2 * cos + x1 * sin) * scale).astype(o_ref.dtype)
        else:
            c = t0_ref[...]
            s_up = t1_ref[...]
            s_dn = t2_ref[...]
            half = ROPE_DIMS // 2
            group = 2 * DIFF_DH
            for hs in range(tn // group):
                acc = cols(hs * group, group)
                for g in range(group // DIFF_DH):
                    lo = hs * group + g * DIFF_DH
                    seg = acc[:, g * DIFF_DH:(g + 1) * DIFF_DH]
                    rot = seg * c + pltpu.roll(seg, half, 1) * s_up + pltpu.roll(seg, DIFF_DH - half, 1) * s_dn
                    o_ref[:, lo:lo + DIFF_DH] = (rot * scale).astype(o_ref.dtype)


def _row_sources(sources, tm, index_of, **spec_kwargs):
    firsts, specs, first = [], [], 0
    for src in sources:
        nblk = src.shape[0] // tm
        firsts.append(first)
        specs.append(pl.BlockSpec((tm, src.shape[1]),
                                  lambda *g, first=first, nblk=nblk: (jnp.clip(index_of(*g) - first, 0, nblk - 1), 0),
                                  **spec_kwargs))
        first += nblk
    return tuple(firsts), specs, first


def _proj(sources, w, tables, segs, *, mode, n_q_cols, n_qk_cols, q_scale, k_scale):
    k = sources[0].shape[1]
    n = w.shape[1]
    seg_len = [s for _, s in segs]
    tm = _pick_tile(math.gcd(*seg_len), (1024, 512, 256))
    tn = 1024
    segs_blocks = [(start // tm, s // tm) for start, s in segs]
    tw = tables[0].shape[1]
    tab_spec = pl.BlockSpec((tm, tw), lambda i, j: (_seg_local_block(i, segs_blocks), 0))
    kwargs = {"pipeline_mode": pl.Buffered(1)} if len(sources) > 1 else {}
    src_first, src_specs, n_row_blocks = _row_sources(sources, tm, lambda i, j: i, **kwargs)
    kern = functools.partial(_proj_kernel, src_first=src_first, mode=mode, n_q_tiles=n_q_cols // tn,
                             n_qk_tiles=n_qk_cols // tn, q_scale=q_scale, k_scale=k_scale)
    return pl.pallas_call(
        kern,
        grid=(n_row_blocks, n // tn),
        in_specs=src_specs + [pl.BlockSpec((k, tn), lambda i, j: (0, j)), tab_spec, tab_spec, tab_spec],
        out_specs=pl.BlockSpec((tm, tn), lambda i, j: (i, j)),
        out_shape=jax.ShapeDtypeStruct((n_row_blocks * tm, n), BF16),
        scratch_shapes=[pltpu.VMEM((tm, k), BF16)] if len(sources) > 1 else [],
        compiler_params=_cparams(("arbitrary", "arbitrary")),
        name=f"proj_{mode}",
    )(*sources, w, *tables)


HIGH_HALF = 0xFFFF0000


def _pack_rows(y):
    n = y.shape[1] // 2
    lo = lax.bitcast_convert_type(y[:, :n].astype(BF16).astype(F32), U32) >> 16
    hi = lax.bitcast_convert_type(y[:, n:].astype(BF16).astype(F32), U32) & jnp.uint32(HIGH_HALF)
    return lo | hi


def _unpack_rows(w):
    lo = lax.bitcast_convert_type(w << 16, F32)
    hi = lax.bitcast_convert_type(w & jnp.uint32(HIGH_HALF), F32)
    return lo, hi


PACK_WORDS = D_MODEL // 2
PACK_SUB = PACK_WORDS // LANES
assert PACK_SUB == SUBLANES


def _tile_of(row):
    start = row * PACK_SUB
    return pl.ds(start if isinstance(start, int) else pl.multiple_of(start, PACK_SUB), PACK_SUB)


def _store_packed(ref, words):
    r = words.shape[0]
    for s in range(PACK_SUB):
        ref[pl.ds(s, r, stride=PACK_SUB), :] = words[:, s * LANES:(s + 1) * LANES]


def _load_packed(ref, r):
    return jnp.concatenate([ref[pl.ds(s, r, stride=PACK_SUB), :] for s in range(PACK_SUB)], axis=1)


def _layer_norm_rows(z, g, b):
    mu = jnp.mean(z, axis=-1, keepdims=True)
    zc = z - mu
    var = jnp.mean(zc * zc, axis=-1, keepdims=True)
    return zc * lax.rsqrt(var + LN_EPS) * g + b


def _out_ln_kernel(a_ref, w_ref, g_ref, b_ref, *refs, src_first):
    n_src = len(src_first)
    x_refs = refs[:n_src]
    o_ref, op_ref = refs[n_src:]
    i = pl.program_id(0)
    h = jnp.dot(a_ref[...], w_ref[...], preferred_element_type=F32)

    def finish(x_ref):
        y = _layer_norm_rows(ALPHA * x_ref[...] + h, g_ref[...], b_ref[...])
        o_ref[...] = y
        _store_packed(op_ref, _pack_rows(y))

    if n_src == 1:
        finish(x_refs[0])
    else:
        for s in range(n_src):
            hi = src_first[s + 1] if s + 1 < n_src else None
            in_src = (i >= src_first[s]) if hi is None else ((i >= src_first[s]) & (i < hi))

            @pl.when(in_src)
            def _(s=s):
                finish(x_refs[s])


def _cast_rows_kernel(*refs, src_first):
    n_src = len(src_first)
    o_ref = refs[n_src]
    i = pl.program_id(0)
    for s in range(n_src):
        hi = src_first[s + 1] if s + 1 < n_src else None
        in_src = (i >= src_first[s]) if hi is None else ((i >= src_first[s]) & (i < hi))

        @pl.when(in_src)
        def _(s=s):
            o_ref[...] = refs[s][...].astype(o_ref.dtype)


def _cast_rows(sources):
    d = sources[0].shape[1]
    tm = 512
    src_first, src_specs, n_row_blocks = _row_sources(sources, tm, lambda i: i)
    return pl.pallas_call(
        functools.partial(_cast_rows_kernel, src_first=src_first),
        grid=(n_row_blocks,),
        in_specs=src_specs,
        out_specs=pl.BlockSpec((tm, d), lambda i: (i, 0)),
        out_shape=jax.ShapeDtypeStruct((n_row_blocks * tm, d), BF16),
        compiler_params=_cparams(("arbitrary",)),
        name="cast_rows",
    )(*sources)


def _out_ln(a, w, x_sources, g, b, name):
    m, kk = a.shape
    d = w.shape[1]
    tm = 512 if kk * d * 2 <= 8 * 1024 * 1024 else 256
    row = pl.BlockSpec((tm, d), lambda i: (i, 0))
    vec = pl.BlockSpec((1, d), lambda i: (0, 0))
    src_first, src_specs, n_row_blocks = _row_sources(x_sources, tm, lambda i: i)
    assert n_row_blocks * tm == m
    return pl.pallas_call(
        functools.partial(_out_ln_kernel, src_first=src_first),
        grid=(m // tm,),
        in_specs=[pl.BlockSpec((tm, kk), lambda i: (i, 0)),
                  pl.BlockSpec((kk, d), lambda i: (0, 0), pipeline_mode=pl.Buffered(1)), vec, vec] + src_specs,
        out_specs=[row, pl.BlockSpec((tm * PACK_SUB, LANES), lambda i: (i, 0))],
        out_shape=[jax.ShapeDtypeStruct((m, d), F32), jax.ShapeDtypeStruct((m * PACK_SUB, LANES), U32)],
        compiler_params=_cparams(("arbitrary",)),
        name=name,
    )(a, w, g.reshape(1, d), b.reshape(1, d), *x_sources)


def _ret_kernel(lg_ref, cd_ref, q_ref, k_ref, v_ref, *rest, backward, segs_chunks):
    if backward:
        of_ref, g_ref, o_ref, state_ref, dmat_ref, qd_ref, kd_ref = rest
    else:
        o_ref, state_ref, dmat_ref, qd_ref, kd_ref = rest
    h = pl.program_id(0)
    c = pl.program_id(1)
    nc = pl.num_programs(1)
    cc = (nc - 1 - c) if backward else c
    lg = lg_ref[h]
    cs = RET_CHUNK

    @pl.when(c == 0)
    def _tables():
        ii = lax.broadcasted_iota(I32, (cs, cs), 0)
        jj = lax.broadcasted_iota(I32, (cs, cs), 1)
        if backward:
            mask = jj > ii
            dist = (jj - ii).astype(F32)
        else:
            mask = ii >= jj
            dist = (ii - jj).astype(F32)
        dmat_ref[...] = jnp.where(mask, jnp.exp(jnp.where(mask, dist, 0.0) * lg), 0.0)
        pos = lax.broadcasted_iota(I32, (cs, 1), 0).astype(F32)
        if backward:
            qd_ref[...] = jnp.exp((cs - pos) * lg)
            kd_ref[...] = jnp.exp(pos * lg)
        else:
            qd_ref[...] = jnp.exp((pos + 1.0) * lg)
            kd_ref[...] = jnp.exp((cs - 1.0 - pos) * lg)

    loc = _seg_local_block(cc, segs_chunks)
    if backward:
        per = None
        for first, p in segs_chunks:
            per = p if per is None else jnp.where(cc >= first, p, per)
        is_start = loc == per - 1
    else:
        is_start = loc == 0

    @pl.when(is_start)
    def _reset():
        state_ref[...] = jnp.zeros_like(state_ref)

    state = state_ref[...]
    n_sub = q_ref.shape[0] // cs
    for sub in (range(n_sub - 1, -1, -1) if backward else range(n_sub)):
        rows = pl.ds(sub * cs, cs)
        q = q_ref[rows, :]
        k = k_ref[rows, :]
        v = v_ref[rows, :]
        att = lax.dot_general(q, k, (((1,), (1,)), ((), ())), preferred_element_type=F32) * dmat_ref[...]
        inner = jnp.dot(att.astype(BF16), v, preferred_element_type=F32)
        cross = jnp.dot(q, state.astype(BF16), preferred_element_type=F32) * qd_ref[...]
        o = inner + cross
        kt = (k.astype(F32) * kd_ref[...]).T.astype(BF16)
        state = state * cd_ref[h] + jnp.dot(kt, v, preferred_element_type=F32)

        if backward:
            tot = of_ref[rows, :] + o
            mu = jnp.mean(tot, axis=-1, keepdims=True)
            tc = tot - mu
            var = jnp.mean(tc * tc, axis=-1, keepdims=True)
            on = tc * lax.rsqrt(var + LN_EPS)
            g = g_ref[rows, :].astype(F32)
            o_ref[rows, :] = ((g * (1.0 / (1.0 + jnp.exp(-g)))) * on).astype(o_ref.dtype)
        else:
            o_ref[rows, :] = o
    state_ref[...] = state


def _retention(proj, lg_f, lg_b, segs):
    t = proj.shape[0]
    cs = RET_CHUNK
    rb = RET_CHUNK * RET_GROUP
    nc = t // rb
    segs_chunks = [(start // rb, s // rb) for start, s in segs]
    qk_blocks = (RET_HEADS * RET_DK) // RET_DK
    v_blocks0 = (2 * RET_HEADS * RET_DK) // RET_DV
    g_blocks0 = v_blocks0 + RET_HEADS
    smem = pl.BlockSpec(memory_space=pltpu.SMEM)
    scratch = [pltpu.VMEM((RET_DK, RET_DV), F32), pltpu.VMEM((cs, cs), F32),
               pltpu.VMEM((cs, 1), F32), pltpu.VMEM((cs, 1), F32)]

    def run(backward, lg, extra):
        def cmap(c):
            return (nc - 1 - c) if backward else c
        in_specs = [smem, smem,
                    pl.BlockSpec((rb, RET_DK), lambda h, c: (cmap(c), h)),
                    pl.BlockSpec((rb, RET_DK), lambda h, c: (cmap(c), qk_blocks + h)),
                    pl.BlockSpec((rb, RET_DV), lambda h, c: (cmap(c), v_blocks0 + h))]
        args = [lg, jnp.exp(cs * lg), proj, proj, proj]
        if backward:
            in_specs += [pl.BlockSpec((rb, RET_DV), lambda h, c: (cmap(c), h)),
                         pl.BlockSpec((rb, RET_DV), lambda h, c: (cmap(c), g_blocks0 + h))]
            args += [extra, proj]
        return pl.pallas_call(
            functools.partial(_ret_kernel, backward=backward, segs_chunks=segs_chunks),
            grid=(RET_HEADS, nc),
            in_specs=in_specs,
            out_specs=pl.BlockSpec((rb, RET_DV), lambda h, c: (cmap(c), h)),
            out_shape=jax.ShapeDtypeStruct((t, RET_HEADS * RET_DV), BF16 if backward else F32),
            scratch_shapes=scratch,
            compiler_params=_cparams(("arbitrary", "arbitrary")),
            name="retention_bwd" if backward else "retention_fwd",
        )(*args)

    o_f = run(False, lg_f, None)
    return run(True, lg_b, o_f)


def _attn_kernel(lam_ref, q_ref, k_ref, v_ref, sg_ref, *rest, out_scale):
    o_ref = rest[-1]
    probs = []
    for c in range(2):
        q = q_ref[:, c * DIFF_DH:(c + 1) * DIFF_DH]
        k = k_ref[:, c * DIFF_DH:(c + 1) * DIFF_DH]
        s = lax.dot_general(q, k, (((1,), (1,)), ((), ())), preferred_element_type=F32)
        p = jnp.exp2(s - jnp.max(s, axis=-1, keepdims=True))
        probs.append((p, jnp.sum(p, axis=-1, keepdims=True)))
    (p0, den0), (p1, den1) = probs
    a = p0 * (1.0 / den0) - p1 * (lam_ref[0] / den1)
    o = jnp.dot(a.astype(BF16), v_ref[...], preferred_element_type=F32)
    ms = jnp.mean(o * o, axis=-1, keepdims=True)
    o_ref[...] = (o * lax.rsqrt(ms + LN_EPS) * sg_ref[...] * out_scale).astype(o_ref.dtype)


def _diff_attention(proj, lam, subln_g, segs, lambda_init):
    nq_blocks = DIFF_HEADS
    t = proj.shape[0]
    out = None
    for start, s, n_seq in segs:
        tq = ATTN_Q_TILE
        assert s % tq == 0 and start % s == 0
        qb0 = start // tq
        kb0 = start // s
        nqt = s // tq
        in_specs = [pl.BlockSpec(memory_space=pltpu.SMEM),
                    pl.BlockSpec((tq, 2 * DIFF_DH), lambda b, h, i: (qb0 + b * nqt + i, h)),
                    pl.BlockSpec((s, 2 * DIFF_DH), lambda b, h, i: (kb0 + b, nq_blocks + h)),
                    pl.BlockSpec((s, DIFF_DV), lambda b, h, i: (kb0 + b, 2 * nq_blocks + h)),
                    pl.BlockSpec((1, DIFF_DV), lambda b, h, i: (0, 0))]
        args = [lam, proj, proj, proj, subln_g.reshape(1, DIFF_DV)]
        aliases = {}
        if out is not None:
            in_specs.append(pl.BlockSpec(memory_space=pl.ANY))
            args.append(out)
            aliases = {len(args) - 1: 0}
        out = pl.pallas_call(
            functools.partial(_attn_kernel, out_scale=1.0 - lambda_init),
            grid=(n_seq, DIFF_HEADS, nqt),
            in_specs=in_specs,
            out_specs=pl.BlockSpec((tq, DIFF_DV), lambda b, h, i: (qb0 + b * nqt + i, h)),
            out_shape=jax.ShapeDtypeStruct((t, DIFF_HEADS * DIFF_DV), BF16),
            input_output_aliases=aliases,
            compiler_params=_cparams(("arbitrary", "arbitrary", "arbitrary")),
            name=f"diff_attn_s{s}",
        )(*args)
    return out


def _first_max(cur, rows, n_rows):
    m = jnp.max(cur, axis=0, keepdims=True)
    idx = jnp.min(jnp.where(cur == m, rows, float(n_rows)), axis=0, keepdims=True)
    return m, idx


def _router_kernel(x_ref, wt_ref, b_ref, idx_ref, gate_ref, rank_ref, cnt_ref, carry_ref):
    i = pl.program_id(0)

    @pl.when(i == 0)
    def _init():
        carry_ref[...] = jnp.zeros_like(carry_ref)

    tn = x_ref.shape[0]
    x = x_ref[...]
    xh = x.astype(BF16)
    xl = (x - xh.astype(F32)).astype(BF16)
    w = wt_ref[...]
    wh = w.astype(BF16)
    wl = (w - wh.astype(F32)).astype(BF16)

    def nt(a, b):
        return lax.dot_general(a, b, (((1,), (1,)), ((), ())), preferred_element_type=F32)

    logits = nt(wh, xh) + nt(wh, xl) + nt(wl, xh)
    scores = 1.0 / (1.0 + jnp.exp(-logits))
    biased = scores + b_ref[...]

    rows_g = lax.broadcasted_iota(I32, (GROUP_SIZE, tn), 0).astype(F32)
    rows_e = lax.broadcasted_iota(I32, (N_EXPERTS, tn), 0).astype(F32)

    gscore = []
    for g in range(N_GROUPS):
        slab = biased[g * GROUP_SIZE:(g + 1) * GROUP_SIZE]
        m1, i1 = _first_max(slab, rows_g, GROUP_SIZE)
        m2 = jnp.max(jnp.where(rows_g == i1, NEG_INF, slab), axis=0, keepdims=True)
        gscore.append(m1 + m2)
    cur = jnp.concatenate(gscore, axis=0)
    rows_grp = lax.broadcasted_iota(I32, (N_GROUPS, tn), 0).astype(F32)
    gsel = jnp.zeros((N_GROUPS, tn), F32)
    for _ in range(TOPK_GROUPS):
        _, gi = _first_max(cur, rows_grp, N_GROUPS)
        pick = rows_grp == gi
        gsel = jnp.where(pick, 1.0, gsel)
        cur = jnp.where(pick, NEG_INF, cur)

    masked = jnp.concatenate(
        [jnp.where(gsel[g:g + 1] > 0.0, biased[g * GROUP_SIZE:(g + 1) * GROUP_SIZE], NEG_INF)
         for g in range(N_GROUPS)], axis=0)

    sel = jnp.zeros((N_EXPERTS, tn), F32)
    picks = []
    cur = masked
    for _ in range(TOP_K):
        _, ei = _first_max(cur, rows_e, N_EXPERTS)
        pick = rows_e == ei
        sel = jnp.where(pick, 1.0, sel)
        cur = jnp.where(pick, NEG_INF, cur)
        picks.append(ei)

    gsum = jnp.sum(scores * sel, axis=0, keepdims=True)

    ta = lax.broadcasted_iota(I32, (tn, tn), 0)
    tb = lax.broadcasted_iota(I32, (tn, tn), 1)
    before = jnp.where(ta < tb, 1.0, 0.0).astype(BF16)
    rank = jnp.dot(sel.astype(BF16), before, preferred_element_type=F32) + carry_ref[...]
    carry_ref[...] = carry_ref[...] + jnp.sum(sel, axis=1, keepdims=True)
    cnt_ref[...] = carry_ref[...]

    gates = []
    ranks = []
    for ei in picks:
        hit = rows_e == ei
        gates.append(jnp.sum(jnp.where(hit, scores, 0.0), axis=0, keepdims=True) / gsum * ROUTED_SCALE)
        ranks.append(jnp.sum(jnp.where(hit, rank, 0.0), axis=0, keepdims=True))
    idx_ref[...] = jnp.concatenate(picks, axis=0).astype(I32)
    gate_ref[...] = jnp.concatenate(gates, axis=0)
    rank_ref[...] = jnp.concatenate(ranks, axis=0).astype(I32)


def _router(x, w_router, b_router):
    t, d = x.shape
    tn = _pick_tile(t, (512, 256))
    kt = pl.BlockSpec((TOP_K, tn), lambda i: (0, i))
    return pl.pallas_call(
        _router_kernel,
        grid=(t // tn,),
        in_specs=[pl.BlockSpec((tn, d), lambda i: (i, 0)),
                  pl.BlockSpec((N_EXPERTS, d), lambda i: (0, 0)),
                  pl.BlockSpec((N_EXPERTS, 1), lambda i: (0, 0))],
        out_specs=[kt, kt, kt, pl.BlockSpec((N_EXPERTS, 1), lambda i: (0, 0))],
        out_shape=[jax.ShapeDtypeStruct((TOP_K, t), I32), jax.ShapeDtypeStruct((TOP_K, t), F32),
                   jax.ShapeDtypeStruct((TOP_K, t), I32), jax.ShapeDtypeStruct((N_EXPERTS, 1), F32)],
        scratch_shapes=[pltpu.VMEM((N_EXPERTS, 1), F32)],
        compiler_params=_cparams(("arbitrary",)),
        name="router",
    )(x, w_router.T, b_router.reshape(N_EXPERTS, 1))


def _slot_kernel(idx_ref, rank_ref, start_ref, slot_ref):
    tn = idx_ref.shape[1]
    rows_e = lax.broadcasted_iota(I32, (N_EXPERTS, tn), 0)
    out = []
    for k in range(TOP_K):
        hit = rows_e == idx_ref[k:k + 1, :]
        base = jnp.sum(jnp.where(hit, start_ref[...], 0.0), axis=0, keepdims=True)
        out.append(base.astype(I32) + rank_ref[k:k + 1, :])
    slot_ref[...] = jnp.concatenate(out, axis=0)


def _slots(idx_t, rank_t, pad_start):
    t = idx_t.shape[1]
    tn = _pick_tile(t, (512, 256))
    kt = pl.BlockSpec((TOP_K, tn), lambda i: (0, i))
    return pl.pallas_call(
        _slot_kernel,
        grid=(t // tn,),
        in_specs=[kt, kt, pl.BlockSpec((N_EXPERTS, 1), lambda i: (0, 0))],
        out_specs=kt,
        out_shape=jax.ShapeDtypeStruct((TOP_K, t), I32),
        compiler_params=_cparams(("arbitrary",)),
        name="slots",
    )(idx_t, rank_t, pad_start.astype(F32).reshape(N_EXPERTS, 1))


def _row_copy(src, src_row, dst, dst_row, sem):
    return pltpu.make_async_copy(src.at[_tile_of(src_row)], dst.at[_tile_of(dst_row)], sem)


def _dispatch_kernel(pad_end_ref, padded_ref, slot_hbm, x_ref, xs_hbm, slot_smem, zero_ref, sem, slot_sem):
    i = pl.program_id(0)
    tt = x_ref.shape[0] // PACK_SUB
    bm = zero_ref.shape[0] // PACK_SUB
    slot_cp = pltpu.make_async_copy(slot_hbm.at[i], slot_smem, slot_sem)
    slot_cp.start()

    @pl.when(i == 0)
    def _clear():
        zero_ref[...] = jnp.zeros_like(zero_ref)

        def pad_copy(e):
            off = pl.multiple_of((pad_end_ref[e] - bm) * PACK_SUB, bm * PACK_SUB)
            return pltpu.make_async_copy(zero_ref, xs_hbm.at[pl.ds(off, bm * PACK_SUB)], sem)

        def start(e, carry):
            @pl.when(padded_ref[e] > 0)
            def _():
                pad_copy(e).start()
            return carry

        def wait(e, carry):
            @pl.when(padded_ref[e] > 0)
            def _():
                pad_copy(e).wait()
            return carry

        lax.fori_loop(0, N_EXPERTS, start, 0)
        lax.fori_loop(0, N_EXPERTS, wait, 0)

    slot_cp.wait()
    per_row = LANES // TOP_K

    def issue(r, carry):
        for j in range(LANES):
            _row_copy(x_ref, r * per_row + j // TOP_K, xs_hbm, slot_smem[r, j], sem).start(priority=j % 2)
        return carry

    lax.fori_loop(0, tt // per_row, issue, 0)

    def drain(t, carry):
        for k in range(TOP_K):
            _row_copy(x_ref, t, xs_hbm, 0, sem).wait()
        return carry

    lax.fori_loop(0, tt, drain, 0)


def _dispatch(xp, slot_tiles, pad_end, padded, n_slots):
    t = xp.shape[0] // PACK_SUB
    tt = DISPATCH_TILE
    grid_spec = pltpu.PrefetchScalarGridSpec(
        num_scalar_prefetch=2,
        grid=(t // tt,),
        in_specs=[pl.BlockSpec(memory_space=pl.ANY),
                  pl.BlockSpec((tt * PACK_SUB, LANES), lambda i, pe, pd: (i, 0))],
        out_specs=pl.BlockSpec(memory_space=pl.ANY),
        scratch_shapes=[pltpu.SMEM((tt * TOP_K // LANES, LANES), I32),
                        pltpu.VMEM((EXPERT_ROWS * PACK_SUB, LANES), U32),
                        pltpu.SemaphoreType.DMA, pltpu.SemaphoreType.DMA],
    )
    return pl.pallas_call(
        _dispatch_kernel,
        grid_spec=grid_spec,
        out_shape=jax.ShapeDtypeStruct((n_slots * PACK_SUB, LANES), U32),
        compiler_params=_cparams(("arbitrary",)),
        name="dispatch",
    )(pad_end, padded, slot_tiles, xp)


def _ffn_kernel(block_e_ref, n_used_ref, x_ref, wgu_ref, wd_ref, o_ref, wgu_bf, wd_bf, *, pack_out):
    b = pl.program_id(0)

    @pl.when(b < n_used_ref[0])
    def _():
        e = block_e_ref[b]
        e_prev = block_e_ref[jnp.maximum(b - 1, 0)]

        @pl.when((b == 0) | (e != e_prev))
        def _():
            wgu_bf[...] = wgu_ref[0, 0].astype(BF16)
            wd_bf[...] = wd_ref[0, 0].astype(BF16)

        hidden = wd_bf.shape[0]
        rows = x_ref.shape[0] // PACK_SUB
        lo, hi = _unpack_rows(_load_packed(x_ref, rows))
        gu = (jnp.dot(lo.astype(BF16), wgu_bf[:PACK_WORDS], preferred_element_type=F32)
              + jnp.dot(hi.astype(BF16), wgu_bf[PACK_WORDS:], preferred_element_type=F32))
        gate = gu[:, :hidden]
        act = gate * (1.0 / (1.0 + jnp.exp(-gate))) * gu[:, hidden:]
        y = jnp.dot(act.astype(BF16), wd_bf[...], preferred_element_type=F32)
        if pack_out:
            _store_packed(o_ref, _pack_rows(y))
        else:
            o_ref[...] = y


def _ffn(xp, w_gu, w_down, layer, block_e, n_used, rows, pack_out, name):
    n = xp.shape[0] // PACK_SUB
    d = 2 * PACK_WORDS
    hidden = w_down.shape[2]

    def row_map(b, be, nu):
        return (jnp.minimum(b, nu[0] - 1), 0)

    packed_rows = pl.BlockSpec((rows * PACK_SUB, LANES), row_map)
    if pack_out:
        out_spec, out_shape = packed_rows, jax.ShapeDtypeStruct((n * PACK_SUB, LANES), U32)
    else:
        out_spec, out_shape = pl.BlockSpec((rows, d), row_map), jax.ShapeDtypeStruct((n, d), F32)
    grid_spec = pltpu.PrefetchScalarGridSpec(
        num_scalar_prefetch=2,
        grid=(n // rows,),
        in_specs=[packed_rows,
                  pl.BlockSpec((1, 1, d, 2 * hidden), lambda b, be, nu: (layer, be[b], 0, 0)),
                  pl.BlockSpec((1, 1, hidden, d), lambda b, be, nu: (layer, be[b], 0, 0))],
        out_specs=out_spec,
        scratch_shapes=[pltpu.VMEM((d, 2 * hidden), BF16), pltpu.VMEM((hidden, d), BF16)],
    )
    return pl.pallas_call(
        functools.partial(_ffn_kernel, pack_out=pack_out),
        grid_spec=grid_spec,
        out_shape=out_shape,
        compiler_params=_cparams(("arbitrary",)),
        name=name,
    )(block_e, n_used, xp, w_gu, w_down)


def _combine_kernel(slot_hbm, x_ref, sh_ref, gate_ref, g_ref, b_ref, ys_hbm, o_ref, *rest, tile0, with_bf16):
    if with_bf16:
        ob_ref, *rest = rest
    slots0, slots1, buf0, buf1, sems, slot_sem = rest
    slot_smem = (slots0, slots1)
    bufs = (buf0, buf1)
    i = pl.program_id(0)
    n = pl.num_programs(0)
    tt = x_ref.shape[0]
    per_row = LANES // TOP_K
    groups = tt // per_row

    def load_slots(tile, par):
        slot_cp = pltpu.make_async_copy(slot_hbm.at[tile0 + tile], slot_smem[par], slot_sem)
        slot_cp.start()
        slot_cp.wait()

    def issue_group(r, par):
        for j in range(LANES):
            _row_copy(ys_hbm, slot_smem[par][r, j], bufs[par].at[j % TOP_K], r * per_row + j // TOP_K,
                      sems.at[par]).start(priority=j % 2)

    def drain(par):
        def body(t, carry):
            for k in range(TOP_K):
                _row_copy(ys_hbm, 0, bufs[par].at[k], t, sems.at[par]).wait()
            return carry
        lax.fori_loop(0, tt, body, 0)

    def reduce_group(r, par):
        rows = slice(r * per_row, (r + 1) * per_row)
        sh = sh_ref[rows, :]
        h_lo = sh[:, :PACK_WORDS]
        h_hi = sh[:, PACK_WORDS:]
        gate = gate_ref[rows, :]
        for k in range(TOP_K):
            words = jnp.concatenate(
                [bufs[par][k, pl.ds(r * per_row * PACK_SUB + s, per_row, stride=PACK_SUB), :]
                 for s in range(PACK_SUB)], axis=1)
            lo, hi = _unpack_rows(words)
            h_lo = h_lo + lo * gate[:, k:k + 1]
            h_hi = h_hi + hi * gate[:, k:k + 1]
        h = jnp.concatenate([h_lo, h_hi], axis=1)
        y = _layer_norm_rows(ALPHA * x_ref[rows, :] + h, g_ref[...], b_ref[...])
        o_ref[rows, :] = y
        if with_bf16:
            ob_ref[rows, :] = y.astype(BF16)

    @pl.when(i == 0)
    def _():
        load_slots(0, 0)
        for r in range(groups):
            issue_group(r, 0)

    nxt = jnp.minimum(i + 1, n - 1)
    for par in range(2):
        @pl.when(i % 2 == par)
        def _(par=par):
            drain(par)
            load_slots(nxt, 1 - par)
            for r in range(groups):
                issue_group(r, 1 - par)
                reduce_group(r, par)

            @pl.when(i == n - 1)
            def _():
                drain(1 - par)


def _combine(x, shared, ys, slot_tiles, gate, g, b, row_range, with_bf16):
    d = x.shape[1]
    tt = COMBINE_TILE
    r0, r1 = row_range
    tile0 = r0 // tt
    row_in = pl.BlockSpec((tt, d), lambda i: (tile0 + i, 0))
    row_out = pl.BlockSpec((tt, d), lambda i: (i, 0))
    vec = pl.BlockSpec((1, d), lambda i: (0, 0))
    out_specs = [row_out]
    out_shape = [jax.ShapeDtypeStruct((r1 - r0, d), F32)]
    if with_bf16:
        out_specs.append(row_out)
        out_shape.append(jax.ShapeDtypeStruct((r1 - r0, d), BF16))
    return pl.pallas_call(
        functools.partial(_combine_kernel, tile0=tile0, with_bf16=with_bf16),
        grid=((r1 - r0) // tt,),
        in_specs=[pl.BlockSpec(memory_space=pl.ANY), row_in, row_in,
                  pl.BlockSpec((tt, TOP_K), lambda i: (tile0 + i, 0)), vec, vec,
                  pl.BlockSpec(memory_space=pl.ANY)],
        out_specs=out_specs,
        out_shape=out_shape,
        scratch_shapes=[pltpu.SMEM((tt * TOP_K // LANES, LANES), I32), pltpu.SMEM((tt * TOP_K // LANES, LANES), I32),
                        pltpu.VMEM((TOP_K, tt * PACK_SUB, LANES), U32), pltpu.VMEM((TOP_K, tt * PACK_SUB, LANES), U32),
                        pltpu.SemaphoreType.DMA((2,)), pltpu.SemaphoreType.DMA],
        compiler_params=_cparams(("arbitrary",)),
        name="combine",
    )(slot_tiles, x, shared, gate, g.reshape(1, d), b.reshape(1, d), ys)


def _moe_layer(x, xp, layer, w_router, b_router, w_gu, w_down, sw_gu, sw_down, ln_g, ln_b, split_rows):
    t, d = x.shape
    bm = EXPERT_ROWS
    idx_t, gate_t, rank_t, counts = _router(x, w_router, b_router)

    counts = counts.reshape(N_EXPERTS).astype(I32)
    padded = (counts + bm - 1) // bm * bm
    pad_end = jnp.cumsum(padded)
    pad_start = pad_end - padded
    n_blocks = (t * TOP_K + N_EXPERTS * (bm - 1) + bm - 1) // bm
    n_slots = n_blocks * bm
    block_first = jnp.arange(n_blocks, dtype=I32) * bm
    block_e = jnp.minimum(jnp.sum((pad_end[None, :] <= block_first[:, None]).astype(I32), axis=1), N_EXPERTS - 1)
    n_used = (pad_end[-1:] // bm).astype(I32)

    slot_t = _slots(idx_t, rank_t, pad_start)
    slot_flat = slot_t.T.reshape(t * TOP_K // LANES, LANES)

    def slot_tiles(tile):
        return slot_flat.reshape(t // tile, tile * TOP_K // LANES, LANES)

    xs = _dispatch(xp, slot_tiles(DISPATCH_TILE), pad_end.astype(I32), padded.astype(I32), n_slots)
    ys = _ffn(xs, w_gu, w_down, layer, block_e, n_used, bm, True, "expert_ffn")
    shared_rows = _pick_tile(t, (512, 256))
    shared = _ffn(xp, sw_gu[:, None], sw_down[:, None], layer, jnp.zeros((t // shared_rows,), I32),
                  jnp.full((1,), t // shared_rows, I32), shared_rows, False, "shared_ffn")
    args = (x, shared, ys, slot_tiles(COMBINE_TILE), gate_t.T, ln_g, ln_b)
    if split_rows is None:
        return _combine(*args, (0, t), True)
    return _combine(*args, (0, split_rows), False)[0], _combine(*args, (split_rows, t), False)[0]


def _rope_angles(s, theta, half):
    inv = theta ** (-jnp.arange(half, dtype=F32) / half)
    ang = jnp.arange(s, dtype=F32)[:, None] * inv[None, :]
    return jnp.cos(ang), jnp.sin(ang)


def _ret_tables(s):
    cos, sin = _rope_angles(s, RET_THETA, RET_DK // 2)
    return cos, sin, sin


def _diff_tables(s):
    half = ROPE_DIMS // 2
    cos, sin = _rope_angles(s, ROPE_THETA, half)
    rest = DIFF_DH - ROPE_DIMS
    zeros_h = jnp.zeros((s, half), F32)
    c = jnp.concatenate([cos, cos, jnp.ones((s, rest), F32)], axis=1)
    s_up = jnp.concatenate([zeros_h, sin, jnp.zeros((s, rest), F32)], axis=1)
    s_dn = jnp.concatenate([-sin, zeros_h, jnp.zeros((s, rest), F32)], axis=1)
    return c, s_up, s_dn


def kernel(x_prompt, x_sample, ret_w_in, ret_decay_f, ret_decay_b, ret_w_out, diff_w_in, diff_lam_q1, diff_lam_k1,
           diff_lam_q2, diff_lam_k2, diff_subln_g, diff_w_out, ln_mix_g, ln_mix_b, router_w, router_b, exp_w_gu,
           exp_w_down, shared_w_gu, shared_w_down, ln_ffn_g, ln_ffn_b):
    bp, sp, d = x_prompt.shape
    bs, ss, _ = x_sample.shape
    tp = bp * sp
    x_rows = [x_prompt.reshape(tp, d), x_sample.reshape(bs * ss, d)]
    proj_rows = [_cast_rows(x_rows)]
    segs = [(0, sp), (tp, ss)]
    s_max = max(sp, ss)

    for i in range(DEPTH):
        j = i // 2
        if i % 2 == 0:
            qk = RET_HEADS * RET_DK
            proj = _proj(proj_rows, ret_w_in[j].astype(BF16), _ret_tables(s_max), segs, mode="ret",
                         n_q_cols=qk, n_qk_cols=2 * qk, q_scale=1.0, k_scale=RET_DK ** -0.5)
            lg_f = -jax.nn.softplus(-ret_decay_f[j].astype(F32))
            lg_b = -jax.nn.softplus(-ret_decay_b[j].astype(F32))
            mixed = _retention(proj, lg_f, lg_b, segs)
            x, xp = _out_ln(mixed, ret_w_out[j].astype(BF16), x_rows, ln_mix_g[i], ln_mix_b[i], "ret_out")
        else:
            lambda_init = 0.8 - 0.6 * math.exp(-0.3 * i)
            nq = DIFF_HEADS * 2 * DIFF_DH
            proj = _proj(proj_rows, diff_w_in[j].astype(BF16), _diff_tables(s_max), segs, mode="diff",
                         n_q_cols=nq, n_qk_cols=2 * nq, q_scale=DIFF_DH ** -0.5 * LOG2E, k_scale=1.0)
            lam = (jnp.exp(jnp.sum(diff_lam_q1[j].astype(F32) * diff_lam_k1[j].astype(F32)))
                   - jnp.exp(jnp.sum(diff_lam_q2[j].astype(F32) * diff_lam_k2[j].astype(F32))) + lambda_init)
            mixed = _diff_attention(proj, lam.reshape(1), diff_subln_g[j],
                                    [(0, sp, bp), (tp, ss, bs)], lambda_init)
            x, xp = _out_ln(mixed, diff_w_out[j].astype(BF16), x_rows, ln_mix_g[i], ln_mix_b[i], "diff_out")
        last = i == DEPTH - 1
        x, xb = _moe_layer(x, xp, i, router_w[i], router_b[i], exp_w_gu, exp_w_down, shared_w_gu, shared_w_down,
                           ln_ffn_g[i], ln_ffn_b[i], tp if last else None)
        x_rows, proj_rows = [x], [xb]
    return x.reshape(bp, sp, d), xb.reshape(bs, ss, d)
```

```python
import functools
import math

import jax
import jax.numpy as jnp
from jax import lax
from jax.experimental import pallas as pl
from jax.experimental.pallas import tpu as pltpu

F32 = jnp.float32
BF16 = jnp.bfloat16
I32 = jnp.int32
U32 = jnp.uint32

D_MODEL = 2048
DEPTH = 2
ALPHA = (2 * DEPTH) ** 0.25
LN_EPS = 1e-5
RET_HEADS = 8
RET_DK = D_MODEL // RET_HEADS
RET_DV = 2 * RET_DK
RET_THETA = 10000.0
DIFF_HEADS = 8
DIFF_DH = D_MODEL // (2 * DIFF_HEADS)
DIFF_DV = 2 * DIFF_DH
ROPE_THETA = 500000.0
ROPE_DIMS = DIFF_DH // 4
N_EXPERTS = 64
TOP_K = 8
N_GROUPS = 8
TOPK_GROUPS = 4
GROUP_SIZE = N_EXPERTS // N_GROUPS
EXPERT_HIDDEN = 512
SHARED_HIDDEN = 512
ROUTED_SCALE = 2.5

LANES = 128
SUBLANES = 8
VMEM_LIMIT = 48 * 1024 * 1024

RET_CHUNK = 256
RET_GROUP = 8
EXPERT_ROWS = 512
DISPATCH_TILE = 512
COMBINE_TILE = 128
ATTN_Q_TILE = 512
NEG_INF = float("-inf")
LOG2E = math.log2(math.e)


def _cparams(sem):
    return pltpu.CompilerParams(dimension_semantics=sem, vmem_limit_bytes=VMEM_LIMIT)


def _pick_tile(n, prefs):
    for p in prefs:
        if n % p == 0:
            return p
    raise ValueError(f"no tile in {prefs} divides {n}")


def _seg_local_block(i, segs_blocks):
    val = None
    for first, per in segs_blocks:
        loc = lax.rem(i - first, per)
        val = loc if val is None else jnp.where(i >= first, loc, val)
    return val


def _proj_kernel(*refs, src_first, mode, n_q_tiles, n_qk_tiles, q_scale, k_scale):
    n_src = len(src_first)
    x_refs = refs[:n_src]
    w_ref, t0_ref, t1_ref, t2_ref, o_ref = refs[n_src:n_src + 5]
    i = pl.program_id(0)
    j = pl.program_id(1)
    if n_src == 1:
        x = x_refs[0][...]
    else:
        xb_ref = refs[n_src + 5]
        for s in range(n_src):
            hi = src_first[s + 1] if s + 1 < n_src else None
            in_src = (i >= src_first[s]) if hi is None else ((i >= src_first[s]) & (i < hi))

            @pl.when((j == 0) & in_src)
            def _(s=s):
                xb_ref[...] = x_refs[s][...].astype(BF16)
        x = xb_ref[...]
    tn = o_ref.shape[1]

    def cols(lo, width):
        return jnp.dot(x, w_ref[:, lo:lo + width], preferred_element_type=F32)

    @pl.when(j >= n_qk_tiles)
    def _plain():
        o_ref[...] = cols(0, tn).astype(o_ref.dtype)

    @pl.when(j < n_qk_tiles)
    def _rope():
        scale = jnp.where(j < n_q_tiles, q_scale, k_scale)
        if mode == "ret":
            cos = t0_ref[...]
            sin = t1_ref[...]
            half = RET_DK // 2
            for hs in range(tn // RET_DK):
                lo = hs * RET_DK
                acc = cols(lo, RET_DK)
                x1 = acc[:, :half]
                x2 = acc[:, half:]
                o_ref[:, lo:lo + half] = ((x1 * cos - x2 * sin) * scale).astype(o_ref.dtype)
                o_ref[:, lo + half:lo + RET_DK] = ((x2 * cos + x1 * sin) * scale).astype(o_ref.dtype)
        else:
            c = t0_ref[...]
            s_up = t1_ref[...]
            s_dn = t2_ref[...]
            half = ROPE_DIMS // 2
            group = 2 * DIFF_DH
            for hs in range(tn // group):
                acc = cols(hs * group, group)
                for g in range(group // DIFF_DH):
                    lo = hs * group + g * DIFF_DH
                    seg = acc[:, g * DIFF_DH:(g + 1) * DIFF_DH]
                    rot = seg * c + pltpu.roll(seg, half, 1) * s_up + pltpu.roll(seg, DIFF_DH - half, 1) * s_dn
                    o_ref[:, lo:lo + DIFF_DH] = (rot * scale).astype(o_ref.dtype)


def _row_sources(sources, tm, index_of, **spec_kwargs):
    firsts, specs, first = [], [], 0
    for src in sources:
        nblk = src.shape[0] // tm
        firsts.append(first)
        specs.append(pl.BlockSpec((tm, src.shape[1]),
                                  lambda *g, first=first, nblk=nblk: (jnp.clip(index_of(*g) - first, 0, nblk - 1), 0),
                                  **spec_kwargs))
        first += nblk
    return tuple(firsts), specs, first


def _proj(sources, w, tables, segs, *, mode, n_q_cols, n_qk_cols, q_scale, k_scale):
    k = sources[0].shape[1]
    n = w.shape[1]
    seg_len = [s for _, s in segs]
    tm = _pick_tile(math.gcd(*seg_len), (1024, 512, 256))
    tn = 2048 if len(sources) == 1 else 1024
    segs_blocks = [(start // tm, s // tm) for start, s in segs]
    tw = tables[0].shape[1]
    tab_spec = pl.BlockSpec((tm, tw), lambda i, j: (_seg_local_block(i, segs_blocks), 0))
    kwargs = {"pipeline_mode": pl.Buffered(1)} if len(sources) > 1 else {}
    src_first, src_specs, n_row_blocks = _row_sources(sources, tm, lambda i, j: i, **kwargs)
    kern = functools.partial(_proj_kernel, src_first=src_first, mode=mode, n_q_tiles=n_q_cols // tn,
                             n_qk_tiles=n_qk_cols // tn, q_scale=q_scale, k_scale=k_scale)
    return pl.pallas_call(
        kern,
        grid=(n_row_blocks, n // tn),
        in_specs=src_specs + [pl.BlockSpec((k, tn), lambda i, j: (0, j)), tab_spec, tab_spec, tab_spec],
        out_specs=pl.BlockSpec((tm, tn), lambda i, j: (i, j)),
        out_shape=jax.ShapeDtypeStruct((n_row_blocks * tm, n), BF16),
        scratch_shapes=[pltpu.VMEM((tm, k), BF16)] if len(sources) > 1 else [],
        compiler_params=_cparams(("arbitrary", "arbitrary")),
        name=f"proj_{mode}",
    )(*sources, w, *tables)


HIGH_HALF = 0xFFFF0000


def _pack_rows(y):
    n = y.shape[1] // 2
    lo = lax.bitcast_convert_type(y[:, :n].astype(BF16).astype(F32), U32) >> 16
    hi = lax.bitcast_convert_type(y[:, n:].astype(BF16).astype(F32), U32) & jnp.uint32(HIGH_HALF)
    return lo | hi


def _unpack_rows(w):
    lo = lax.bitcast_convert_type(w << 16, F32)
    hi = lax.bitcast_convert_type(w & jnp.uint32(HIGH_HALF), F32)
    return lo, hi


PACK_WORDS = D_MODEL // 2
PACK_SUB = PACK_WORDS // LANES
assert PACK_SUB == SUBLANES


def _tile_of(row):
    start = row * PACK_SUB
    return pl.ds(start if isinstance(start, int) else pl.multiple_of(start, PACK_SUB), PACK_SUB)


def _store_packed(ref, words):
    r = words.shape[0]
    for s in range(PACK_SUB):
        ref[pl.ds(s, r, stride=PACK_SUB), :] = words[:, s * LANES:(s + 1) * LANES]


def _load_packed(ref, r):
    return jnp.concatenate([ref[pl.ds(s, r, stride=PACK_SUB), :] for s in range(PACK_SUB)], axis=1)


def _layer_norm_rows(z, g, b):
    mu = jnp.mean(z, axis=-1, keepdims=True)
    zc = z - mu
    var = jnp.mean(zc * zc, axis=-1, keepdims=True)
    return zc * lax.rsqrt(var + LN_EPS) * g + b


def _out_ln_kernel(a_ref, w_ref, g_ref, b_ref, *refs, src_first):
    n_src = len(src_first)
    x_refs = refs[:n_src]
    o_ref, op_ref = refs[n_src:]
    i = pl.program_id(0)
    h = jnp.dot(a_ref[...], w_ref[...], preferred_element_type=F32)

    def finish(x_ref):
        y = _layer_norm_rows(ALPHA * x_ref[...] + h, g_ref[...], b_ref[...])
        o_ref[...] = y
        _store_packed(op_ref, _pack_rows(y))

    if n_src == 1:
        finish(x_refs[0])
    else:
        for s in range(n_src):
            hi = src_first[s + 1] if s + 1 < n_src else None
            in_src = (i >= src_first[s]) if hi is None else ((i >= src_first[s]) & (i < hi))

            @pl.when(in_src)
            def _(s=s):
                finish(x_refs[s])


def _cast_rows_kernel(*refs, src_first):
    n_src = len(src_first)
    o_ref = refs[n_src]
    i = pl.program_id(0)
    for s in range(n_src):
        hi = src_first[s + 1] if s + 1 < n_src else None
        in_src = (i >= src_first[s]) if hi is None else ((i >= src_first[s]) & (i < hi))

        @pl.when(in_src)
        def _(s=s):
            o_ref[...] = refs[s][...].astype(o_ref.dtype)


def _cast_rows(sources):
    d = sources[0].shape[1]
    tm = 512
    src_first, src_specs, n_row_blocks = _row_sources(sources, tm, lambda i: i)
    return pl.pallas_call(
        functools.partial(_cast_rows_kernel, src_first=src_first),
        grid=(n_row_blocks,),
        in_specs=src_specs,
        out_specs=pl.BlockSpec((tm, d), lambda i: (i, 0)),
        out_shape=jax.ShapeDtypeStruct((n_row_blocks * tm, d), BF16),
        compiler_params=_cparams(("arbitrary",)),
        name="cast_rows",
    )(*sources)


def _out_ln(a, w, x_sources, g, b, name):
    m, kk = a.shape
    d = w.shape[1]
    tm = 512 if kk * d * 2 <= 8 * 1024 * 1024 else 256
    row = pl.BlockSpec((tm, d), lambda i: (i, 0))
    vec = pl.BlockSpec((1, d), lambda i: (0, 0))
    src_first, src_specs, n_row_blocks = _row_sources(x_sources, tm, lambda i: i)
    assert n_row_blocks * tm == m
    return pl.pallas_call(
        functools.partial(_out_ln_kernel, src_first=src_first),
        grid=(m // tm,),
        in_specs=[pl.BlockSpec((tm, kk), lambda i: (i, 0)),
                  pl.BlockSpec((kk, d), lambda i: (0, 0), pipeline_mode=pl.Buffered(1)), vec, vec] + src_specs,
        out_specs=[row, pl.BlockSpec((tm * PACK_SUB, LANES), lambda i: (i, 0))],
        out_shape=[jax.ShapeDtypeStruct((m, d), F32), jax.ShapeDtypeStruct((m * PACK_SUB, LANES), U32)],
        compiler_params=_cparams(("arbitrary",)),
        name=name,
    )(a, w, g.reshape(1, d), b.reshape(1, d), *x_sources)


def _ret_kernel(lg_ref, cd_ref, q_ref, k_ref, v_ref, *rest, backward, segs_chunks):
    if backward:
        of_ref, g_ref, o_ref, state_ref, dmat_ref, qd_ref, kd_ref = rest
    else:
        o_ref, state_ref, dmat_ref, qd_ref, kd_ref = rest
    h = pl.program_id(0)
    c = pl.program_id(1)
    nc = pl.num_programs(1)
    cc = (nc - 1 - c) if backward else c
    lg = lg_ref[h]
    cs = RET_CHUNK

    @pl.when(c == 0)
    def _tables():
        ii = lax.broadcasted_iota(I32, (cs, cs), 0)
        jj = lax.broadcasted_iota(I32, (cs, cs), 1)
        if backward:
            mask = jj > ii
            dist = (jj - ii).astype(F32)
        else:
            mask = ii >= jj
            dist = (ii - jj).astype(F32)
        dmat_ref[...] = jnp.where(mask, jnp.exp(jnp.where(mask, dist, 0.0) * lg), 0.0)
        pos = lax.broadcasted_iota(I32, (cs, 1), 0).astype(F32)
        if backward:
            qd_ref[...] = jnp.exp((cs - pos) * lg)
            kd_ref[...] = jnp.exp(pos * lg)
        else:
            qd_ref[...] = jnp.exp((pos + 1.0) * lg)
            kd_ref[...] = jnp.exp((cs - 1.0 - pos) * lg)

    loc = _seg_local_block(cc, segs_chunks)
    if backward:
        per = None
        for first, p in segs_chunks:
            per = p if per is None else jnp.where(cc >= first, p, per)
        is_start = loc == per - 1
    else:
        is_start = loc == 0

    @pl.when(is_start)
    def _reset():
        state_ref[...] = jnp.zeros_like(state_ref)

    state = state_ref[...]
    n_sub = q_ref.shape[0] // cs
    for sub in (range(n_sub - 1, -1, -1) if backward else range(n_sub)):
        rows = pl.ds(sub * cs, cs)
        q = q_ref[rows, :]
        k = k_ref[rows, :]
        v = v_ref[rows, :]
        att = lax.dot_general(q, k, (((1,), (1,)), ((), ())), preferred_element_type=F32) * dmat_ref[...]
        inner = jnp.dot(att.astype(BF16), v, preferred_element_type=F32)
        cross = jnp.dot(q, state.astype(BF16), preferred_element_type=F32) * qd_ref[...]
        o = inner + cross
        kt = (k.astype(F32) * kd_ref[...]).T.astype(BF16)
        state = state * cd_ref[h] + jnp.dot(kt, v, preferred_element_type=F32)

        if backward:
            tot = of_ref[rows, :] + o
            mu = jnp.mean(tot, axis=-1, keepdims=True)
            tc = tot - mu
            var = jnp.mean(tc * tc, axis=-1, keepdims=True)
            on = tc * lax.rsqrt(var + LN_EPS)
            g = g_ref[rows, :].astype(F32)
            o_ref[rows, :] = ((g * (1.0 / (1.0 + jnp.exp(-g)))) * on).astype(o_ref.dtype)
        else:
            o_ref[rows, :] = o
    state_ref[...] = state


def _retention(proj, lg_f, lg_b, segs):
    t = proj.shape[0]
    cs = RET_CHUNK
    rb = RET_CHUNK * RET_GROUP
    nc = t // rb
    segs_chunks = [(start // rb, s // rb) for start, s in segs]
    qk_blocks = (RET_HEADS * RET_DK) // RET_DK
    v_blocks0 = (2 * RET_HEADS * RET_DK) // RET_DV
    g_blocks0 = v_blocks0 + RET_HEADS
    smem = pl.BlockSpec(memory_space=pltpu.SMEM)
    scratch = [pltpu.VMEM((RET_DK, RET_DV), F32), pltpu.VMEM((cs, cs), F32),
               pltpu.VMEM((cs, 1), F32), pltpu.VMEM((cs, 1), F32)]

    def run(backward, lg, extra):
        def cmap(c):
            return (nc - 1 - c) if backward else c
        in_specs = [smem, smem,
                    pl.BlockSpec((rb, RET_DK), lambda h, c: (cmap(c), h)),
                    pl.BlockSpec((rb, RET_DK), lambda h, c: (cmap(c), qk_blocks + h)),
                    pl.BlockSpec((rb, RET_DV), lambda h, c: (cmap(c), v_blocks0 + h))]
        args = [lg, jnp.exp(cs * lg), proj, proj, proj]
        if backward:
            in_specs += [pl.BlockSpec((rb, RET_DV), lambda h, c: (cmap(c), h)),
                         pl.BlockSpec((rb, RET_DV), lambda h, c: (cmap(c), g_blocks0 + h))]
            args += [extra, proj]
        return pl.pallas_call(
            functools.partial(_ret_kernel, backward=backward, segs_chunks=segs_chunks),
            grid=(RET_HEADS, nc),
            in_specs=in_specs,
            out_specs=pl.BlockSpec((rb, RET_DV), lambda h, c: (cmap(c), h)),
            out_shape=jax.ShapeDtypeStruct((t, RET_HEADS * RET_DV), BF16 if backward else F32),
            scratch_shapes=scratch,
            compiler_params=_cparams(("arbitrary", "arbitrary")),
            name="retention_bwd" if backward else "retention_fwd",
        )(*args)

    o_f = run(False, lg_f, None)
    return run(True, lg_b, o_f)


def _attn_kernel(lam_ref, q_ref, k_ref, v_ref, sg_ref, *rest, out_scale):
    o_ref = rest[-1]
    probs = []
    for c in range(2):
        q = q_ref[:, c * DIFF_DH:(c + 1) * DIFF_DH]
        k = k_ref[:, c * DIFF_DH:(c + 1) * DIFF_DH]
        s = lax.dot_general(q, k, (((1,), (1,)), ((), ())), preferred_element_type=F32)
        p = jnp.exp2(s - jnp.max(s, axis=-1, keepdims=True))
        probs.append((p, jnp.sum(p, axis=-1, keepdims=True)))
    (p0, den0), (p1, den1) = probs
    a = p0 * (1.0 / den0) - p1 * (lam_ref[0] / den1)
    o = jnp.dot(a.astype(BF16), v_ref[...], preferred_element_type=F32)
    ms = jnp.mean(o * o, axis=-1, keepdims=True)
    o_ref[...] = (o * lax.rsqrt(ms + LN_EPS) * sg_ref[...] * out_scale).astype(o_ref.dtype)


def _diff_attention(proj, lam, subln_g, segs, lambda_init):
    nq_blocks = DIFF_HEADS
    t = proj.shape[0]
    out = None
    for start, s, n_seq in segs:
        tq = ATTN_Q_TILE
        assert s % tq == 0 and start % s == 0
        qb0 = start // tq
        kb0 = start // s
        nqt = s // tq
        in_specs = [pl.BlockSpec(memory_space=pltpu.SMEM),
                    pl.BlockSpec((tq, 2 * DIFF_DH), lambda b, h, i: (qb0 + b * nqt + i, h)),
                    pl.BlockSpec((s, 2 * DIFF_DH), lambda b, h, i: (kb0 + b, nq_blocks + h)),
                    pl.BlockSpec((s, DIFF_DV), lambda b, h, i: (kb0 + b, 2 * nq_blocks + h)),
                    pl.BlockSpec((1, DIFF_DV), lambda b, h, i: (0, 0))]
        args = [lam, proj, proj, proj, subln_g.reshape(1, DIFF_DV)]
        aliases = {}
        if out is not None:
            in_specs.append(pl.BlockSpec(memory_space=pl.ANY))
            args.append(out)
            aliases = {len(args) - 1: 0}
        out = pl.pallas_call(
            functools.partial(_attn_kernel, out_scale=1.0 - lambda_init),
            grid=(n_seq, DIFF_HEADS, nqt),
            in_specs=in_specs,
            out_specs=pl.BlockSpec((tq, DIFF_DV), lambda b, h, i: (qb0 + b * nqt + i, h)),
            out_shape=jax.ShapeDtypeStruct((t, DIFF_HEADS * DIFF_DV), BF16),
            input_output_aliases=aliases,
            compiler_params=_cparams(("arbitrary", "arbitrary", "arbitrary")),
            name=f"diff_attn_s{s}",
        )(*args)
    return out


def _first_max(cur, rows, n_rows):
    m = jnp.max(cur, axis=0, keepdims=True)
    idx = jnp.min(jnp.where(cur == m, rows, float(n_rows)), axis=0, keepdims=True)
    return m, idx


def _router_kernel(x_ref, wt_ref, b_ref, idx_ref, gate_ref, rank_ref, cnt_ref, carry_ref):
    i = pl.program_id(0)

    @pl.when(i == 0)
    def _init():
        carry_ref[...] = jnp.zeros_like(carry_ref)

    tn = x_ref.shape[0]
    x = x_ref[...]
    xh = x.astype(BF16)
    xl = (x - xh.astype(F32)).astype(BF16)
    w = wt_ref[...]
    wh = w.astype(BF16)
    wl = (w - wh.astype(F32)).astype(BF16)

    def nt(a, b):
        return lax.dot_general(a, b, (((1,), (1,)), ((), ())), preferred_element_type=F32)

    logits = nt(wh, xh) + nt(wh, xl) + nt(wl, xh)
    scores = 1.0 / (1.0 + jnp.exp(-logits))
    biased = scores + b_ref[...]

    rows_g = lax.broadcasted_iota(I32, (GROUP_SIZE, tn), 0).astype(F32)
    rows_e = lax.broadcasted_iota(I32, (N_EXPERTS, tn), 0).astype(F32)

    gscore = []
    for g in range(N_GROUPS):
        slab = biased[g * GROUP_SIZE:(g + 1) * GROUP_SIZE]
        m1, i1 = _first_max(slab, rows_g, GROUP_SIZE)
        m2 = jnp.max(jnp.where(rows_g == i1, NEG_INF, slab), axis=0, keepdims=True)
        gscore.append(m1 + m2)
    cur = jnp.concatenate(gscore, axis=0)
    rows_grp = lax.broadcasted_iota(I32, (N_GROUPS, tn), 0).astype(F32)
    gsel = jnp.zeros((N_GROUPS, tn), F32)
    for _ in range(TOPK_GROUPS):
        _, gi = _first_max(cur, rows_grp, N_GROUPS)
        pick = rows_grp == gi
        gsel = jnp.where(pick, 1.0, gsel)
        cur = jnp.where(pick, NEG_INF, cur)

    masked = jnp.concatenate(
        [jnp.where(gsel[g:g + 1] > 0.0, biased[g * GROUP_SIZE:(g + 1) * GROUP_SIZE], NEG_INF)
         for g in range(N_GROUPS)], axis=0)

    sel = jnp.zeros((N_EXPERTS, tn), F32)
    picks = []
    cur = masked
    for _ in range(TOP_K):
        _, ei = _first_max(cur, rows_e, N_EXPERTS)
        pick = rows_e == ei
        sel = jnp.where(pick, 1.0, sel)
        cur = jnp.where(pick, NEG_INF, cur)
        picks.append(ei)

    gsum = jnp.sum(scores * sel, axis=0, keepdims=True)

    ta = lax.broadcasted_iota(I32, (tn, tn), 0)
    tb = lax.broadcasted_iota(I32, (tn, tn), 1)
    before = jnp.where(ta < tb, 1.0, 0.0).astype(BF16)
    rank = jnp.dot(sel.astype(BF16), before, preferred_element_type=F32) + carry_ref[...]
    carry_ref[...] = carry_ref[...] + jnp.sum(sel, axis=1, keepdims=True)
    cnt_ref[...] = carry_ref[...]

    gates = []
    ranks = []
    for ei in picks:
        hit = rows_e == ei
        gates.append(jnp.sum(jnp.where(hit, scores, 0.0), axis=0, keepdims=True) / gsum * ROUTED_SCALE)
        ranks.append(jnp.sum(jnp.where(hit, rank, 0.0), axis=0, keepdims=True))
    idx_ref[...] = jnp.concatenate(picks, axis=0).astype(I32)
    gate_ref[...] = jnp.concatenate(gates, axis=0)
    rank_ref[...] = jnp.concatenate(ranks, axis=0).astype(I32)


def _router(x, w_router, b_router):
    t, d = x.shape
    tn = _pick_tile(t, (512, 256))
    kt = pl.BlockSpec((TOP_K, tn), lambda i: (0, i))
    return pl.pallas_call(
        _router_kernel,
        grid=(t // tn,),
        in_specs=[pl.BlockSpec((tn, d), lambda i: (i, 0)),
                  pl.BlockSpec((N_EXPERTS, d), lambda i: (0, 0)),
                  pl.BlockSpec((N_EXPERTS, 1), lambda i: (0, 0))],
        out_specs=[kt, kt, kt, pl.BlockSpec((N_EXPERTS, 1), lambda i: (0, 0))],
        out_shape=[jax.ShapeDtypeStruct((TOP_K, t), I32), jax.ShapeDtypeStruct((TOP_K, t), F32),
                   jax.ShapeDtypeStruct((TOP_K, t), I32), jax.ShapeDtypeStruct((N_EXPERTS, 1), F32)],
        scratch_shapes=[pltpu.VMEM((N_EXPERTS, 1), F32)],
        compiler_params=_cparams(("arbitrary",)),
        name="router",
    )(x, w_router.T, b_router.reshape(N_EXPERTS, 1))


def _slot_kernel(idx_ref, rank_ref, start_ref, slot_ref):
    tn = idx_ref.shape[1]
    rows_e = lax.broadcasted_iota(I32, (N_EXPERTS, tn), 0)
    out = []
    for k in range(TOP_K):
        hit = rows_e == idx_ref[k:k + 1, :]
        base = jnp.sum(jnp.where(hit, start_ref[...], 0.0), axis=0, keepdims=True)
        out.append(base.astype(I32) + rank_ref[k:k + 1, :])
    slot_ref[...] = jnp.concatenate(out, axis=0)


def _slots(idx_t, rank_t, pad_start):
    t = idx_t.shape[1]
    tn = _pick_tile(t, (512, 256))
    kt = pl.BlockSpec((TOP_K, tn), lambda i: (0, i))
    return pl.pallas_call(
        _slot_kernel,
        grid=(t // tn,),
        in_specs=[kt, kt, pl.BlockSpec((N_EXPERTS, 1), lambda i: (0, 0))],
        out_specs=kt,
        out_shape=jax.ShapeDtypeStruct((TOP_K, t), I32),
        compiler_params=_cparams(("arbitrary",)),
        name="slots",
    )(idx_t, rank_t, pad_start.astype(F32).reshape(N_EXPERTS, 1))


def _row_copy(src, src_row, dst, dst_row, sem):
    return pltpu.make_async_copy(src.at[_tile_of(src_row)], dst.at[_tile_of(dst_row)], sem)


def _dispatch_kernel(pad_end_ref, padded_ref, slot_hbm, x_ref, xs_hbm, slot_smem, zero_ref, sem, slot_sem):
    i = pl.program_id(0)
    tt = x_ref.shape[0] // PACK_SUB
    bm = zero_ref.shape[0] // PACK_SUB
    slot_cp = pltpu.make_async_copy(slot_hbm.at[i], slot_smem, slot_sem)
    slot_cp.start()

    @pl.when(i == 0)
    def _clear():
        zero_ref[...] = jnp.zeros_like(zero_ref)

        def pad_copy(e):
            off = pl.multiple_of((pad_end_ref[e] - bm) * PACK_SUB, bm * PACK_SUB)
            return pltpu.make_async_copy(zero_ref, xs_hbm.at[pl.ds(off, bm * PACK_SUB)], sem)

        def start(e, carry):
            @pl.when(padded_ref[e] > 0)
            def _():
                pad_copy(e).start()
            return carry

        def wait(e, carry):
            @pl.when(padded_ref[e] > 0)
            def _():
                pad_copy(e).wait()
            return carry

        lax.fori_loop(0, N_EXPERTS, start, 0)
        lax.fori_loop(0, N_EXPERTS, wait, 0)

    slot_cp.wait()
    per_row = LANES // TOP_K

    def issue(r, carry):
        for j in range(LANES):
            _row_copy(x_ref, r * per_row + j // TOP_K, xs_hbm, slot_smem[r, j], sem).start(priority=j % 2)
        return carry

    lax.fori_loop(0, tt // per_row, issue, 0)

    def drain(t, carry):
        for k in range(TOP_K):
            _row_copy(x_ref, t, xs_hbm, 0, sem).wait()
        return carry

    lax.fori_loop(0, tt, drain, 0)


def _dispatch(xp, slot_tiles, pad_end, padded, n_slots):
    t = xp.shape[0] // PACK_SUB
    tt = DISPATCH_TILE
    grid_spec = pltpu.PrefetchScalarGridSpec(
        num_scalar_prefetch=2,
        grid=(t // tt,),
        in_specs=[pl.BlockSpec(memory_space=pl.ANY),
                  pl.BlockSpec((tt * PACK_SUB, LANES), lambda i, pe, pd: (i, 0))],
        out_specs=pl.BlockSpec(memory_space=pl.ANY),
        scratch_shapes=[pltpu.SMEM((tt * TOP_K // LANES, LANES), I32),
                        pltpu.VMEM((EXPERT_ROWS * PACK_SUB, LANES), U32),
                        pltpu.SemaphoreType.DMA, pltpu.SemaphoreType.DMA],
    )
    return pl.pallas_call(
        _dispatch_kernel,
        grid_spec=grid_spec,
        out_shape=jax.ShapeDtypeStruct((n_slots * PACK_SUB, LANES), U32),
        compiler_params=_cparams(("arbitrary",)),
        name="dispatch",
    )(pad_end, padded, slot_tiles, xp)


def _ffn_kernel(block_e_ref, n_used_ref, x_ref, wgu_ref, wd_ref, o_ref, wgu_bf, wd_bf, *, pack_out):
    b = pl.program_id(0)

    @pl.when(b < n_used_ref[0])
    def _():
        e = block_e_ref[b]
        e_prev = block_e_ref[jnp.maximum(b - 1, 0)]

        @pl.when((b == 0) | (e != e_prev))
        def _():
            wgu_bf[...] = wgu_ref[0, 0].astype(BF16)
            wd_bf[...] = wd_ref[0, 0].astype(BF16)

        hidden = wd_bf.shape[0]
        rows = x_ref.shape[0] // PACK_SUB
        lo, hi = _unpack_rows(_load_packed(x_ref, rows))
        gu = (jnp.dot(lo.astype(BF16), wgu_bf[:PACK_WORDS], preferred_element_type=F32)
              + jnp.dot(hi.astype(BF16), wgu_bf[PACK_WORDS:], preferred_element_type=F32))
        gate = gu[:, :hidden]
        act = gate * (1.0 / (1.0 + jnp.exp(-gate))) * gu[:, hidden:]
        y = jnp.dot(act.astype(BF16), wd_bf[...], preferred_element_type=F32)
        if pack_out:
            _store_packed(o_ref, _pack_rows(y))
        else:
            o_ref[...] = y


def _ffn(xp, w_gu, w_down, layer, block_e, n_used, rows, pack_out, name):
    n = xp.shape[0] // PACK_SUB
    d = 2 * PACK_WORDS
    hidden = w_down.shape[2]

    def row_map(b, be, nu):
        return (jnp.minimum(b, nu[0] - 1), 0)

    packed_rows = pl.BlockSpec((rows * PACK_SUB, LANES), row_map)
    if pack_out:
        out_spec, out_shape = packed_rows, jax.ShapeDtypeStruct((n * PACK_SUB, LANES), U32)
    else:
        out_spec, out_shape = pl.BlockSpec((rows, d), row_map), jax.ShapeDtypeStruct((n, d), F32)
    grid_spec = pltpu.PrefetchScalarGridSpec(
        num_scalar_prefetch=2,
        grid=(n // rows,),
        in_specs=[packed_rows,
                  pl.BlockSpec((1, 1, d, 2 * hidden), lambda b, be, nu: (layer, be[b], 0, 0)),
                  pl.BlockSpec((1, 1, hidden, d), lambda b, be, nu: (layer, be[b], 0, 0))],
        out_specs=out_spec,
        scratch_shapes=[pltpu.VMEM((d, 2 * hidden), BF16), pltpu.VMEM((hidden, d), BF16)],
    )
    return pl.pallas_call(
        functools.partial(_ffn_kernel, pack_out=pack_out),
        grid_spec=grid_spec,
        out_shape=out_shape,
        compiler_params=_cparams(("arbitrary",)),
        name=name,
    )(block_e, n_used, xp, w_gu, w_down)


def _combine_kernel(slot_hbm, x_ref, sh_ref, gate_ref, g_ref, b_ref, ys_hbm, o_ref, *rest, tile0, with_bf16):
    if with_bf16:
        ob_ref, *rest = rest
    slots0, slots1, buf0, buf1, sems, slot_sem = rest
    slot_smem = (slots0, slots1)
    bufs = (buf0, buf1)
    i = pl.program_id(0)
    n = pl.num_programs(0)
    tt = x_ref.shape[0]
    per_row = LANES // TOP_K
    groups = tt // per_row

    def load_slots(tile, par):
        slot_cp = pltpu.make_async_copy(slot_hbm.at[tile0 + tile], slot_smem[par], slot_sem)
        slot_cp.start()
        slot_cp.wait()

    def issue_group(r, par):
        for j in range(LANES):
            _row_copy(ys_hbm, slot_smem[par][r, j], bufs[par].at[j % TOP_K], r * per_row + j // TOP_K,
                      sems.at[par]).start(priority=j % 2)

    def drain(par):
        def body(t, carry):
            for k in range(TOP_K):
                _row_copy(ys_hbm, 0, bufs[par].at[k], t, sems.at[par]).wait()
            return carry
        lax.fori_loop(0, tt, body, 0)

    def reduce_group(r, par):
        rows = slice(r * per_row, (r + 1) * per_row)
        sh = sh_ref[rows, :]
        h_lo = sh[:, :PACK_WORDS]
        h_hi = sh[:, PACK_WORDS:]
        gate = gate_ref[rows, :]
        for k in range(TOP_K):
            words = jnp.concatenate(
                [bufs[par][k, pl.ds(r * per_row * PACK_SUB + s, per_row, stride=PACK_SUB), :]
                 for s in range(PACK_SUB)], axis=1)
            lo, hi = _unpack_rows(words)
            h_lo = h_lo + lo * gate[:, k:k + 1]
            h_hi = h_hi + hi * gate[:, k:k + 1]
        h = jnp.concatenate([h_lo, h_hi], axis=1)
        y = _layer_norm_rows(ALPHA * x_ref[rows, :] + h, g_ref[...], b_ref[...])
        o_ref[rows, :] = y
        if with_bf16:
            ob_ref[rows, :] = y.astype(BF16)

    @pl.when(i == 0)
    def _():
        load_slots(0, 0)
        for r in range(groups):
            issue_group(r, 0)

    nxt = jnp.minimum(i + 1, n - 1)
    for par in range(2):
        @pl.when(i % 2 == par)
        def _(par=par):
            drain(par)
            load_slots(nxt, 1 - par)
            for r in range(groups):
                issue_group(r, 1 - par)
                reduce_group(r, par)

            @pl.when(i == n - 1)
            def _():
                drain(1 - par)


def _combine(x, shared, ys, slot_tiles, gate, g, b, row_range, with_bf16):
    d = x.shape[1]
    tt = COMBINE_TILE
    r0, r1 = row_range
    tile0 = r0 // tt
    row_in = pl.BlockSpec((tt, d), lambda i: (tile0 + i, 0))
    row_out = pl.BlockSpec((tt, d), lambda i: (i, 0))
    vec = pl.BlockSpec((1, d), lambda i: (0, 0))
    out_specs = [row_out]
    out_shape = [jax.ShapeDtypeStruct((r1 - r0, d), F32)]
    if with_bf16:
        out_specs.append(row_out)
        out_shape.append(jax.ShapeDtypeStruct((r1 - r0, d), BF16))
    return pl.pallas_call(
        functools.partial(_combine_kernel, tile0=tile0, with_bf16=with_bf16),
        grid=((r1 - r0) // tt,),
        in_specs=[pl.BlockSpec(memory_space=pl.ANY), row_in, row_in,
                  pl.BlockSpec((tt, TOP_K), lambda i: (tile0 + i, 0)), vec, vec,
                  pl.BlockSpec(memory_space=pl.ANY)],
        out_specs=out_specs,
        out_shape=out_shape,
        scratch_shapes=[pltpu.SMEM((tt * TOP_K // LANES, LANES), I32), pltpu.SMEM((tt * TOP_K // LANES, LANES), I32),
                        pltpu.VMEM((TOP_K, tt * PACK_SUB, LANES), U32), pltpu.VMEM((TOP_K, tt * PACK_SUB, LANES), U32),
                        pltpu.SemaphoreType.DMA((2,)), pltpu.SemaphoreType.DMA],
        compiler_params=_cparams(("arbitrary",)),
        name="combine",
    )(slot_tiles, x, shared, gate, g.reshape(1, d), b.reshape(1, d), ys)


def _moe_layer(x, xp, layer, w_router, b_router, w_gu, w_down, sw_gu, sw_down, ln_g, ln_b, split_rows):
    t, d = x.shape
    bm = EXPERT_ROWS
    idx_t, gate_t, rank_t, counts = _router(x, w_router, b_router)

    counts = counts.reshape(N_EXPERTS).astype(I32)
    padded = (counts + bm - 1) // bm * bm
    pad_end = jnp.cumsum(padded)
    pad_start = pad_end - padded
    n_blocks = (t * TOP_K + N_EXPERTS * (bm - 1) + bm - 1) // bm
    n_slots = n_blocks * bm
    block_first = jnp.arange(n_blocks, dtype=I32) * bm
    block_e = jnp.minimum(jnp.sum((pad_end[None, :] <= block_first[:, None]).astype(I32), axis=1), N_EXPERTS - 1)
    n_used = (pad_end[-1:] // bm).astype(I32)

    slot_t = _slots(idx_t, rank_t, pad_start)
    slot_flat = slot_t.T.reshape(t * TOP_K // LANES, LANES)

    def slot_tiles(tile):
        return slot_flat.reshape(t // tile, tile * TOP_K // LANES, LANES)

    xs = _dispatch(xp, slot_tiles(DISPATCH_TILE), pad_end.astype(I32), padded.astype(I32), n_slots)
    ys = _ffn(xs, w_gu, w_down, layer, block_e, n_used, bm, True, "expert_ffn")
    shared_rows = _pick_tile(t, (512, 256))
    shared = _ffn(xp, sw_gu[:, None], sw_down[:, None], layer, jnp.zeros((t // shared_rows,), I32),
                  jnp.full((1,), t // shared_rows, I32), shared_rows, False, "shared_ffn")
    args = (x, shared, ys, slot_tiles(COMBINE_TILE), gate_t.T, ln_g, ln_b)
    if split_rows is None:
        return _combine(*args, (0, t), True)
    return _combine(*args, (0, split_rows), False)[0], _combine(*args, (split_rows, t), False)[0]


def _rope_angles(s, theta, half):
    inv = theta ** (-jnp.arange(half, dtype=F32) / half)
    ang = jnp.arange(s, dtype=F32)[:, None] * inv[None, :]
    return jnp.cos(ang), jnp.sin(ang)


def _ret_tables(s):
    cos, sin = _rope_angles(s, RET_THETA, RET_DK // 2)
    return cos, sin, sin


def _diff_tables(s):
    half = ROPE_DIMS // 2
    cos, sin = _rope_angles(s, ROPE_THETA, half)
    rest = DIFF_DH - ROPE_DIMS
    zeros_h = jnp.zeros((s, half), F32)
    c = jnp.concatenate([cos, cos, jnp.ones((s, rest), F32)], axis=1)
    s_up = jnp.concatenate([zeros_h, sin, jnp.zeros((s, rest), F32)], axis=1)
    s_dn = jnp.concatenate([-sin, zeros_h, jnp.zeros((s, rest), F32)], axis=1)
    return c, s_up, s_dn


def kernel(x_prompt, x_sample, ret_w_in, ret_decay_f, ret_decay_b, ret_w_out, diff_w_in, diff_lam_q1, diff_lam_k1,
           diff_lam_q2, diff_lam_k2, diff_subln_g, diff_w_out, ln_mix_g, ln_mix_b, router_w, router_b, exp_w_gu,
           exp_w_down, shared_w_gu, shared_w_down, ln_ffn_g, ln_ffn_b):
    bp, sp, d = x_prompt.shape
    bs, ss, _ = x_sample.shape
    tp = bp * sp
    x_rows = [x_prompt.reshape(tp, d), x_sample.reshape(bs * ss, d)]
    proj_rows = [_cast_rows(x_rows)]
    segs = [(0, sp), (tp, ss)]
    s_max = max(sp, ss)

    for i in range(DEPTH):
        j = i // 2
        if i % 2 == 0:
            qk = RET_HEADS * RET_DK
            proj = _proj(proj_rows, ret_w_in[j].astype(BF16), _ret_tables(s_max), segs, mode="ret",
                         n_q_cols=qk, n_qk_cols=2 * qk, q_scale=1.0, k_scale=RET_DK ** -0.5)
            lg_f = -jax.nn.softplus(-ret_decay_f[j].astype(F32))
            lg_b = -jax.nn.softplus(-ret_decay_b[j].astype(F32))
            mixed = _retention(proj, lg_f, lg_b, segs)
            x, xp = _out_ln(mixed, ret_w_out[j].astype(BF16), x_rows, ln_mix_g[i], ln_mix_b[i], "ret_out")
        else:
            lambda_init = 0.8 - 0.6 * math.exp(-0.3 * i)
            nq = DIFF_HEADS * 2 * DIFF_DH
            proj = _proj(proj_rows, diff_w_in[j].astype(BF16), _diff_tables(s_max), segs, mode="diff",
                         n_q_cols=nq, n_qk_cols=2 * nq, q_scale=DIFF_DH ** -0.5 * LOG2E, k_scale=1.0)
            lam = (jnp.exp(jnp.sum(diff_lam_q1[j].astype(F32) * diff_lam_k1[j].astype(F32)))
                   - jnp.exp(jnp.sum(diff_lam_q2[j].astype(F32) * diff_lam_k2[j].astype(F32))) + lambda_init)
            mixed = _diff_attention(proj, lam.reshape(1), diff_subln_g[j],
                                    [(0, sp, bp), (tp, ss, bs)], lambda_init)
            x, xp = _out_ln(mixed, diff_w_out[j].astype(BF16), x_rows, ln_mix_g[i], ln_mix_b[i], "diff_out")
        last = i == DEPTH - 1
        x, xb = _moe_layer(x, xp, i, router_w[i], router_b[i], exp_w_gu, exp_w_down, shared_w_gu, shared_w_down,
                           ln_ffn_g[i], ln_ffn_b[i], tp if last else None)
        x_rows, proj_rows = [x], [xb]
    return x.reshape(bp, sp, d), xb.reshape(bs, ss, d)
```

```python
import functools
import math

import jax
import jax.numpy as jnp
from jax import lax
from jax.experimental import pallas as pl
from jax.experimental.pallas import tpu as pltpu

F32 = jnp.float32
BF16 = jnp.bfloat16
I32 = jnp.int32
U32 = jnp.uint32

D_MODEL = 2048
DEPTH = 2
ALPHA = (2 * DEPTH) ** 0.25
LN_EPS = 1e-5
RET_HEADS = 8
RET_DK = D_MODEL // RET_HEADS
RET_DV = 2 * RET_DK
RET_THETA = 10000.0
DIFF_HEADS = 8
DIFF_DH = D_MODEL // (2 * DIFF_HEADS)
DIFF_DV = 2 * DIFF_DH
ROPE_THETA = 500000.0
ROPE_DIMS = DIFF_DH // 4
N_EXPERTS = 64
TOP_K = 8
N_GROUPS = 8
TOPK_GROUPS = 4
GROUP_SIZE = N_EXPERTS // N_GROUPS
EXPERT_HIDDEN = 512
SHARED_HIDDEN = 512
ROUTED_SCALE = 2.5

LANES = 128
SUBLANES = 8
VMEM_LIMIT = 48 * 1024 * 1024

RET_CHUNK = 256
RET_GROUP = 8
EXPERT_ROWS = 512
DISPATCH_TILE = 512
COMBINE_TILE = 128
ATTN_Q_TILE = 512
OUT_LN_BIG_TILE_WEIGHT_BYTES = VMEM_LIMIT // 6
NEG_INF = float("-inf")
LOG2E = math.log2(math.e)


def _cparams(sem):
    return pltpu.CompilerParams(dimension_semantics=sem, vmem_limit_bytes=VMEM_LIMIT)


def _pick_tile(n, prefs):
    for p in prefs:
        if n % p == 0:
            return p
    raise ValueError(f"no tile in {prefs} divides {n}")


def _seg_local_block(i, segs_blocks):
    val = None
    for first, per in segs_blocks:
        loc = lax.rem(i - first, per)
        val = loc if val is None else jnp.where(i >= first, loc, val)
    return val


def _proj_kernel(*refs, src_first, mode, n_q_tiles, n_qk_tiles, q_scale, k_scale):
    n_src = len(src_first)
    x_refs = refs[:n_src]
    w_ref, t0_ref, t1_ref, t2_ref, o_ref = refs[n_src:n_src + 5]
    i = pl.program_id(0)
    j = pl.program_id(1)
    if n_src == 1:
        x = x_refs[0][...]
    else:
        xb_ref = refs[n_src + 5]
        for s in range(n_src):
            hi = src_first[s + 1] if s + 1 < n_src else None
            in_src = (i >= src_first[s]) if hi is None else ((i >= src_first[s]) & (i < hi))

            @pl.when((j == 0) & in_src)
            def _(s=s):
                xb_ref[...] = x_refs[s][...].astype(BF16)
        x = xb_ref[...]
    tn = o_ref.shape[1]

    def cols(lo, width):
        return jnp.dot(x, w_ref[:, lo:lo + width], preferred_element_type=F32)

    @pl.when(j >= n_qk_tiles)
    def _plain():
        o_ref[...] = cols(0, tn).astype(o_ref.dtype)

    @pl.when(j < n_qk_tiles)
    def _rope():
        scale = jnp.where(j < n_q_tiles, q_scale, k_scale)
        if mode == "ret":
            cos = t0_ref[...]
            sin = t1_ref[...]
            half = RET_DK // 2
            for hs in range(tn // RET_DK):
                lo = hs * RET_DK
                acc = cols(lo, RET_DK)
                x1 = acc[:, :half]
                x2 = acc[:, half:]
                o_ref[:, lo:lo + half] = ((x1 * cos - x2 * sin) * scale).astype(o_ref.dtype)
                o_ref[:, lo + half:lo + RET_DK] = ((x2 * cos + x1 * sin) * scale).astype(o_ref.dtype)
        else:
            c = t0_ref[...]
            s_up = t1_ref[...]
            s_dn = t2_ref[...]
            half = ROPE_DIMS // 2
            group = 2 * DIFF_DH
            for hs in range(tn // group):
                acc = cols(hs * group, group)
                for g in range(group // DIFF_DH):
                    lo = hs * group + g * DIFF_DH
                    seg = acc[:, g * DIFF_DH:(g + 1) * DIFF_DH]
                    rot = seg * c + pltpu.roll(seg, half, 1) * s_up + pltpu.roll(seg, DIFF_DH - half, 1) * s_dn
                    o_ref[:, lo:lo + DIFF_DH] = (rot * scale).astype(o_ref.dtype)


def _row_sources(sources, tm, index_of, **spec_kwargs):
    firsts, specs, first = [], [], 0
    for src in sources:
        nblk = src.shape[0] // tm
        firsts.append(first)
        specs.append(pl.BlockSpec((tm, src.shape[1]),
                                  lambda *g, first=first, nblk=nblk: (jnp.clip(index_of(*g) - first, 0, nblk - 1), 0),
                                  **spec_kwargs))
        first += nblk
    return tuple(firsts), specs, first


def _proj(sources, w, tables, segs, *, mode, n_q_cols, n_qk_cols, q_scale, k_scale):
    k = sources[0].shape[1]
    n = w.shape[1]
    seg_len = [s for _, s in segs]
    tm = _pick_tile(math.gcd(*seg_len), (1024, 512, 256))
    tn = 2048 if len(sources) == 1 else 1024
    segs_blocks = [(start // tm, s // tm) for start, s in segs]
    tw = tables[0].shape[1]
    tab_spec = pl.BlockSpec((tm, tw), lambda i, j: (_seg_local_block(i, segs_blocks), 0))
    kwargs = {"pipeline_mode": pl.Buffered(1)} if len(sources) > 1 else {}
    src_first, src_specs, n_row_blocks = _row_sources(sources, tm, lambda i, j: i, **kwargs)
    kern = functools.partial(_proj_kernel, src_first=src_first, mode=mode, n_q_tiles=n_q_cols // tn,
                             n_qk_tiles=n_qk_cols // tn, q_scale=q_scale, k_scale=k_scale)
    return pl.pallas_call(
        kern,
        grid=(n_row_blocks, n // tn),
        in_specs=src_specs + [pl.BlockSpec((k, tn), lambda i, j: (0, j)), tab_spec, tab_spec, tab_spec],
        out_specs=pl.BlockSpec((tm, tn), lambda i, j: (i, j)),
        out_shape=jax.ShapeDtypeStruct((n_row_blocks * tm, n), BF16),
        scratch_shapes=[pltpu.VMEM((tm, k), BF16)] if len(sources) > 1 else [],
        compiler_params=_cparams(("arbitrary", "arbitrary")),
        name=f"proj_{mode}",
    )(*sources, w, *tables)


HIGH_HALF = 0xFFFF0000


def _pack_rows(y):
    n = y.shape[1] // 2
    lo = lax.bitcast_convert_type(y[:, :n].astype(BF16).astype(F32), U32) >> 16
    hi = lax.bitcast_convert_type(y[:, n:].astype(BF16).astype(F32), U32) & jnp.uint32(HIGH_HALF)
    return lo | hi


def _unpack_rows(w):
    lo = lax.bitcast_convert_type(w << 16, F32)
    hi = lax.bitcast_convert_type(w & jnp.uint32(HIGH_HALF), F32)
    return lo, hi


PACK_WORDS = D_MODEL // 2
PACK_SUB = PACK_WORDS // LANES
assert PACK_SUB == SUBLANES


def _tile_of(row):
    start = row * PACK_SUB
    return pl.ds(start if isinstance(start, int) else pl.multiple_of(start, PACK_SUB), PACK_SUB)


def _store_packed(ref, words):
    r = words.shape[0]
    for s in range(PACK_SUB):
        ref[pl.ds(s, r, stride=PACK_SUB), :] = words[:, s * LANES:(s + 1) * LANES]


def _load_packed(ref, r):
    return jnp.concatenate([ref[pl.ds(s, r, stride=PACK_SUB), :] for s in range(PACK_SUB)], axis=1)


def _layer_norm_rows(z, g, b):
    mu = jnp.mean(z, axis=-1, keepdims=True)
    zc = z - mu
    var = jnp.mean(zc * zc, axis=-1, keepdims=True)
    return zc * lax.rsqrt(var + LN_EPS) * g + b


def _out_ln_kernel(a_ref, w_ref, g_ref, b_ref, *refs, src_first):
    n_src = len(src_first)
    x_refs = refs[:n_src]
    o_ref, op_ref = refs[n_src:]
    i = pl.program_id(0)
    h = jnp.dot(a_ref[...], w_ref[...], preferred_element_type=F32)

    def finish(x_ref):
        y = _layer_norm_rows(ALPHA * x_ref[...] + h, g_ref[...], b_ref[...])
        o_ref[...] = y
        _store_packed(op_ref, _pack_rows(y))

    if n_src == 1:
        finish(x_refs[0])
    else:
        for s in range(n_src):
            hi = src_first[s + 1] if s + 1 < n_src else None
            in_src = (i >= src_first[s]) if hi is None else ((i >= src_first[s]) & (i < hi))

            @pl.when(in_src)
            def _(s=s):
                finish(x_refs[s])


def _cast_rows_kernel(*refs, src_first):
    n_src = len(src_first)
    o_ref = refs[n_src]
    i = pl.program_id(0)
    for s in range(n_src):
        hi = src_first[s + 1] if s + 1 < n_src else None
        in_src = (i >= src_first[s]) if hi is None else ((i >= src_first[s]) & (i < hi))

        @pl.when(in_src)
        def _(s=s):
            o_ref[...] = refs[s][...].astype(o_ref.dtype)


def _cast_rows(sources):
    d = sources[0].shape[1]
    tm = 512
    src_first, src_specs, n_row_blocks = _row_sources(sources, tm, lambda i: i)
    return pl.pallas_call(
        functools.partial(_cast_rows_kernel, src_first=src_first),
        grid=(n_row_blocks,),
        in_specs=src_specs,
        out_specs=pl.BlockSpec((tm, d), lambda i: (i, 0)),
        out_shape=jax.ShapeDtypeStruct((n_row_blocks * tm, d), BF16),
        compiler_params=_cparams(("arbitrary",)),
        name="cast_rows",
    )(*sources)


def _out_ln(a, w, x_sources, g, b, name):
    m, kk = a.shape
    d = w.shape[1]
    tm = 512 if kk * d * 2 <= OUT_LN_BIG_TILE_WEIGHT_BYTES else 256
    row = pl.BlockSpec((tm, d), lambda i: (i, 0))
    vec = pl.BlockSpec((1, d), lambda i: (0, 0))
    src_first, src_specs, n_row_blocks = _row_sources(x_sources, tm, lambda i: i)
    assert n_row_blocks * tm == m
    return pl.pallas_call(
        functools.partial(_out_ln_kernel, src_first=src_first),
        grid=(m // tm,),
        in_specs=[pl.BlockSpec((tm, kk), lambda i: (i, 0)),
                  pl.BlockSpec((kk, d), lambda i: (0, 0), pipeline_mode=pl.Buffered(1)), vec, vec] + src_specs,
        out_specs=[row, pl.BlockSpec((tm * PACK_SUB, LANES), lambda i: (i, 0))],
        out_shape=[jax.ShapeDtypeStruct((m, d), F32), jax.ShapeDtypeStruct((m * PACK_SUB, LANES), U32)],
        compiler_params=_cparams(("arbitrary",)),
        name=name,
    )(a, w, g.reshape(1, d), b.reshape(1, d), *x_sources)


def _ret_kernel(lg_ref, cd_ref, q_ref, k_ref, v_ref, *rest, backward, segs_chunks):
    if backward:
        of_ref, g_ref, o_ref, state_ref, dmat_ref, qd_ref, kd_ref = rest
    else:
        o_ref, state_ref, dmat_ref, qd_ref, kd_ref = rest
    h = pl.program_id(0)
    c = pl.program_id(1)
    nc = pl.num_programs(1)
    cc = (nc - 1 - c) if backward else c
    lg = lg_ref[h]
    cs = RET_CHUNK

    @pl.when(c == 0)
    def _tables():
        ii = lax.broadcasted_iota(I32, (cs, cs), 0)
        jj = lax.broadcasted_iota(I32, (cs, cs), 1)
        if backward:
            mask = jj > ii
            dist = (jj - ii).astype(F32)
        else:
            mask = ii >= jj
            dist = (ii - jj).astype(F32)
        dmat_ref[...] = jnp.where(mask, jnp.exp(jnp.where(mask, dist, 0.0) * lg), 0.0)
        pos = lax.broadcasted_iota(I32, (cs, 1), 0).astype(F32)
        if backward:
            qd_ref[...] = jnp.exp((cs - pos) * lg)
            kd_ref[...] = jnp.exp(pos * lg)
        else:
            qd_ref[...] = jnp.exp((pos + 1.0) * lg)
            kd_ref[...] = jnp.exp((cs - 1.0 - pos) * lg)

    loc = _seg_local_block(cc, segs_chunks)
    if backward:
        per = None
        for first, p in segs_chunks:
            per = p if per is None else jnp.where(cc >= first, p, per)
        is_start = loc == per - 1
    else:
        is_start = loc == 0

    @pl.when(is_start)
    def _reset():
        state_ref[...] = jnp.zeros_like(state_ref)

    state = state_ref[...]
    n_sub = q_ref.shape[0] // cs
    for sub in (range(n_sub - 1, -1, -1) if backward else range(n_sub)):
        rows = pl.ds(sub * cs, cs)
        q = q_ref[rows, :]
        k = k_ref[rows, :]
        v = v_ref[rows, :]
        att = lax.dot_general(q, k, (((1,), (1,)), ((), ())), preferred_element_type=F32) * dmat_ref[...]
        inner = jnp.dot(att.astype(BF16), v, preferred_element_type=F32)
        cross = jnp.dot(q, state.astype(BF16), preferred_element_type=F32) * qd_ref[...]
        o = inner + cross
        kt = (k.astype(F32) * kd_ref[...]).T.astype(BF16)
        state = state * cd_ref[h] + jnp.dot(kt, v, preferred_element_type=F32)

        if backward:
            tot = of_ref[rows, :] + o
            mu = jnp.mean(tot, axis=-1, keepdims=True)
            tc = tot - mu
            var = jnp.mean(tc * tc, axis=-1, keepdims=True)
            on = tc * lax.rsqrt(var + LN_EPS)
            g = g_ref[rows, :].astype(F32)
            o_ref[rows, :] = ((g * (1.0 / (1.0 + jnp.exp(-g)))) * on).astype(o_ref.dtype)
        else:
            o_ref[rows, :] = o
    state_ref[...] = state


def _retention(proj, lg_f, lg_b, segs):
    t = proj.shape[0]
    cs = RET_CHUNK
    rb = RET_CHUNK * RET_GROUP
    nc = t // rb
    segs_chunks = [(start // rb, s // rb) for start, s in segs]
    qk_blocks = (RET_HEADS * RET_DK) // RET_DK
    v_blocks0 = (2 * RET_HEADS * RET_DK) // RET_DV
    g_blocks0 = v_blocks0 + RET_HEADS
    smem = pl.BlockSpec(memory_space=pltpu.SMEM)
    scratch = [pltpu.VMEM((RET_DK, RET_DV), F32), pltpu.VMEM((cs, cs), F32),
               pltpu.VMEM((cs, 1), F32), pltpu.VMEM((cs, 1), F32)]

    def run(backward, lg, extra):
        def cmap(c):
            return (nc - 1 - c) if backward else c
        in_specs = [smem, smem,
                    pl.BlockSpec((rb, RET_DK), lambda h, c: (cmap(c), h)),
                    pl.BlockSpec((rb, RET_DK), lambda h, c: (cmap(c), qk_blocks + h)),
                    pl.BlockSpec((rb, RET_DV), lambda h, c: (cmap(c), v_blocks0 + h))]
        args = [lg, jnp.exp(cs * lg), proj, proj, proj]
        if backward:
            in_specs += [pl.BlockSpec((rb, RET_DV), lambda h, c: (cmap(c), h)),
                         pl.BlockSpec((rb, RET_DV), lambda h, c: (cmap(c), g_blocks0 + h))]
            args += [extra, proj]
        return pl.pallas_call(
            functools.partial(_ret_kernel, backward=backward, segs_chunks=segs_chunks),
            grid=(RET_HEADS, nc),
            in_specs=in_specs,
            out_specs=pl.BlockSpec((rb, RET_DV), lambda h, c: (cmap(c), h)),
            out_shape=jax.ShapeDtypeStruct((t, RET_HEADS * RET_DV), BF16 if backward else F32),
            scratch_shapes=scratch,
            compiler_params=_cparams(("arbitrary", "arbitrary")),
            name="retention_bwd" if backward else "retention_fwd",
        )(*args)

    o_f = run(False, lg_f, None)
    return run(True, lg_b, o_f)


def _attn_kernel(lam_ref, q_ref, k_ref, v_ref, sg_ref, *rest, out_scale):
    o_ref = rest[-1]
    probs = []
    for c in range(2):
        q = q_ref[:, c * DIFF_DH:(c + 1) * DIFF_DH]
        k = k_ref[:, c * DIFF_DH:(c + 1) * DIFF_DH]
        s = lax.dot_general(q, k, (((1,), (1,)), ((), ())), preferred_element_type=F32)
        p = jnp.exp2(s - jnp.max(s, axis=-1, keepdims=True))
        probs.append((p, jnp.sum(p, axis=-1, keepdims=True)))
    (p0, den0), (p1, den1) = probs
    a = p0 * (1.0 / den0) - p1 * (lam_ref[0] / den1)
    o = jnp.dot(a.astype(BF16), v_ref[...], preferred_element_type=F32)
    ms = jnp.mean(o * o, axis=-1, keepdims=True)
    o_ref[...] = (o * lax.rsqrt(ms + LN_EPS) * sg_ref[...] * out_scale).astype(o_ref.dtype)


def _diff_attention(proj, lam, subln_g, segs, lambda_init):
    nq_blocks = DIFF_HEADS
    t = proj.shape[0]
    out = None
    for start, s, n_seq in segs:
        tq = ATTN_Q_TILE
        assert s % tq == 0 and start % s == 0
        qb0 = start // tq
        kb0 = start // s
        nqt = s // tq
        in_specs = [pl.BlockSpec(memory_space=pltpu.SMEM),
                    pl.BlockSpec((tq, 2 * DIFF_DH), lambda b, h, i: (qb0 + b * nqt + i, h)),
                    pl.BlockSpec((s, 2 * DIFF_DH), lambda b, h, i: (kb0 + b, nq_blocks + h)),
                    pl.BlockSpec((s, DIFF_DV), lambda b, h, i: (kb0 + b, 2 * nq_blocks + h)),
                    pl.BlockSpec((1, DIFF_DV), lambda b, h, i: (0, 0))]
        args = [lam, proj, proj, proj, subln_g.reshape(1, DIFF_DV)]
        aliases = {}
        if out is not None:
            in_specs.append(pl.BlockSpec(memory_space=pl.ANY))
            args.append(out)
            aliases = {len(args) - 1: 0}
        out = pl.pallas_call(
            functools.partial(_attn_kernel, out_scale=1.0 - lambda_init),
            grid=(n_seq, DIFF_HEADS, nqt),
            in_specs=in_specs,
            out_specs=pl.BlockSpec((tq, DIFF_DV), lambda b, h, i: (qb0 + b * nqt + i, h)),
            out_shape=jax.ShapeDtypeStruct((t, DIFF_HEADS * DIFF_DV), BF16),
            input_output_aliases=aliases,
            compiler_params=_cparams(("arbitrary", "arbitrary", "arbitrary")),
            name=f"diff_attn_s{s}",
        )(*args)
    return out


def _first_max(cur, rows, n_rows):
    m = jnp.max(cur, axis=0, keepdims=True)
    idx = jnp.min(jnp.where(cur == m, rows, float(n_rows)), axis=0, keepdims=True)
    return m, idx


def _router_kernel(x_ref, wt_ref, b_ref, idx_ref, gate_ref, rank_ref, cnt_ref, carry_ref):
    i = pl.program_id(0)

    @pl.when(i == 0)
    def _init():
        carry_ref[...] = jnp.zeros_like(carry_ref)

    tn = x_ref.shape[0]
    x = x_ref[...]
    xh = x.astype(BF16)
    xl = (x - xh.astype(F32)).astype(BF16)
    w = wt_ref[...]
    wh = w.astype(BF16)
    wl = (w - wh.astype(F32)).astype(BF16)

    def nt(a, b):
        return lax.dot_general(a, b, (((1,), (1,)), ((), ())), preferred_element_type=F32)

    logits = nt(wh, xh) + nt(wh, xl) + nt(wl, xh)
    scores = 1.0 / (1.0 + jnp.exp(-logits))
    biased = scores + b_ref[...]

    rows_g = lax.broadcasted_iota(I32, (GROUP_SIZE, tn), 0).astype(F32)
    rows_e = lax.broadcasted_iota(I32, (N_EXPERTS, tn), 0).astype(F32)

    gscore = []
    for g in range(N_GROUPS):
        slab = biased[g * GROUP_SIZE:(g + 1) * GROUP_SIZE]
        m1, i1 = _first_max(slab, rows_g, GROUP_SIZE)
        m2 = jnp.max(jnp.where(rows_g == i1, NEG_INF, slab), axis=0, keepdims=True)
        gscore.append(m1 + m2)
    cur = jnp.concatenate(gscore, axis=0)
    rows_grp = lax.broadcasted_iota(I32, (N_GROUPS, tn), 0).astype(F32)
    gsel = jnp.zeros((N_GROUPS, tn), F32)
    for _ in range(TOPK_GROUPS):
        _, gi = _first_max(cur, rows_grp, N_GROUPS)
        pick = rows_grp == gi
        gsel = jnp.where(pick, 1.0, gsel)
        cur = jnp.where(pick, NEG_INF, cur)

    masked = jnp.concatenate(
        [jnp.where(gsel[g:g + 1] > 0.0, biased[g * GROUP_SIZE:(g + 1) * GROUP_SIZE], NEG_INF)
         for g in range(N_GROUPS)], axis=0)

    sel = jnp.zeros((N_EXPERTS, tn), F32)
    picks = []
    cur = masked
    for _ in range(TOP_K):
        _, ei = _first_max(cur, rows_e, N_EXPERTS)
        pick = rows_e == ei
        sel = jnp.where(pick, 1.0, sel)
        cur = jnp.where(pick, NEG_INF, cur)
        picks.append(ei)

    gsum = jnp.sum(scores * sel, axis=0, keepdims=True)

    ta = lax.broadcasted_iota(I32, (tn, tn), 0)
    tb = lax.broadcasted_iota(I32, (tn, tn), 1)
    before = jnp.where(ta < tb, 1.0, 0.0).astype(BF16)
    rank = jnp.dot(sel.astype(BF16), before, preferred_element_type=F32) + carry_ref[...]
    carry_ref[...] = carry_ref[...] + jnp.sum(sel, axis=1, keepdims=True)
    cnt_ref[...] = carry_ref[...]

    gates = []
    ranks = []
    for ei in picks:
        hit = rows_e == ei
        gates.append(jnp.sum(jnp.where(hit, scores, 0.0), axis=0, keepdims=True) / gsum * ROUTED_SCALE)
        ranks.append(jnp.sum(jnp.where(hit, rank, 0.0), axis=0, keepdims=True))
    idx_ref[...] = jnp.concatenate(picks, axis=0).astype(I32)
    gate_ref[...] = jnp.concatenate(gates, axis=0)
    rank_ref[...] = jnp.concatenate(ranks, axis=0).astype(I32)


def _router(x, w_router, b_router):
    t, d = x.shape
    tn = _pick_tile(t, (512, 256))
    kt = pl.BlockSpec((TOP_K, tn), lambda i: (0, i))
    return pl.pallas_call(
        _router_kernel,
        grid=(t // tn,),
        in_specs=[pl.BlockSpec((tn, d), lambda i: (i, 0)),
                  pl.BlockSpec((N_EXPERTS, d), lambda i: (0, 0)),
                  pl.BlockSpec((N_EXPERTS, 1), lambda i: (0, 0))],
        out_specs=[kt, kt, kt, pl.BlockSpec((N_EXPERTS, 1), lambda i: (0, 0))],
        out_shape=[jax.ShapeDtypeStruct((TOP_K, t), I32), jax.ShapeDtypeStruct((TOP_K, t), F32),
                   jax.ShapeDtypeStruct((TOP_K, t), I32), jax.ShapeDtypeStruct((N_EXPERTS, 1), F32)],
        scratch_shapes=[pltpu.VMEM((N_EXPERTS, 1), F32)],
        compiler_params=_cparams(("arbitrary",)),
        name="router",
    )(x, w_router.T, b_router.reshape(N_EXPERTS, 1))


def _slot_kernel(idx_ref, rank_ref, start_ref, slot_ref):
    tn = idx_ref.shape[1]
    rows_e = lax.broadcasted_iota(I32, (N_EXPERTS, tn), 0)
    out = []
    for k in range(TOP_K):
        hit = rows_e == idx_ref[k:k + 1, :]
        base = jnp.sum(jnp.where(hit, start_ref[...], 0.0), axis=0, keepdims=True)
        out.append(base.astype(I32) + rank_ref[k:k + 1, :])
    slot_ref[...] = jnp.concatenate(out, axis=0)


def _slots(idx_t, rank_t, pad_start):
    t = idx_t.shape[1]
    tn = _pick_tile(t, (512, 256))
    kt = pl.BlockSpec((TOP_K, tn), lambda i: (0, i))
    return pl.pallas_call(
        _slot_kernel,
        grid=(t // tn,),
        in_specs=[kt, kt, pl.BlockSpec((N_EXPERTS, 1), lambda i: (0, 0))],
        out_specs=kt,
        out_shape=jax.ShapeDtypeStruct((TOP_K, t), I32),
        compiler_params=_cparams(("arbitrary",)),
        name="slots",
    )(idx_t, rank_t, pad_start.astype(F32).reshape(N_EXPERTS, 1))


def _row_copy(src, src_row, dst, dst_row, sem):
    return pltpu.make_async_copy(src.at[_tile_of(src_row)], dst.at[_tile_of(dst_row)], sem)


def _dispatch_kernel(pad_end_ref, padded_ref, slot_hbm, x_ref, xs_hbm, slot_smem, zero_ref, sem, slot_sem):
    i = pl.program_id(0)
    tt = x_ref.shape[0] // PACK_SUB
    bm = zero_ref.shape[0] // PACK_SUB
    slot_cp = pltpu.make_async_copy(slot_hbm.at[i], slot_smem, slot_sem)
    slot_cp.start()

    @pl.when(i == 0)
    def _clear():
        zero_ref[...] = jnp.zeros_like(zero_ref)

        def pad_copy(e):
            off = pl.multiple_of((pad_end_ref[e] - bm) * PACK_SUB, bm * PACK_SUB)
            return pltpu.make_async_copy(zero_ref, xs_hbm.at[pl.ds(off, bm * PACK_SUB)], sem)

        def start(e, carry):
            @pl.when(padded_ref[e] > 0)
            def _():
                pad_copy(e).start()
            return carry

        def wait(e, carry):
            @pl.when(padded_ref[e] > 0)
            def _():
                pad_copy(e).wait()
            return carry

        lax.fori_loop(0, N_EXPERTS, start, 0)
        lax.fori_loop(0, N_EXPERTS, wait, 0)

    slot_cp.wait()
    per_row = LANES // TOP_K

    def issue(r, carry):
        for j in range(LANES):
            _row_copy(x_ref, r * per_row + j // TOP_K, xs_hbm, slot_smem[r, j], sem).start(priority=j % 2)
        return carry

    lax.fori_loop(0, tt // per_row, issue, 0)

    def drain(t, carry):
        for k in range(TOP_K):
            _row_copy(x_ref, t, xs_hbm, 0, sem).wait()
        return carry

    lax.fori_loop(0, tt, drain, 0)


def _dispatch(xp, slot_tiles, pad_end, padded, n_slots):
    t = xp.shape[0] // PACK_SUB
    tt = DISPATCH_TILE
    grid_spec = pltpu.PrefetchScalarGridSpec(
        num_scalar_prefetch=2,
        grid=(t // tt,),
        in_specs=[pl.BlockSpec(memory_space=pl.ANY),
                  pl.BlockSpec((tt * PACK_SUB, LANES), lambda i, pe, pd: (i, 0))],
        out_specs=pl.BlockSpec(memory_space=pl.ANY),
        scratch_shapes=[pltpu.SMEM((tt * TOP_K // LANES, LANES), I32),
                        pltpu.VMEM((EXPERT_ROWS * PACK_SUB, LANES), U32),
                        pltpu.SemaphoreType.DMA, pltpu.SemaphoreType.DMA],
    )
    return pl.pallas_call(
        _dispatch_kernel,
        grid_spec=grid_spec,
        out_shape=jax.ShapeDtypeStruct((n_slots * PACK_SUB, LANES), U32),
        compiler_params=_cparams(("arbitrary",)),
        name="dispatch",
    )(pad_end, padded, slot_tiles, xp)


def _ffn_kernel(block_e_ref, n_used_ref, x_ref, wgu_ref, wd_ref, o_ref, wgu_bf, wd_bf, *, pack_out):
    b = pl.program_id(0)

    @pl.when(b < n_used_ref[0])
    def _():
        e = block_e_ref[b]
        e_prev = block_e_ref[jnp.maximum(b - 1, 0)]

        @pl.when((b == 0) | (e != e_prev))
        def _():
            wgu_bf[...] = wgu_ref[0, 0].astype(BF16)
            wd_bf[...] = wd_ref[0, 0].astype(BF16)

        hidden = wd_bf.shape[0]
        rows = x_ref.shape[0] // PACK_SUB
        lo, hi = _unpack_rows(_load_packed(x_ref, rows))
        gu = (jnp.dot(lo.astype(BF16), wgu_bf[:PACK_WORDS], preferred_element_type=F32)
              + jnp.dot(hi.astype(BF16), wgu_bf[PACK_WORDS:], preferred_element_type=F32))
        gate = gu[:, :hidden]
        act = gate * (1.0 / (1.0 + jnp.exp(-gate))) * gu[:, hidden:]
        y = jnp.dot(act.astype(BF16), wd_bf[...], preferred_element_type=F32)
        if pack_out:
            _store_packed(o_ref, _pack_rows(y))
        else:
            o_ref[...] = y


def _ffn(xp, w_gu, w_down, layer, block_e, n_used, rows, pack_out, name):
    n = xp.shape[0] // PACK_SUB
    d = 2 * PACK_WORDS
    hidden = w_down.shape[2]

    def row_map(b, be, nu):
        return (jnp.minimum(b, nu[0] - 1), 0)

    packed_rows = pl.BlockSpec((rows * PACK_SUB, LANES), row_map)
    if pack_out:
        out_spec, out_shape = packed_rows, jax.ShapeDtypeStruct((n * PACK_SUB, LANES), U32)
    else:
        out_spec, out_shape = pl.BlockSpec((rows, d), row_map), jax.ShapeDtypeStruct((n, d), F32)
    grid_spec = pltpu.PrefetchScalarGridSpec(
        num_scalar_prefetch=2,
        grid=(n // rows,),
        in_specs=[packed_rows,
                  pl.BlockSpec((1, 1, d, 2 * hidden), lambda b, be, nu: (layer, be[b], 0, 0)),
                  pl.BlockSpec((1, 1, hidden, d), lambda b, be, nu: (layer, be[b], 0, 0))],
        out_specs=out_spec,
        scratch_shapes=[pltpu.VMEM((d, 2 * hidden), BF16), pltpu.VMEM((hidden, d), BF16)],
    )
    return pl.pallas_call(
        functools.partial(_ffn_kernel, pack_out=pack_out),
        grid_spec=grid_spec,
        out_shape=out_shape,
        compiler_params=_cparams(("arbitrary",)),
        name=name,
    )(block_e, n_used, xp, w_gu, w_down)


def _combine_kernel(slot_hbm, x_ref, sh_ref, gate_ref, g_ref, b_ref, ys_hbm, o_ref, *rest, tile0, with_bf16):
    if with_bf16:
        ob_ref, *rest = rest
    slots0, slots1, buf0, buf1, sems, slot_sem = rest
    slot_smem = (slots0, slots1)
    bufs = (buf0, buf1)
    i = pl.program_id(0)
    n = pl.num_programs(0)
    tt = x_ref.shape[0]
    per_row = LANES // TOP_K
    groups = tt // per_row

    def load_slots(tile, par):
        slot_cp = pltpu.make_async_copy(slot_hbm.at[tile0 + tile], slot_smem[par], slot_sem)
        slot_cp.start()
        slot_cp.wait()

    def issue_group(r, par):
        for j in range(LANES):
            _row_copy(ys_hbm, slot_smem[par][r, j], bufs[par].at[j % TOP_K], r * per_row + j // TOP_K,
                      sems.at[par]).start(priority=j % 2)

    def drain(par):
        def body(t, carry):
            for k in range(TOP_K):
                _row_copy(ys_hbm, 0, bufs[par].at[k], t, sems.at[par]).wait()
            return carry
        lax.fori_loop(0, tt, body, 0)

    def reduce_group(r, par):
        rows = slice(r * per_row, (r + 1) * per_row)
        sh = sh_ref[rows, :]
        h_lo = sh[:, :PACK_WORDS]
        h_hi = sh[:, PACK_WORDS:]
        gate = gate_ref[rows, :]
        for k in range(TOP_K):
            words = jnp.concatenate(
                [bufs[par][k, pl.ds(r * per_row * PACK_SUB + s, per_row, stride=PACK_SUB), :]
                 for s in range(PACK_SUB)], axis=1)
            lo, hi = _unpack_rows(words)
            h_lo = h_lo + lo * gate[:, k:k + 1]
            h_hi = h_hi + hi * gate[:, k:k + 1]
        h = jnp.concatenate([h_lo, h_hi], axis=1)
        y = _layer_norm_rows(ALPHA * x_ref[rows, :] + h, g_ref[...], b_ref[...])
        o_ref[rows, :] = y
        if with_bf16:
            ob_ref[rows, :] = y.astype(BF16)

    @pl.when(i == 0)
    def _():
        load_slots(0, 0)
        for r in range(groups):
            issue_group(r, 0)

    nxt = jnp.minimum(i + 1, n - 1)
    for par in range(2):
        @pl.when(i % 2 == par)
        def _(par=par):
            drain(par)
            load_slots(nxt, 1 - par)
            for r in range(groups):
                issue_group(r, 1 - par)
                reduce_group(r, par)

            @pl.when(i == n - 1)
            def _():
                drain(1 - par)


def _combine(x, shared, ys, slot_tiles, gate, g, b, row_range, with_bf16):
    d = x.shape[1]
    tt = COMBINE_TILE
    r0, r1 = row_range
    tile0 = r0 // tt
    row_in = pl.BlockSpec((tt, d), lambda i: (tile0 + i, 0))
    row_out = pl.BlockSpec((tt, d), lambda i: (i, 0))
    vec = pl.BlockSpec((1, d), lambda i: (0, 0))
    out_specs = [row_out]
    out_shape = [jax.ShapeDtypeStruct((r1 - r0, d), F32)]
    if with_bf16:
        out_specs.append(row_out)
        out_shape.append(jax.ShapeDtypeStruct((r1 - r0, d), BF16))
    return pl.pallas_call(
        functools.partial(_combine_kernel, tile0=tile0, with_bf16=with_bf16),
        grid=((r1 - r0) // tt,),
        in_specs=[pl.BlockSpec(memory_space=pl.ANY), row_in, row_in,
                  pl.BlockSpec((tt, TOP_K), lambda i: (tile0 + i, 0)), vec, vec,
                  pl.BlockSpec(memory_space=pl.ANY)],
        out_specs=out_specs,
        out_shape=out_shape,
        scratch_shapes=[pltpu.SMEM((tt * TOP_K // LANES, LANES), I32), pltpu.SMEM((tt * TOP_K // LANES, LANES), I32),
                        pltpu.VMEM((TOP_K, tt * PACK_SUB, LANES), U32), pltpu.VMEM((TOP_K, tt * PACK_SUB, LANES), U32),
                        pltpu.SemaphoreType.DMA((2,)), pltpu.SemaphoreType.DMA],
        compiler_params=_cparams(("arbitrary",)),
        name="combine",
    )(slot_tiles, x, shared, gate, g.reshape(1, d), b.reshape(1, d), ys)


def _moe_layer(x, xp, layer, w_router, b_router, w_gu, w_down, sw_gu, sw_down, ln_g, ln_b, split_rows):
    t, d = x.shape
    bm = EXPERT_ROWS
    idx_t, gate_t, rank_t, counts = _router(x, w_router, b_router)

    counts = counts.reshape(N_EXPERTS).astype(I32)
    padded = (counts + bm - 1) // bm * bm
    pad_end = jnp.cumsum(padded)
    pad_start = pad_end - padded
    n_blocks = (t * TOP_K + N_EXPERTS * (bm - 1) + bm - 1) // bm
    n_slots = n_blocks * bm
    block_first = jnp.arange(n_blocks, dtype=I32) * bm
    block_e = jnp.minimum(jnp.sum((pad_end[None, :] <= block_first[:, None]).astype(I32), axis=1), N_EXPERTS - 1)
    n_used = (pad_end[-1:] // bm).astype(I32)

    slot_t = _slots(idx_t, rank_t, pad_start)
    slot_flat = slot_t.T.reshape(t * TOP_K // LANES, LANES)

    def slot_tiles(tile):
        return slot_flat.reshape(t // tile, tile * TOP_K // LANES, LANES)

    xs = _dispatch(xp, slot_tiles(DISPATCH_TILE), pad_end.astype(I32), padded.astype(I32), n_slots)
    ys = _ffn(xs, w_gu, w_down, layer, block_e, n_used, bm, True, "expert_ffn")
    shared_rows = _pick_tile(t, (512, 256))
    shared = _ffn(xp, sw_gu[:, None], sw_down[:, None], layer, jnp.zeros((t // shared_rows,), I32),
                  jnp.full((1,), t // shared_rows, I32), shared_rows, False, "shared_ffn")
    args = (x, shared, ys, slot_tiles(COMBINE_TILE), gate_t.T, ln_g, ln_b)
    if split_rows is None:
        return _combine(*args, (0, t), True)
    return _combine(*args, (0, split_rows), False)[0], _combine(*args, (split_rows, t), False)[0]


def _rope_angles(s, theta, half):
    inv = theta ** (-jnp.arange(half, dtype=F32) / half)
    ang = jnp.arange(s, dtype=F32)[:, None] * inv[None, :]
    return jnp.cos(ang), jnp.sin(ang)


def _ret_tables(s):
    cos, sin = _rope_angles(s, RET_THETA, RET_DK // 2)
    return cos, sin, sin


def _diff_tables(s):
    half = ROPE_DIMS // 2
    cos, sin = _rope_angles(s, ROPE_THETA, half)
    rest = DIFF_DH - ROPE_DIMS
    zeros_h = jnp.zeros((s, half), F32)
    c = jnp.concatenate([cos, cos, jnp.ones((s, rest), F32)], axis=1)
    s_up = jnp.concatenate([zeros_h, sin, jnp.zeros((s, rest), F32)], axis=1)
    s_dn = jnp.concatenate([-sin, zeros_h, jnp.zeros((s, rest), F32)], axis=1)
    return c, s_up, s_dn


def kernel(x_prompt, x_sample, ret_w_in, ret_decay_f, ret_decay_b, ret_w_out, diff_w_in, diff_lam_q1, diff_lam_k1,
           diff_lam_q2, diff_lam_k2, diff_subln_g, diff_w_out, ln_mix_g, ln_mix_b, router_w, router_b, exp_w_gu,
           exp_w_down, shared_w_gu, shared_w_down, ln_ffn_g, ln_ffn_b):
    bp, sp, d = x_prompt.shape
    bs, ss, _ = x_sample.shape
    tp = bp * sp
    x_rows = [x_prompt.reshape(tp, d), x_sample.reshape(bs * ss, d)]
    proj_rows = [_cast_rows(x_rows)]
    segs = [(0, sp), (tp, ss)]
    s_max = max(sp, ss)

    for i in range(DEPTH):
        j = i // 2
        if i % 2 == 0:
            qk = RET_HEADS * RET_DK
            proj = _proj(proj_rows, ret_w_in[j].astype(BF16), _ret_tables(s_max), segs, mode="ret",
                         n_q_cols=qk, n_qk_cols=2 * qk, q_scale=1.0, k_scale=RET_DK ** -0.5)
            lg_f = -jax.nn.softplus(-ret_decay_f[j].astype(F32))
            lg_b = -jax.nn.softplus(-ret_decay_b[j].astype(F32))
            mixed = _retention(proj, lg_f, lg_b, segs)
            x, xp = _out_ln(mixed, ret_w_out[j].astype(BF16), x_rows, ln_mix_g[i], ln_mix_b[i], "ret_out")
        else:
            lambda_init = 0.8 - 0.6 * math.exp(-0.3 * i)
            nq = DIFF_HEADS * 2 * DIFF_DH
            proj = _proj(proj_rows, diff_w_in[j].astype(BF16), _diff_tables(s_max), segs, mode="diff",
                         n_q_cols=nq, n_qk_cols=2 * nq, q_scale=DIFF_DH ** -0.5 * LOG2E, k_scale=1.0)
            lam = (jnp.exp(jnp.sum(diff_lam_q1[j].astype(F32) * diff_lam_k1[j].astype(F32)))
                   - jnp.exp(jnp.sum(diff_lam_q2[j].astype(F32) * diff_lam_k2[j].astype(F32))) + lambda_init)
            mixed = _diff_attention(proj, lam.reshape(1), diff_subln_g[j],
                                    [(0, sp, bp), (tp, ss, bs)], lambda_init)
            x, xp = _out_ln(mixed, diff_w_out[j].astype(BF16), x_rows, ln_mix_g[i], ln_mix_b[i], "diff_out")
        last = i == DEPTH - 1
        x, xb = _moe_layer(x, xp, i, router_w[i], router_b[i], exp_w_gu, exp_w_down, shared_w_gu, shared_w_down,
                           ln_ffn_g[i], ln_ffn_b[i], tp if last else None)
        x_rows, proj_rows = [x], [xb]
    return x.reshape(bp, sp, d), xb.reshape(bs, ss, d)
```

```python
import functools
import math

import jax
import jax.numpy as jnp
from jax import lax
from jax.experimental import pallas as pl
from jax.experimental.pallas import tpu as pltpu

F32 = jnp.float32
BF16 = jnp.bfloat16
I32 = jnp.int32
U32 = jnp.uint32

D_MODEL = 2048
DEPTH = 2
ALPHA = (2 * DEPTH) ** 0.25
LN_EPS = 1e-5
RET_HEADS = 8
RET_DK = D_MODEL // RET_HEADS
RET_DV = 2 * RET_DK
RET_THETA = 10000.0
DIFF_HEADS = 8
DIFF_DH = D_MODEL // (2 * DIFF_HEADS)
DIFF_DV = 2 * DIFF_DH
ROPE_THETA = 500000.0
ROPE_DIMS = DIFF_DH // 4
N_EXPERTS = 64
TOP_K = 8
N_GROUPS = 8
TOPK_GROUPS = 4
GROUP_SIZE = N_EXPERTS // N_GROUPS
EXPERT_HIDDEN = 512
SHARED_HIDDEN = 512
ROUTED_SCALE = 2.5

LANES = 128
SUBLANES = 8
VMEM_LIMIT = 48 * 1024 * 1024

RET_CHUNK = 256
RET_GROUP = 8
EXPERT_ROWS = 512
DISPATCH_TILE = 512
COMBINE_TILE = 128
ATTN_Q_TILE = 512
OUT_LN_BIG_TILE_WEIGHT_BYTES = VMEM_LIMIT // 6
NEG_INF = float("-inf")
LOG2E = math.log2(math.e)


def _cparams(sem):
    return pltpu.CompilerParams(dimension_semantics=sem, vmem_limit_bytes=VMEM_LIMIT)


def _pick_tile(n, prefs):
    for p in prefs:
        if n % p == 0:
            return p
    raise ValueError(f"no tile in {prefs} divides {n}")


def _seg_local_block(i, segs_blocks):
    val = None
    for first, per in segs_blocks:
        loc = lax.rem(i - first, per)
        val = loc if val is None else jnp.where(i >= first, loc, val)
    return val


def _proj_kernel(*refs, src_first, mode, n_q_tiles, n_qk_tiles, q_scale, k_scale):
    n_src = len(src_first)
    x_refs = refs[:n_src]
    w_ref, t0_ref, t1_ref, t2_ref, o_ref = refs[n_src:n_src + 5]
    i = pl.program_id(0)
    j = pl.program_id(1)
    if n_src == 1:
        x = x_refs[0][...]
    else:
        xb_ref = refs[n_src + 5]
        for s in range(n_src):
            hi = src_first[s + 1] if s + 1 < n_src else None
            in_src = (i >= src_first[s]) if hi is None else ((i >= src_first[s]) & (i < hi))

            @pl.when((j == 0) & in_src)
            def _(s=s):
                xb_ref[...] = x_refs[s][...].astype(BF16)
        x = xb_ref[...]
    tn = o_ref.shape[1]

    def cols(lo, width):
        return jnp.dot(x, w_ref[:, lo:lo + width], preferred_element_type=F32)

    @pl.when(j >= n_qk_tiles)
    def _plain():
        o_ref[...] = cols(0, tn).astype(o_ref.dtype)

    @pl.when(j < n_qk_tiles)
    def _rope():
        scale = jnp.where(j < n_q_tiles, q_scale, k_scale)
        if mode == "ret":
            cos = t0_ref[...]
            sin = t1_ref[...]
            half = RET_DK // 2
            for hs in range(tn // RET_DK):
                lo = hs * RET_DK
                acc = cols(lo, RET_DK)
                x1 = acc[:, :half]
                x2 = acc[:, half:]
                o_ref[:, lo:lo + half] = ((x1 * cos - x2 * sin) * scale).astype(o_ref.dtype)
                o_ref[:, lo + half:lo + RET_DK] = ((x2 * cos + x1 * sin) * scale).astype(o_ref.dtype)
        else:
            c = t0_ref[...]
            s_up = t1_ref[...]
            s_dn = t2_ref[...]
            half = ROPE_DIMS // 2
            group = 2 * DIFF_DH
            for hs in range(tn // group):
                acc = cols(hs * group, group)
                for g in range(group // DIFF_DH):
                    lo = hs * group + g * DIFF_DH
                    seg = acc[:, g * DIFF_DH:(g + 1) * DIFF_DH]
                    rot = seg * c + pltpu.roll(seg, half, 1) * s_up + pltpu.roll(seg, DIFF_DH - half, 1) * s_dn
                    o_ref[:, lo:lo + DIFF_DH] = (rot * scale).astype(o_ref.dtype)


def _row_sources(sources, tm, index_of, **spec_kwargs):
    firsts, specs, first = [], [], 0
    for src in sources:
        nblk = src.shape[0] // tm
        firsts.append(first)
        specs.append(pl.BlockSpec((tm, src.shape[1]),
                                  lambda *g, first=first, nblk=nblk: (jnp.clip(index_of(*g) - first, 0, nblk - 1), 0),
                                  **spec_kwargs))
        first += nblk
    return tuple(firsts), specs, first


def _proj(sources, w, tables, segs, *, mode, n_q_cols, n_qk_cols, q_scale, k_scale):
    k = sources[0].shape[1]
    n = w.shape[1]
    seg_len = [s for _, s in segs]
    tm = _pick_tile(math.gcd(*seg_len), (1024, 512, 256))
    tn = 2048 if len(sources) == 1 else 1024
    segs_blocks = [(start // tm, s // tm) for start, s in segs]
    tw = tables[0].shape[1]
    tab_spec = pl.BlockSpec((tm, tw), lambda i, j: (_seg_local_block(i, segs_blocks), 0))
    kwargs = {"pipeline_mode": pl.Buffered(1)} if len(sources) > 1 else {}
    src_first, src_specs, n_row_blocks = _row_sources(sources, tm, lambda i, j: i, **kwargs)
    kern = functools.partial(_proj_kernel, src_first=src_first, mode=mode, n_q_tiles=n_q_cols // tn,
                             n_qk_tiles=n_qk_cols // tn, q_scale=q_scale, k_scale=k_scale)
    return pl.pallas_call(
        kern,
        grid=(n_row_blocks, n // tn),
        in_specs=src_specs + [pl.BlockSpec((k, tn), lambda i, j: (0, j)), tab_spec, tab_spec, tab_spec],
        out_specs=pl.BlockSpec((tm, tn), lambda i, j: (i, j)),
        out_shape=jax.ShapeDtypeStruct((n_row_blocks * tm, n), BF16),
        scratch_shapes=[pltpu.VMEM((tm, k), BF16)] if len(sources) > 1 else [],
        compiler_params=_cparams(("arbitrary", "arbitrary")),
        name=f"proj_{mode}",
    )(*sources, w, *tables)


HIGH_HALF = 0xFFFF0000


def _pack_rows(y):
    n = y.shape[1] // 2
    lo = lax.bitcast_convert_type(y[:, :n].astype(BF16).astype(F32), U32) >> 16
    hi = lax.bitcast_convert_type(y[:, n:].astype(BF16).astype(F32), U32) & jnp.uint32(HIGH_HALF)
    return lo | hi


def _unpack_rows(w):
    lo = lax.bitcast_convert_type(w << 16, F32)
    hi = lax.bitcast_convert_type(w & jnp.uint32(HIGH_HALF), F32)
    return lo, hi


PACK_WORDS = D_MODEL // 2
PACK_SUB = PACK_WORDS // LANES
assert PACK_SUB == SUBLANES


def _tile_of(row):
    start = row * PACK_SUB
    return pl.ds(start if isinstance(start, int) else pl.multiple_of(start, PACK_SUB), PACK_SUB)


def _store_packed(ref, words):
    r = words.shape[0]
    for s in range(PACK_SUB):
        ref[pl.ds(s, r, stride=PACK_SUB), :] = words[:, s * LANES:(s + 1) * LANES]


def _load_packed(ref, r):
    return jnp.concatenate([ref[pl.ds(s, r, stride=PACK_SUB), :] for s in range(PACK_SUB)], axis=1)


def _layer_norm_rows(z, g, b):
    mu = jnp.mean(z, axis=-1, keepdims=True)
    zc = z - mu
    var = jnp.mean(zc * zc, axis=-1, keepdims=True)
    return zc * lax.rsqrt(var + LN_EPS) * g + b


def _out_ln_kernel(a_ref, w_ref, g_ref, b_ref, *refs, src_first):
    n_src = len(src_first)
    x_refs = refs[:n_src]
    o_ref, op_ref = refs[n_src:]
    i = pl.program_id(0)
    h = jnp.dot(a_ref[...], w_ref[...], preferred_element_type=F32)

    def finish(x_ref):
        y = _layer_norm_rows(ALPHA * x_ref[...] + h, g_ref[...], b_ref[...])
        o_ref[...] = y
        _store_packed(op_ref, _pack_rows(y))

    if n_src == 1:
        finish(x_refs[0])
    else:
        for s in range(n_src):
            hi = src_first[s + 1] if s + 1 < n_src else None
            in_src = (i >= src_first[s]) if hi is None else ((i >= src_first[s]) & (i < hi))

            @pl.when(in_src)
            def _(s=s):
                finish(x_refs[s])


def _cast_rows_kernel(*refs, src_first):
    n_src = len(src_first)
    o_ref = refs[n_src]
    i = pl.program_id(0)
    for s in range(n_src):
        hi = src_first[s + 1] if s + 1 < n_src else None
        in_src = (i >= src_first[s]) if hi is None else ((i >= src_first[s]) & (i < hi))

        @pl.when(in_src)
        def _(s=s):
            o_ref[...] = refs[s][...].astype(o_ref.dtype)


def _cast_rows(sources):
    d = sources[0].shape[1]
    tm = 512
    src_first, src_specs, n_row_blocks = _row_sources(sources, tm, lambda i: i)
    return pl.pallas_call(
        functools.partial(_cast_rows_kernel, src_first=src_first),
        grid=(n_row_blocks,),
        in_specs=src_specs,
        out_specs=pl.BlockSpec((tm, d), lambda i: (i, 0)),
        out_shape=jax.ShapeDtypeStruct((n_row_blocks * tm, d), BF16),
        compiler_params=_cparams(("arbitrary",)),
        name="cast_rows",
    )(*sources)


def _out_ln(a, w, x_sources, g, b, name):
    m, kk = a.shape
    d = w.shape[1]
    tm = 512 if kk * d * 2 <= OUT_LN_BIG_TILE_WEIGHT_BYTES else 256
    row = pl.BlockSpec((tm, d), lambda i: (i, 0))
    vec = pl.BlockSpec((1, d), lambda i: (0, 0))
    src_first, src_specs, n_row_blocks = _row_sources(x_sources, tm, lambda i: i)
    assert n_row_blocks * tm == m
    return pl.pallas_call(
        functools.partial(_out_ln_kernel, src_first=src_first),
        grid=(m // tm,),
        in_specs=[pl.BlockSpec((tm, kk), lambda i: (i, 0)),
                  pl.BlockSpec((kk, d), lambda i: (0, 0), pipeline_mode=pl.Buffered(1)), vec, vec] + src_specs,
        out_specs=[row, pl.BlockSpec((tm * PACK_SUB, LANES), lambda i: (i, 0))],
        out_shape=[jax.ShapeDtypeStruct((m, d), F32), jax.ShapeDtypeStruct((m * PACK_SUB, LANES), U32)],
        compiler_params=_cparams(("arbitrary",)),
        name=name,
    )(a, w, g.reshape(1, d), b.reshape(1, d), *x_sources)


def _ret_kernel(lg_ref, cd_ref, q_ref, k_ref, v_ref, *rest, backward, segs_chunks):
    if backward:
        of_ref, g_ref, o_ref, state_ref, dmat_ref, qd_ref, kd_ref = rest
    else:
        o_ref, state_ref, dmat_ref, qd_ref, kd_ref = rest
    h = pl.program_id(0)
    c = pl.program_id(1)
    nc = pl.num_programs(1)
    cc = (nc - 1 - c) if backward else c
    lg = lg_ref[h]
    cs = RET_CHUNK

    @pl.when(c == 0)
    def _tables():
        ii = lax.broadcasted_iota(I32, (cs, cs), 0)
        jj = lax.broadcasted_iota(I32, (cs, cs), 1)
        if backward:
            mask = jj > ii
            dist = (jj - ii).astype(F32)
        else:
            mask = ii >= jj
            dist = (ii - jj).astype(F32)
        dmat_ref[...] = jnp.where(mask, jnp.exp(jnp.where(mask, dist, 0.0) * lg), 0.0)
        pos = lax.broadcasted_iota(I32, (cs, 1), 0).astype(F32)
        if backward:
            qd_ref[...] = jnp.exp((cs - pos) * lg)
            kd_ref[...] = jnp.exp(pos * lg)
        else:
            qd_ref[...] = jnp.exp((pos + 1.0) * lg)
            kd_ref[...] = jnp.exp((cs - 1.0 - pos) * lg)

    loc = _seg_local_block(cc, segs_chunks)
    if backward:
        per = None
        for first, p in segs_chunks:
            per = p if per is None else jnp.where(cc >= first, p, per)
        is_start = loc == per - 1
    else:
        is_start = loc == 0

    @pl.when(is_start)
    def _reset():
        state_ref[...] = jnp.zeros_like(state_ref)

    state = state_ref[...]
    n_sub = q_ref.shape[0] // cs
    for sub in (range(n_sub - 1, -1, -1) if backward else range(n_sub)):
        rows = pl.ds(sub * cs, cs)
        q = q_ref[rows, :]
        k = k_ref[rows, :]
        v = v_ref[rows, :]
        att = lax.dot_general(q, k, (((1,), (1,)), ((), ())), preferred_element_type=F32) * dmat_ref[...]
        inner = jnp.dot(att.astype(BF16), v, preferred_element_type=F32)
        cross = jnp.dot(q, state.astype(BF16), preferred_element_type=F32) * qd_ref[...]
        o = inner + cross
        kt = (k.astype(F32) * kd_ref[...]).T.astype(BF16)
        state = state * cd_ref[h] + jnp.dot(kt, v, preferred_element_type=F32)

        if backward:
            tot = of_ref[rows, :] + o
            mu = jnp.mean(tot, axis=-1, keepdims=True)
            tc = tot - mu
            var = jnp.mean(tc * tc, axis=-1, keepdims=True)
            on = tc * lax.rsqrt(var + LN_EPS)
            g = g_ref[rows, :].astype(F32)
            o_ref[rows, :] = ((g * (1.0 / (1.0 + jnp.exp(-g)))) * on).astype(o_ref.dtype)
        else:
            o_ref[rows, :] = o
    state_ref[...] = state


def _retention(proj, lg_f, lg_b, segs):
    t = proj.shape[0]
    cs = RET_CHUNK
    rb = RET_CHUNK * RET_GROUP
    nc = t // rb
    segs_chunks = [(start // rb, s // rb) for start, s in segs]
    qk_blocks = (RET_HEADS * RET_DK) // RET_DK
    v_blocks0 = (2 * RET_HEADS * RET_DK) // RET_DV
    g_blocks0 = v_blocks0 + RET_HEADS
    smem = pl.BlockSpec(memory_space=pltpu.SMEM)
    scratch = [pltpu.VMEM((RET_DK, RET_DV), F32), pltpu.VMEM((cs, cs), F32),
               pltpu.VMEM((cs, 1), F32), pltpu.VMEM((cs, 1), F32)]

    def run(backward, lg, extra):
        def cmap(c):
            return (nc - 1 - c) if backward else c
        in_specs = [smem, smem,
                    pl.BlockSpec((rb, RET_DK), lambda h, c: (cmap(c), h)),
                    pl.BlockSpec((rb, RET_DK), lambda h, c: (cmap(c), qk_blocks + h)),
                    pl.BlockSpec((rb, RET_DV), lambda h, c: (cmap(c), v_blocks0 + h))]
        args = [lg, jnp.exp(cs * lg), proj, proj, proj]
        if backward:
            in_specs += [pl.BlockSpec((rb, RET_DV), lambda h, c: (cmap(c), h)),
                         pl.BlockSpec((rb, RET_DV), lambda h, c: (cmap(c), g_blocks0 + h))]
            args += [extra, proj]
        return pl.pallas_call(
            functools.partial(_ret_kernel, backward=backward, segs_chunks=segs_chunks),
            grid=(RET_HEADS, nc),
            in_specs=in_specs,
            out_specs=pl.BlockSpec((rb, RET_DV), lambda h, c: (cmap(c), h)),
            out_shape=jax.ShapeDtypeStruct((t, RET_HEADS * RET_DV), BF16 if backward else F32),
            scratch_shapes=scratch,
            compiler_params=_cparams(("arbitrary", "arbitrary")),
            name="retention_bwd" if backward else "retention_fwd",
        )(*args)

    o_f = run(False, lg_f, None)
    return run(True, lg_b, o_f)


def _attn_kernel(lam_ref, q_ref, k_ref, v_ref, sg_ref, *rest, out_scale):
    o_ref = rest[-1]
    probs = []
    for c in range(2):
        q = q_ref[:, c * DIFF_DH:(c + 1) * DIFF_DH]
        k = k_ref[:, c * DIFF_DH:(c + 1) * DIFF_DH]
        st = lax.dot_general(k, q, (((1,), (1,)), ((), ())), preferred_element_type=F32)
        p = jnp.exp2(st - jnp.max(st, axis=0, keepdims=True))
        probs.append((p, jnp.sum(p, axis=0, keepdims=True)))
    (p0, den0), (p1, den1) = probs
    at = (p0 * (1.0 / den0) - p1 * (lam_ref[0] / den1)).astype(BF16)
    o = lax.dot_general(at, v_ref[...], (((0,), (0,)), ((), ())), preferred_element_type=F32)
    ms = jnp.mean(o * o, axis=-1, keepdims=True)
    o_ref[...] = (o * lax.rsqrt(ms + LN_EPS) * sg_ref[...] * out_scale).astype(o_ref.dtype)


def _diff_attention(proj, lam, subln_g, segs, lambda_init):
    nq_blocks = DIFF_HEADS
    t = proj.shape[0]
    out = None
    for start, s, n_seq in segs:
        tq = ATTN_Q_TILE
        assert s % tq == 0 and start % s == 0
        qb0 = start // tq
        kb0 = start // s
        nqt = s // tq
        in_specs = [pl.BlockSpec(memory_space=pltpu.SMEM),
                    pl.BlockSpec((tq, 2 * DIFF_DH), lambda b, h, i: (qb0 + b * nqt + i, h)),
                    pl.BlockSpec((s, 2 * DIFF_DH), lambda b, h, i: (kb0 + b, nq_blocks + h)),
                    pl.BlockSpec((s, DIFF_DV), lambda b, h, i: (kb0 + b, 2 * nq_blocks + h)),
                    pl.BlockSpec((1, DIFF_DV), lambda b, h, i: (0, 0))]
        args = [lam, proj, proj, proj, subln_g.reshape(1, DIFF_DV)]
        aliases = {}
        if out is not None:
            in_specs.append(pl.BlockSpec(memory_space=pl.ANY))
            args.append(out)
            aliases = {len(args) - 1: 0}
        out = pl.pallas_call(
            functools.partial(_attn_kernel, out_scale=1.0 - lambda_init),
            grid=(n_seq, DIFF_HEADS, nqt),
            in_specs=in_specs,
            out_specs=pl.BlockSpec((tq, DIFF_DV), lambda b, h, i: (qb0 + b * nqt + i, h)),
            out_shape=jax.ShapeDtypeStruct((t, DIFF_HEADS * DIFF_DV), BF16),
            input_output_aliases=aliases,
            compiler_params=_cparams(("arbitrary", "arbitrary", "arbitrary")),
            name=f"diff_attn_s{s}",
        )(*args)
    return out


def _first_max(cur, rows, n_rows):
    m = jnp.max(cur, axis=0, keepdims=True)
    idx = jnp.min(jnp.where(cur == m, rows, float(n_rows)), axis=0, keepdims=True)
    return m, idx


def _router_kernel(x_ref, wt_ref, b_ref, idx_ref, gate_ref, rank_ref, cnt_ref, carry_ref):
    i = pl.program_id(0)

    @pl.when(i == 0)
    def _init():
        carry_ref[...] = jnp.zeros_like(carry_ref)

    tn = x_ref.shape[0]
    x = x_ref[...]
    xh = x.astype(BF16)
    xl = (x - xh.astype(F32)).astype(BF16)
    w = wt_ref[...]
    wh = w.astype(BF16)
    wl = (w - wh.astype(F32)).astype(BF16)

    def nt(a, b):
        return lax.dot_general(a, b, (((1,), (1,)), ((), ())), preferred_element_type=F32)

    logits = nt(wh, xh) + nt(wh, xl) + nt(wl, xh)
    scores = 1.0 / (1.0 + jnp.exp(-logits))
    biased = scores + b_ref[...]

    rows_g = lax.broadcasted_iota(I32, (GROUP_SIZE, tn), 0).astype(F32)
    rows_e = lax.broadcasted_iota(I32, (N_EXPERTS, tn), 0).astype(F32)

    gscore = []
    for g in range(N_GROUPS):
        slab = biased[g * GROUP_SIZE:(g + 1) * GROUP_SIZE]
        m1, i1 = _first_max(slab, rows_g, GROUP_SIZE)
        m2 = jnp.max(jnp.where(rows_g == i1, NEG_INF, slab), axis=0, keepdims=True)
        gscore.append(m1 + m2)
    cur = jnp.concatenate(gscore, axis=0)
    rows_grp = lax.broadcasted_iota(I32, (N_GROUPS, tn), 0).astype(F32)
    gsel = jnp.zeros((N_GROUPS, tn), F32)
    for _ in range(TOPK_GROUPS):
        _, gi = _first_max(cur, rows_grp, N_GROUPS)
        pick = rows_grp == gi
        gsel = jnp.where(pick, 1.0, gsel)
        cur = jnp.where(pick, NEG_INF, cur)

    masked = jnp.concatenate(
        [jnp.where(gsel[g:g + 1] > 0.0, biased[g * GROUP_SIZE:(g + 1) * GROUP_SIZE], NEG_INF)
         for g in range(N_GROUPS)], axis=0)

    sel = jnp.zeros((N_EXPERTS, tn), F32)
    picks = []
    cur = masked
    for _ in range(TOP_K):
        _, ei = _first_max(cur, rows_e, N_EXPERTS)
        pick = rows_e == ei
        sel = jnp.where(pick, 1.0, sel)
        cur = jnp.where(pick, NEG_INF, cur)
        picks.append(ei)

    gsum = jnp.sum(scores * sel, axis=0, keepdims=True)

    ta = lax.broadcasted_iota(I32, (tn, tn), 0)
    tb = lax.broadcasted_iota(I32, (tn, tn), 1)
    before = jnp.where(ta < tb, 1.0, 0.0).astype(BF16)
    rank = jnp.dot(sel.astype(BF16), before, preferred_element_type=F32) + carry_ref[...]
    carry_ref[...] = carry_ref[...] + jnp.sum(sel, axis=1, keepdims=True)
    cnt_ref[...] = carry_ref[...]

    gates = []
    ranks = []
    for ei in picks:
        hit = rows_e == ei
        gates.append(jnp.sum(jnp.where(hit, scores, 0.0), axis=0, keepdims=True) / gsum * ROUTED_SCALE)
        ranks.append(jnp.sum(jnp.where(hit, rank, 0.0), axis=0, keepdims=True))
    idx_ref[...] = jnp.concatenate(picks, axis=0).astype(I32)
    gate_ref[...] = jnp.concatenate(gates, axis=0)
    rank_ref[...] = jnp.concatenate(ranks, axis=0).astype(I32)


def _router(x, w_router, b_router):
    t, d = x.shape
    tn = _pick_tile(t, (512, 256))
    kt = pl.BlockSpec((TOP_K, tn), lambda i: (0, i))
    return pl.pallas_call(
        _router_kernel,
        grid=(t // tn,),
        in_specs=[pl.BlockSpec((tn, d), lambda i: (i, 0)),
                  pl.BlockSpec((N_EXPERTS, d), lambda i: (0, 0)),
                  pl.BlockSpec((N_EXPERTS, 1), lambda i: (0, 0))],
        out_specs=[kt, kt, kt, pl.BlockSpec((N_EXPERTS, 1), lambda i: (0, 0))],
        out_shape=[jax.ShapeDtypeStruct((TOP_K, t), I32), jax.ShapeDtypeStruct((TOP_K, t), F32),
                   jax.ShapeDtypeStruct((TOP_K, t), I32), jax.ShapeDtypeStruct((N_EXPERTS, 1), F32)],
        scratch_shapes=[pltpu.VMEM((N_EXPERTS, 1), F32)],
        compiler_params=_cparams(("arbitrary",)),
        name="router",
    )(x, w_router.T, b_router.reshape(N_EXPERTS, 1))


def _slot_kernel(idx_ref, rank_ref, start_ref, slot_ref):
    tn = idx_ref.shape[1]
    rows_e = lax.broadcasted_iota(I32, (N_EXPERTS, tn), 0)
    out = []
    for k in range(TOP_K):
        hit = rows_e == idx_ref[k:k + 1, :]
        base = jnp.sum(jnp.where(hit, start_ref[...], 0.0), axis=0, keepdims=True)
        out.append(base.astype(I32) + rank_ref[k:k + 1, :])
    slot_ref[...] = jnp.concatenate(out, axis=0)


def _slots(idx_t, rank_t, pad_start):
    t = idx_t.shape[1]
    tn = _pick_tile(t, (512, 256))
    kt = pl.BlockSpec((TOP_K, tn), lambda i: (0, i))
    return pl.pallas_call(
        _slot_kernel,
        grid=(t // tn,),
        in_specs=[kt, kt, pl.BlockSpec((N_EXPERTS, 1), lambda i: (0, 0))],
        out_specs=kt,
        out_shape=jax.ShapeDtypeStruct((TOP_K, t), I32),
        compiler_params=_cparams(("arbitrary",)),
        name="slots",
    )(idx_t, rank_t, pad_start.astype(F32).reshape(N_EXPERTS, 1))


def _row_copy(src, src_row, dst, dst_row, sem):
    return pltpu.make_async_copy(src.at[_tile_of(src_row)], dst.at[_tile_of(dst_row)], sem)


def _dispatch_kernel(pad_end_ref, padded_ref, slot_hbm, x_ref, xs_hbm, slot_smem, zero_ref, sem, slot_sem):
    i = pl.program_id(0)
    tt = x_ref.shape[0] // PACK_SUB
    bm = zero_ref.shape[0] // PACK_SUB
    slot_cp = pltpu.make_async_copy(slot_hbm.at[i], slot_smem, slot_sem)
    slot_cp.start()

    @pl.when(i == 0)
    def _clear():
        zero_ref[...] = jnp.zeros_like(zero_ref)

        def pad_copy(e):
            off = pl.multiple_of((pad_end_ref[e] - bm) * PACK_SUB, bm * PACK_SUB)
            return pltpu.make_async_copy(zero_ref, xs_hbm.at[pl.ds(off, bm * PACK_SUB)], sem)

        def start(e, carry):
            @pl.when(padded_ref[e] > 0)
            def _():
                pad_copy(e).start()
            return carry

        def wait(e, carry):
            @pl.when(padded_ref[e] > 0)
            def _():
                pad_copy(e).wait()
            return carry

        lax.fori_loop(0, N_EXPERTS, start, 0)
        lax.fori_loop(0, N_EXPERTS, wait, 0)

    slot_cp.wait()
    per_row = LANES // TOP_K

    def issue(r, carry):
        for j in range(LANES):
            _row_copy(x_ref, r * per_row + j // TOP_K, xs_hbm, slot_smem[r, j], sem).start(priority=j % 2)
        return carry

    lax.fori_loop(0, tt // per_row, issue, 0)

    def drain(t, carry):
        for k in range(TOP_K):
            _row_copy(x_ref, t, xs_hbm, 0, sem).wait()
        return carry

    lax.fori_loop(0, tt, drain, 0)


def _dispatch(xp, slot_tiles, pad_end, padded, n_slots):
    t = xp.shape[0] // PACK_SUB
    tt = DISPATCH_TILE
    grid_spec = pltpu.PrefetchScalarGridSpec(
        num_scalar_prefetch=2,
        grid=(t // tt,),
        in_specs=[pl.BlockSpec(memory_space=pl.ANY),
                  pl.BlockSpec((tt * PACK_SUB, LANES), lambda i, pe, pd: (i, 0))],
        out_specs=pl.BlockSpec(memory_space=pl.ANY),
        scratch_shapes=[pltpu.SMEM((tt * TOP_K // LANES, LANES), I32),
                        pltpu.VMEM((EXPERT_ROWS * PACK_SUB, LANES), U32),
                        pltpu.SemaphoreType.DMA, pltpu.SemaphoreType.DMA],
    )
    return pl.pallas_call(
        _dispatch_kernel,
        grid_spec=grid_spec,
        out_shape=jax.ShapeDtypeStruct((n_slots * PACK_SUB, LANES), U32),
        compiler_params=_cparams(("arbitrary",)),
        name="dispatch",
    )(pad_end, padded, slot_tiles, xp)


def _ffn_kernel(block_e_ref, n_used_ref, x_ref, wgu_ref, wd_ref, o_ref, wgu_bf, wd_bf, *, pack_out):
    b = pl.program_id(0)

    @pl.when(b < n_used_ref[0])
    def _():
        e = block_e_ref[b]
        e_prev = block_e_ref[jnp.maximum(b - 1, 0)]

        @pl.when((b == 0) | (e != e_prev))
        def _():
            wgu_bf[...] = wgu_ref[0, 0].astype(BF16)
            wd_bf[...] = wd_ref[0, 0].astype(BF16)

        hidden = wd_bf.shape[0]
        rows = x_ref.shape[0] // PACK_SUB
        lo, hi = _unpack_rows(_load_packed(x_ref, rows))
        gu = (jnp.dot(lo.astype(BF16), wgu_bf[:PACK_WORDS], preferred_element_type=F32)
              + jnp.dot(hi.astype(BF16), wgu_bf[PACK_WORDS:], preferred_element_type=F32))
        gate = gu[:, :hidden]
        act = gate * (1.0 / (1.0 + jnp.exp(-gate))) * gu[:, hidden:]
        y = jnp.dot(act.astype(BF16), wd_bf[...], preferred_element_type=F32)
        if pack_out:
            _store_packed(o_ref, _pack_rows(y))
        else:
            o_ref[...] = y


def _ffn(xp, w_gu, w_down, layer, block_e, n_used, rows, pack_out, name):
    n = xp.shape[0] // PACK_SUB
    d = 2 * PACK_WORDS
    hidden = w_down.shape[2]

    def row_map(b, be, nu):
        return (jnp.minimum(b, nu[0] - 1), 0)

    packed_rows = pl.BlockSpec((rows * PACK_SUB, LANES), row_map)
    if pack_out:
        out_spec, out_shape = packed_rows, jax.ShapeDtypeStruct((n * PACK_SUB, LANES), U32)
    else:
        out_spec, out_shape = pl.BlockSpec((rows, d), row_map), jax.ShapeDtypeStruct((n, d), F32)
    grid_spec = pltpu.PrefetchScalarGridSpec(
        num_scalar_prefetch=2,
        grid=(n // rows,),
        in_specs=[packed_rows,
                  pl.BlockSpec((1, 1, d, 2 * hidden), lambda b, be, nu: (layer, be[b], 0, 0)),
                  pl.BlockSpec((1, 1, hidden, d), lambda b, be, nu: (layer, be[b], 0, 0))],
        out_specs=out_spec,
        scratch_shapes=[pltpu.VMEM((d, 2 * hidden), BF16), pltpu.VMEM((hidden, d), BF16)],
    )
    return pl.pallas_call(
        functools.partial(_ffn_kernel, pack_out=pack_out),
        grid_spec=grid_spec,
        out_shape=out_shape,
        compiler_params=_cparams(("arbitrary",)),
        name=name,
    )(block_e, n_used, xp, w_gu, w_down)


def _combine_kernel(slot_hbm, x_ref, sh_ref, gate_ref, g_ref, b_ref, ys_hbm, o_ref, *rest, tile0, with_bf16):
    if with_bf16:
        ob_ref, *rest = rest
    slots0, slots1, buf0, buf1, sems, slot_sem = rest
    slot_smem = (slots0, slots1)
    bufs = (buf0, buf1)
    i = pl.program_id(0)
    n = pl.num_programs(0)
    tt = x_ref.shape[0]
    per_row = LANES // TOP_K
    groups = tt // per_row

    def load_slots(tile, par):
        slot_cp = pltpu.make_async_copy(slot_hbm.at[tile0 + tile], slot_smem[par], slot_sem)
        slot_cp.start()
        slot_cp.wait()

    def issue_group(r, par):
        for j in range(LANES):
            _row_copy(ys_hbm, slot_smem[par][r, j], bufs[par].at[j % TOP_K], r * per_row + j // TOP_K,
                      sems.at[par]).start(priority=j % 2)

    def drain(par):
        def body(t, carry):
            for k in range(TOP_K):
                _row_copy(ys_hbm, 0, bufs[par].at[k], t, sems.at[par]).wait()
            return carry
        lax.fori_loop(0, tt, body, 0)

    def reduce_group(r, par):
        rows = slice(r * per_row, (r + 1) * per_row)
        sh = sh_ref[rows, :]
        h_lo = sh[:, :PACK_WORDS]
        h_hi = sh[:, PACK_WORDS:]
        gate = gate_ref[rows, :]
        for k in range(TOP_K):
            words = jnp.concatenate(
                [bufs[par][k, pl.ds(r * per_row * PACK_SUB + s, per_row, stride=PACK_SUB), :]
                 for s in range(PACK_SUB)], axis=1)
            lo, hi = _unpack_rows(words)
            h_lo = h_lo + lo * gate[:, k:k + 1]
            h_hi = h_hi + hi * gate[:, k:k + 1]
        h = jnp.concatenate([h_lo, h_hi], axis=1)
        y = _layer_norm_rows(ALPHA * x_ref[rows, :] + h, g_ref[...], b_ref[...])
        o_ref[rows, :] = y
        if with_bf16:
            ob_ref[rows, :] = y.astype(BF16)

    @pl.when(i == 0)
    def _():
        load_slots(0, 0)
        for r in range(groups):
            issue_group(r, 0)

    nxt = jnp.minimum(i + 1, n - 1)
    for par in range(2):
        @pl.when(i % 2 == par)
        def _(par=par):
            drain(par)
            load_slots(nxt, 1 - par)
            for r in range(groups):
                issue_group(r, 1 - par)
                reduce_group(r, par)

            @pl.when(i == n - 1)
            def _():
                drain(1 - par)


def _combine(x, shared, ys, slot_tiles, gate, g, b, row_range, with_bf16):
    d = x.shape[1]
    tt = COMBINE_TILE
    r0, r1 = row_range
    tile0 = r0 // tt
    row_in = pl.BlockSpec((tt, d), lambda i: (tile0 + i, 0))
    row_out = pl.BlockSpec((tt, d), lambda i: (i, 0))
    vec = pl.BlockSpec((1, d), lambda i: (0, 0))
    out_specs = [row_out]
    out_shape = [jax.ShapeDtypeStruct((r1 - r0, d), F32)]
    if with_bf16:
        out_specs.append(row_out)
        out_shape.append(jax.ShapeDtypeStruct((r1 - r0, d), BF16))
    return pl.pallas_call(
        functools.partial(_combine_kernel, tile0=tile0, with_bf16=with_bf16),
        grid=((r1 - r0) // tt,),
        in_specs=[pl.BlockSpec(memory_space=pl.ANY), row_in, row_in,
                  pl.BlockSpec((tt, TOP_K), lambda i: (tile0 + i, 0)), vec, vec,
                  pl.BlockSpec(memory_space=pl.ANY)],
        out_specs=out_specs,
        out_shape=out_shape,
        scratch_shapes=[pltpu.SMEM((tt * TOP_K // LANES, LANES), I32), pltpu.SMEM((tt * TOP_K // LANES, LANES), I32),
                        pltpu.VMEM((TOP_K, tt * PACK_SUB, LANES), U32), pltpu.VMEM((TOP_K, tt * PACK_SUB, LANES), U32),
                        pltpu.SemaphoreType.DMA((2,)), pltpu.SemaphoreType.DMA],
        compiler_params=_cparams(("arbitrary",)),
        name="combine",
    )(slot_tiles, x, shared, gate, g.reshape(1, d), b.reshape(1, d), ys)


def _moe_layer(x, xp, layer, w_router, b_router, w_gu, w_down, sw_gu, sw_down, ln_g, ln_b, split_rows):
    t, d = x.shape
    bm = EXPERT_ROWS
    idx_t, gate_t, rank_t, counts = _router(x, w_router, b_router)

    counts = counts.reshape(N_EXPERTS).astype(I32)
    padded = (counts + bm - 1) // bm * bm
    pad_end = jnp.cumsum(padded)
    pad_start = pad_end - padded
    n_blocks = (t * TOP_K + N_EXPERTS * (bm - 1) + bm - 1) // bm
    n_slots = n_blocks * bm
    block_first = jnp.arange(n_blocks, dtype=I32) * bm
    block_e = jnp.minimum(jnp.sum((pad_end[None, :] <= block_first[:, None]).astype(I32), axis=1), N_EXPERTS - 1)
    n_used = (pad_end[-1:] // bm).astype(I32)

    slot_t = _slots(idx_t, rank_t, pad_start)
    slot_flat = slot_t.T.reshape(t * TOP_K // LANES, LANES)

    def slot_tiles(tile):
        return slot_flat.reshape(t // tile, tile * TOP_K // LANES, LANES)

    xs = _dispatch(xp, slot_tiles(DISPATCH_TILE), pad_end.astype(I32), padded.astype(I32), n_slots)
    ys = _ffn(xs, w_gu, w_down, layer, block_e, n_used, bm, True, "expert_ffn")
    shared_rows = _pick_tile(t, (512, 256))
    shared = _ffn(xp, sw_gu[:, None], sw_down[:, None], layer, jnp.zeros((t // shared_rows,), I32),
                  jnp.full((1,), t // shared_rows, I32), shared_rows, False, "shared_ffn")
    args = (x, shared, ys, slot_tiles(COMBINE_TILE), gate_t.T, ln_g, ln_b)
    if split_rows is None:
        return _combine(*args, (0, t), True)
    return _combine(*args, (0, split_rows), False)[0], _combine(*args, (split_rows, t), False)[0]


def _rope_angles(s, theta, half):
    inv = theta ** (-jnp.arange(half, dtype=F32) / half)
    ang = jnp.arange(s, dtype=F32)[:, None] * inv[None, :]
    return jnp.cos(ang), jnp.sin(ang)


def _ret_tables(s):
    cos, sin = _rope_angles(s, RET_THETA, RET_DK // 2)
    return cos, sin, sin


def _diff_tables(s):
    half = ROPE_DIMS // 2
    cos, sin = _rope_angles(s, ROPE_THETA, half)
    rest = DIFF_DH - ROPE_DIMS
    zeros_h = jnp.zeros((s, half), F32)
    c = jnp.concatenate([cos, cos, jnp.ones((s, rest), F32)], axis=1)
    s_up = jnp.concatenate([zeros_h, sin, jnp.zeros((s, rest), F32)], axis=1)
    s_dn = jnp.concatenate([-sin, zeros_h, jnp.zeros((s, rest), F32)], axis=1)
    return c, s_up, s_dn


def kernel(x_prompt, x_sample, ret_w_in, ret_decay_f, ret_decay_b, ret_w_out, diff_w_in, diff_lam_q1, diff_lam_k1,
           diff_lam_q2, diff_lam_k2, diff_subln_g, diff_w_out, ln_mix_g, ln_mix_b, router_w, router_b, exp_w_gu,
           exp_w_down, shared_w_gu, shared_w_down, ln_ffn_g, ln_ffn_b):
    bp, sp, d = x_prompt.shape
    bs, ss, _ = x_sample.shape
    tp = bp * sp
    x_rows = [x_prompt.reshape(tp, d), x_sample.reshape(bs * ss, d)]
    proj_rows = [_cast_rows(x_rows)]
    segs = [(0, sp), (tp, ss)]
    s_max = max(sp, ss)

    for i in range(DEPTH):
        j = i // 2
        if i % 2 == 0:
            qk = RET_HEADS * RET_DK
            proj = _proj(proj_rows, ret_w_in[j].astype(BF16), _ret_tables(s_max), segs, mode="ret",
                         n_q_cols=qk, n_qk_cols=2 * qk, q_scale=1.0, k_scale=RET_DK ** -0.5)
            lg_f = -jax.nn.softplus(-ret_decay_f[j].astype(F32))
            lg_b = -jax.nn.softplus(-ret_decay_b[j].astype(F32))
            mixed = _retention(proj, lg_f, lg_b, segs)
            x, xp = _out_ln(mixed, ret_w_out[j].astype(BF16), x_rows, ln_mix_g[i], ln_mix_b[i], "ret_out")
        else:
            lambda_init = 0.8 - 0.6 * math.exp(-0.3 * i)
            nq = DIFF_HEADS * 2 * DIFF_DH
            proj = _proj(proj_rows, diff_w_in[j].astype(BF16), _diff_tables(s_max), segs, mode="diff",
                         n_q_cols=nq, n_qk_cols=2 * nq, q_scale=DIFF_DH ** -0.5 * LOG2E, k_scale=1.0)
            lam = (jnp.exp(jnp.sum(diff_lam_q1[j].astype(F32) * diff_lam_k1[j].astype(F32)))
                   - jnp.exp(jnp.sum(diff_lam_q2[j].astype(F32) * diff_lam_k2[j].astype(F32))) + lambda_init)
            mixed = _diff_attention(proj, lam.reshape(1), diff_subln_g[j],
                                    [(0, sp, bp), (tp, ss, bs)], lambda_init)
            x, xp = _out_ln(mixed, diff_w_out[j].astype(BF16), x_rows, ln_mix_g[i], ln_mix_b[i], "diff_out")
        last = i == DEPTH - 1
        x, xb = _moe_layer(x, xp, i, router_w[i], router_b[i], exp_w_gu, exp_w_down, shared_w_gu, shared_w_down,
                           ln_ffn_g[i], ln_ffn_b[i], tp if last else None)
        x_rows, proj_rows = [x], [xb]
    return x.reshape(bp, sp, d), xb.reshape(bs, ss, d)
```
